```python
import math
import jax, jax.numpy as jnp
from jax import lax
import numpy as np

D_MODEL = 1024
BATCH = 8
SEQ = 4096
DEPTH = 1

CHUNK = 64
Q_BLOCK = 128
HEAD_DIM = 64
ROT_DIM = HEAD_DIM // 4
ROPE_THETA = 500000.0
RMS_EPS = 1e-6
A_HEADS = 4
A_QK_DIM = HEAD_DIM
A_V_DIM = 2 * HEAD_DIM
A_WIDTH = A_HEADS * A_V_DIM
B_HEADS = 8
B_QK_DIM = HEAD_DIM
B_LATENT = 128
B_OUT_DIM = HEAD_DIM
B_WIDTH = B_HEADS * B_OUT_DIM
IDX_HEADS = 8
IDX_DIM = 64
IDX_TOPK_MAX = 256
MIX_WIDTH = A_WIDTH + B_WIDTH
IN_SIZES = (A_HEADS * A_QK_DIM, A_HEADS * A_QK_DIM, A_HEADS * A_QK_DIM, A_HEADS * A_QK_DIM,
            A_HEADS * A_V_DIM,
            B_HEADS * B_QK_DIM, B_QK_DIM, B_LATENT, IDX_HEADS * IDX_DIM, IDX_DIM, IDX_HEADS)
N_IN = sum(IN_SIZES)
N_EXPERTS = 32
TOP_K = 4
D_FF = D_MODEL
SWIGLU_LIMIT = 7.0
SWIGLU_ALPHA = 1.702
EXPERT_BLOCK = 256

kernel_name = "hybrid_diffattn_dsa_moe_block"


def rms_norm(x, g):
    xf = x.astype(jnp.float32)
    y = xf * lax.rsqrt(jnp.mean(xf * xf, axis=-1, keepdims=True) + RMS_EPS)
    return (y * g.astype(jnp.float32)).astype(x.dtype)


def rope_tables(S):
    pos = jnp.arange(S, dtype=jnp.float32)
    inv = ROPE_THETA ** (-jnp.arange(0, ROT_DIM, 2, dtype=jnp.float32) / ROT_DIM)
    ang = pos[:, None] * inv[None, :]
    return jnp.cos(ang), jnp.sin(ang)


def apply_partial_rope(x, cos, sin):
    half = ROT_DIM // 2
    shape = (1, x.shape[1]) + (1,) * (x.ndim - 3) + (half,)
    cs, sn = cos.reshape(shape), sin.reshape(shape)
    xr = x[..., :ROT_DIM].astype(jnp.float32)
    x1, x2 = xr[..., :half], xr[..., half:]
    rot = jnp.concatenate([x1 * cs - x2 * sn, x2 * cs + x1 * sn], axis=-1)
    return jnp.concatenate([rot.astype(x.dtype), x[..., ROT_DIM:]], axis=-1)


def to_blocks(t):
    B, S = t.shape[:2]
    return jnp.moveaxis(t.reshape((B, S // Q_BLOCK, Q_BLOCK) + t.shape[2:]), 1, 0)


def from_blocks(t):
    nb, B, qb = t.shape[:3]
    return jnp.moveaxis(t, 0, 1).reshape((B, nb * qb) + t.shape[3:])


def diff_attention(q1, q2, k1, k2, v, lam_vecs, sub_g, lambda_init):
    B, S = q1.shape[:2]
    lv = lam_vecs.astype(jnp.float32)
    lam = jnp.exp(jnp.sum(lv[0] * lv[1])) - jnp.exp(jnp.sum(lv[2] * lv[3])) + lambda_init
    key_chunk = jnp.arange(S) // CHUNK
    scale = A_QK_DIM ** -0.5

    def block(args):
        q1b, q2b, i = args
        q_chunk = (i * Q_BLOCK + jnp.arange(Q_BLOCK)) // CHUNK
        allowed = key_chunk[None, :] <= q_chunk[:, None]

        def probs(qb, k):
            s = jnp.einsum('bqhd,bshd->bhqs', qb, k).astype(jnp.float32) * scale
            return jax.nn.softmax(jnp.where(allowed, s, -jnp.inf), axis=-1)

        a = probs(q1b, k1) - lam * probs(q2b, k2)
        return jnp.einsum('bhqs,bshe->bqhe', a.astype(v.dtype), v)

    nqb = S // Q_BLOCK
    o = from_blocks(lax.map(block, (to_blocks(q1), to_blocks(q2), jnp.arange(nqb))))
    o = rms_norm(o, sub_g) * (1.0 - lambda_init)
    return o.reshape(B, S, A_WIDTH)


def dsa_attention(q, k, lat, iq, ik, iw, w_uv):
    B, S = q.shape[:2]
    topk = min(IDX_TOPK_MAX, S // 4)
    key_chunk = jnp.arange(S) // CHUNK
    scale = B_QK_DIM ** -0.5
    iw = iw.astype(jnp.float32) * (IDX_HEADS ** -0.5) * (IDX_DIM ** -0.5)
    gather = jax.vmap(lambda arr, idx: arr[idx])

    def block(args):
        qb, iqb, iwb, i = args
        q_chunk = (i * Q_BLOCK + jnp.arange(Q_BLOCK)) // CHUNK
        allowed = key_chunk[None, :] <= q_chunk[:, None]
        rel = jax.nn.relu(jnp.einsum('bqhd,bsd->bqhs', iqb, ik).astype(jnp.float32))
        score = jnp.einsum('bqhs,bqh->bqs', rel, iwb)
        score = jnp.where(allowed[None], score, -jnp.inf)
        _, idx = lax.top_k(score, topk)
        valid = (idx // CHUNK) <= q_chunk[None, :, None]
        flat = idx.reshape(B, Q_BLOCK * topk)
        k_sel = gather(k, flat).reshape(B, Q_BLOCK, topk, B_QK_DIM)
        v_sel = gather(lat, flat).reshape(B, Q_BLOCK, topk, B_LATENT)
        s = jnp.einsum('bqhd,bqkd->bqhk', qb, k_sel).astype(jnp.float32) * scale
        p = jax.nn.softmax(jnp.where(valid[:, :, None, :], s, -jnp.inf), axis=-1)
        return jnp.einsum('bqhk,bqkc->bqhc', p.astype(v_sel.dtype), v_sel)

    nqb = S // Q_BLOCK
    o = from_blocks(lax.map(block, (to_blocks(q), to_blocks(iq), to_blocks(iw), jnp.arange(nqb))))
    o = jnp.einsum('bshc,hcd->bshd', o, w_uv)
    return o.reshape(B, S, B_WIDTH)


def moe(h, router_w, router_b, w_gate, b_gate, w_up, b_up, w_down, b_down):
    B, S, D = h.shape
    T = B * S
    P = T * TOP_K
    G = EXPERT_BLOCK
    hf = h.reshape(T, D)
    logits = (hf @ router_w + router_b).astype(jnp.float32)
    top_vals, top_idx = lax.top_k(logits, TOP_K)
    gates = jax.nn.softmax(top_vals, axis=-1)
    e_flat = top_idx.reshape(P)
    g_flat = gates.reshape(P)
    tok = jnp.arange(P) // TOP_K
    order = jnp.argsort(e_flat, stable=True)
    e_s, tok_s, g_s = e_flat[order], tok[order], g_flat[order]
    counts = jnp.bincount(e_flat, length=N_EXPERTS)
    padded = ((counts + G - 1) // G) * G
    off = jnp.cumsum(counts) - counts
    pcum = jnp.cumsum(padded)
    poff = pcum - padded
    dest = poff[e_s] + (jnp.arange(P) - off[e_s])
    nb = -(-P // G) + N_EXPERTS
    xbuf = jnp.zeros((nb * G, D), hf.dtype).at[dest].set(hf[tok_s])
    block_expert = jnp.minimum(jnp.searchsorted(pcum, jnp.arange(nb) * G, side='right'), N_EXPERTS - 1)

    def expert_block(args):
        xb, e = args
        g = xb @ w_gate[e] + b_gate[e]
        u = xb @ w_up[e] + b_up[e]
        g = jnp.minimum(g, SWIGLU_LIMIT)
        u = jnp.clip(u, -SWIGLU_LIMIT, SWIGLU_LIMIT)
        a = g * jax.nn.sigmoid(SWIGLU_ALPHA * g) * (u + 1.0)
        return a @ w_down[e] + b_down[e]

    ybuf = lax.map(expert_block, (xbuf.reshape(nb, G, D), block_expert)).reshape(nb * G, D)
    contrib = ybuf[dest].astype(jnp.float32) * g_s[:, None]
    y = jax.ops.segment_sum(contrib, tok_s, num_segments=T)
    return y.astype(h.dtype).reshape(B, S, D)


def setup_inputs(seed: int = 0) -> dict:
    key = jax.random.key(seed)
    ks = jax.random.split(key, 24)

    def nrm(k, shape, scale):
        return jax.random.normal(k, shape, jnp.float32) * scale

    D = D_MODEL
    return {
        "x": nrm(ks[0], (BATCH, SEQ, D), 1.0),
        "c": nrm(ks[1], (BATCH, D), 1.0),
        "norm1_g": 1.0 + nrm(ks[2], (DEPTH, D), 0.02),
        "norm2_g": 1.0 + nrm(ks[3], (DEPTH, D), 0.02),
        "w_ada": nrm(ks[4], (DEPTH, D, 6 * D), D ** -0.5),
        "b_ada": nrm(ks[5], (DEPTH, 6 * D), 0.02),
        "w_in": nrm(ks[6], (DEPTH, D, N_IN), D ** -0.5),
        "w_out": nrm(ks[7], (DEPTH, MIX_WIDTH, D), MIX_WIDTH ** -0.5),
        "a_q_norm_g": 1.0 + nrm(ks[8], (DEPTH, A_QK_DIM), 0.02),
        "a_k_norm_g": 1.0 + nrm(ks[9], (DEPTH, A_QK_DIM), 0.02),
        "a_lambda": nrm(ks[10], (DEPTH, 4, A_QK_DIM), 0.1),
        "a_sub_g": 1.0 + nrm(ks[11], (DEPTH, A_V_DIM), 0.02),
        "b_q_norm_g": 1.0 + nrm(ks[12], (DEPTH, B_QK_DIM), 0.02),
        "b_k_norm_g": 1.0 + nrm(ks[13], (DEPTH, B_QK_DIM), 0.02),
        "b_kv_norm_g": 1.0 + nrm(ks[14], (DEPTH, B_LATENT), 0.02),
        "b_w_uv": nrm(ks[15], (DEPTH, B_HEADS, B_LATENT, B_OUT_DIM), B_LATENT ** -0.5),
        "router_w": nrm(ks[16], (DEPTH, D, N_EXPERTS), D ** -0.5),
        "router_b": nrm(ks[17], (DEPTH, N_EXPERTS), 0.01),
        "w_gate": nrm(ks[18], (DEPTH, N_EXPERTS, D, D_FF), D ** -0.5),
        "b_gate": nrm(ks[19], (DEPTH, N_EXPERTS, D_FF), 0.01),
        "w_up": nrm(ks[20], (DEPTH, N_EXPERTS, D, D_FF), D ** -0.5),
        "b_up": nrm(ks[21], (DEPTH, N_EXPERTS, D_FF), 0.01),
        "w_down": nrm(ks[22], (DEPTH, N_EXPERTS, D_FF, D), D_FF ** -0.5),
        "b_down": nrm(ks[23], (DEPTH, N_EXPERTS, D), 0.01),
    }


def reference(x, c, norm1_g, norm2_g, w_ada, b_ada, w_in, w_out, a_q_norm_g, a_k_norm_g,
              a_lambda, a_sub_g, b_q_norm_g, b_k_norm_g, b_kv_norm_g, b_w_uv, router_w, router_b,
              w_gate, b_gate, w_up, b_up, w_down, b_down):
    B, S, D = x.shape
    cos, sin = rope_tables(S)
    split_pts = [int(v) for v in np.cumsum(IN_SIZES)[:-1]]
    for l in range(DEPTH):
        lambda_init = 0.8 - 0.6 * math.exp(-0.3 * l)
        mod = jax.nn.silu(c) @ w_ada[l] + b_ada[l]
        sh1, sc1, g1, sh2, sc2, g2 = jnp.split(mod[:, None, :], 6, axis=-1)

        h = rms_norm(x, norm1_g[l]) * (1.0 + sc1) + sh1
        proj = h @ w_in[l]
        aq1, aq2, ak1, ak2, av, bq, bk, blat, iq, ik, iw = jnp.split(proj, split_pts, axis=-1)

        def a_heads(t, g):
            t = rms_norm(t.reshape(B, S, A_HEADS, A_QK_DIM), g)
            return apply_partial_rope(t, cos, sin)

        a_out = diff_attention(a_heads(aq1, a_q_norm_g[l]), a_heads(aq2, a_q_norm_g[l]),
                               a_heads(ak1, a_k_norm_g[l]), a_heads(ak2, a_k_norm_g[l]),
                               av.reshape(B, S, A_HEADS, A_V_DIM), a_lambda[l], a_sub_g[l],
                               lambda_init)

        bq = apply_partial_rope(rms_norm(bq.reshape(B, S, B_HEADS, B_QK_DIM), b_q_norm_g[l]), cos, sin)
        bk = apply_partial_rope(rms_norm(bk, b_k_norm_g[l]), cos, sin)
        blat = rms_norm(blat, b_kv_norm_g[l])
        iq = apply_partial_rope(iq.reshape(B, S, IDX_HEADS, IDX_DIM), cos, sin)
        ik = apply_partial_rope(ik, cos, sin)
        b_out = dsa_attention(bq, bk, blat, iq, ik, iw, b_w_uv[l])

        mix = jnp.concatenate([a_out, b_out], axis=-1) @ w_out[l]
        x = x + g1 * mix

        h2 = rms_norm(x, norm2_g[l]) * (1.0 + sc2) + sh2
        x = x + g2 * moe(h2, router_w[l], router_b[l], w_gate[l], b_gate[l], w_up[l], b_up[l],
                         w_down[l], b_down[l])
    return x
```

```python
import functools
import math

import numpy as np
import jax
import jax.numpy as jnp
from jax import lax
from jax.experimental import pallas as pl
from jax.experimental.pallas import tpu as pltpu

F32 = jnp.float32
BF16 = jnp.bfloat16
I32 = jnp.int32

CHUNK = 64
HEAD_DIM = 64
ROT_DIM = HEAD_DIM // 4
ROPE_THETA = 500000.0
RMS_EPS = 1e-6
A_HEADS = 4
A_V_DIM = 2 * HEAD_DIM
B_HEADS = 8
B_LATENT = 128
IDX_HEADS = 8
IDX_TOPK_MAX = 256
N_EXPERTS = 32
TOP_K = 4
SWIGLU_LIMIT = 7.0
SWIGLU_ALPHA = 1.702

LANES = 128
INT_MIN = -(2 ** 31)
NEG_BIG = -1e30
VMEM_LIMIT = 56 * 1024 * 1024

NORM_SLACK = 1.01
PLAIN_EXP_MAX_BOUND = 30.0

PROJ_ROWS = 256
ATT_TQ = 256
ATT_TK = 512
DSA_TQ = 128
DSA_TK = 512
OUT_ROWS = 512
MOE_ROWS = 512
MOVE_ROWS = 256


def _cparams(n_axes):
    return pltpu.CompilerParams(
        dimension_semantics=("arbitrary",) * n_axes, vmem_limit_bytes=VMEM_LIMIT)


def _dot_nt(a, b):
    return lax.dot_general(a, b, (((1,), (1,)), ((), ())), preferred_element_type=F32)


def _ada_kernel(c_ref, w_ref, b_ref, o_ref):
    c = c_ref[...]
    sc = c / (1.0 + jnp.exp(-c))
    o_ref[...] = jnp.dot(sc, w_ref[...], preferred_element_type=F32,
                         precision=lax.Precision.HIGHEST) + b_ref[...]


def _ada(c, w, b):
    B, D = c.shape
    N = w.shape[1]
    return pl.pallas_call(
        _ada_kernel,
        grid=(N // D,),
        in_specs=[pl.BlockSpec((B, D), lambda j: (0, 0)),
                  pl.BlockSpec((D, D), lambda j: (0, j)),
                  pl.BlockSpec((1, D), lambda j: (0, j))],
        out_specs=pl.BlockSpec((B, D), lambda j: (0, j)),
        out_shape=jax.ShapeDtypeStruct((B, N), F32),
        compiler_params=_cparams(1),
    )(c, w, b.reshape(1, N))


C_QK = 0
C_BQ = 1024
C_BKIK = 1536
C_IQ = 1664
C_LAT = 2176
C_AV = 2304
C_IW = 2816
C_END = 2944
N_GAIN = C_IQ


def _group_sumsq(p, bd):
    sq = p * p
    hi = sq.astype(BF16)
    lo = (sq - hi.astype(F32)).astype(BF16)
    return (jnp.dot(hi, bd, preferred_element_type=F32)
            + jnp.dot(lo, bd, preferred_element_type=F32))


def _rope(y, c, s1, s2):
    w = y.shape[1]
    return y * c + pltpu.roll(y, w - ROT_DIM // 2, 1) * s1 + pltpu.roll(y, ROT_DIM // 2, 1) * s2


def _proj_kernel(x_ref, sh_ref, sc_ref, g_ref, w_ref, gain_ref, kb_ref, latg_ref, rc_ref, rs1_ref,
                 rs2_ref, bd_ref,
                 aq1_ref, aq2_ref, ak1_ref, ak2_ref, bq_ref, bk_ref, ik_ref, iq_ref, lat_ref,
                 av_ref, iw_ref):
    x = x_ref[0]
    ms = jnp.mean(x * x, axis=-1, keepdims=True)
    h = x * lax.rsqrt(ms + RMS_EPS) * g_ref[...]
    h = (h * (1.0 + sc_ref[0]) + sh_ref[0]).astype(BF16)

    rc, rs1, rs2 = rc_ref[...], rs1_ref[...], rs2_ref[...]
    bd = bd_ref[...]
    ts = x.shape[0]
    lane = lax.broadcasted_iota(I32, (ts, LANES), 1)
    ones = jnp.ones((ts, LANES), F32)

    def proj(c0, width):
        return jnp.dot(h, w_ref[:, c0:c0 + width], preferred_element_type=F32)

    def normed(p, c0):
        width = p.shape[1]
        ss = _group_sumsq(p, bd[:width, :width])
        return p * lax.rsqrt(ss * (1.0 / HEAD_DIM) + RMS_EPS) * gain_ref[:, c0:c0 + width]

    def store_slots(ref, y, extra, first_head=0):
        for pair in range(y.shape[1] // LANES):
            z = y[:, pair * LANES:(pair + 1) * LANES]
            e = extra[:, pair * LANES:(pair + 1) * LANES]
            even = jnp.where(lane < HEAD_DIM, z,
                             jnp.where(lane == HEAD_DIM, pltpu.roll(e, HEAD_DIM, 1), 0.0))
            odd = jnp.where(lane < HEAD_DIM, pltpu.roll(z, HEAD_DIM, 1),
                            jnp.where(lane == HEAD_DIM, e, 0.0))
            ref[0, first_head + 2 * pair] = even.astype(BF16)
            ref[0, first_head + 2 * pair + 1] = odd.astype(BF16)

    def query(c0):
        y = _rope(normed(proj(c0, 256), c0), rc, rs1, rs2)
        return y, -jnp.sqrt(_group_sumsq(y, bd)) * kb_ref[:, c0:c0 + 256]

    def key(c0):
        return _rope(normed(proj(c0, 256), c0), rc, rs1, rs2), jnp.ones((ts, 256), F32)

    store_slots(aq1_ref, *query(0))
    store_slots(aq2_ref, *query(256))
    store_slots(ak1_ref, *key(512))
    store_slots(ak2_ref, *key(768))
    for half in range(2):
        store_slots(bq_ref, *query(C_BQ + 256 * half), first_head=4 * half)
        y = _rope(proj(C_IQ + 256 * half, 256), rc, rs1, rs2)
        for j in range(4):
            iq_ref[0, 4 * half + j] = y[:, j * HEAD_DIM:(j + 1) * HEAD_DIM].astype(BF16)

    p = proj(C_BKIK, LANES)
    y = _rope(jnp.where(lane < HEAD_DIM, normed(p, C_BKIK), p),
              rc[:, :LANES], rs1[:, :LANES], rs2[:, :LANES])
    bk_ref[0] = jnp.where(lane < HEAD_DIM, y, jnp.where(lane == HEAD_DIM, 1.0, 0.0)).astype(BF16)
    ik_ref[0] = y[:, HEAD_DIM:].astype(BF16)

    p = proj(C_LAT, LANES)
    ms = jnp.mean(p * p, axis=-1, keepdims=True)
    lat = p * lax.rsqrt(ms + RMS_EPS) * latg_ref[...]
    lat_ref[0] = jnp.concatenate([lat, ones], axis=1).astype(BF16)

    for j in range(A_HEADS):
        av_ref[0, j] = jnp.concatenate([proj(C_AV + j * A_V_DIM, A_V_DIM), ones],
                                       axis=1).astype(BF16)

    p = proj(C_IW, LANES)
    iw_ref[0] = p[:, :IDX_HEADS] * (IDX_HEADS ** -0.5 * HEAD_DIM ** -0.5)


def _rope_tables(S, width):
    half = ROT_DIM // 2
    pos = jnp.arange(S, dtype=F32)
    inv = ROPE_THETA ** (-jnp.arange(0, ROT_DIM, 2, dtype=F32) / ROT_DIM)
    ang = pos[:, None] * inv[None, :]
    cos, sin = jnp.cos(ang), jnp.sin(ang)
    zeros = jnp.zeros((S, HEAD_DIM - ROT_DIM), F32)
    c = jnp.concatenate([cos, cos, zeros + 1.0], axis=1)
    s1 = jnp.concatenate([-sin, jnp.zeros((S, half), F32), zeros], axis=1)
    s2 = jnp.concatenate([jnp.zeros((S, half), F32), sin, zeros], axis=1)
    reps = width // HEAD_DIM
    return tuple(jnp.tile(t, (1, reps)) for t in (c, s1, s2))


def _project(x, sh1, sc1, norm_g, w_in, a_q_g, a_k_g, b_q_g, b_k_g, b_kv_g):
    B, S, D = x.shape
    ts = PROJ_ROWS
    sizes = (256, 256, 256, 256, 512, 512, 64, 128, 512, 64, 8)
    offs = np.concatenate([[0], np.cumsum(sizes)])
    seg = lambda i: w_in[:, offs[i]:offs[i + 1]]
    w_p = jnp.concatenate(
        [seg(0), seg(1), seg(2), seg(3), seg(5), seg(6), seg(9), seg(8), seg(7), seg(4), seg(10),
         jnp.zeros((D, C_END - C_IW - IDX_HEADS), F32)], axis=1).astype(BF16)
    scale = HEAD_DIM ** -0.5
    gain = jnp.concatenate(
        [jnp.tile(a_q_g * scale, 2 * A_HEADS), jnp.tile(a_k_g, 2 * A_HEADS),
         jnp.tile(b_q_g * scale, B_HEADS), b_k_g, jnp.ones((HEAD_DIM,), F32)]).reshape(1, N_GAIN)
    kb_a = 8.0 * NORM_SLACK * jnp.max(jnp.abs(a_k_g))
    kb_b = 8.0 * NORM_SLACK * jnp.max(jnp.abs(b_k_g))
    zeros = lambda n: jnp.zeros((n,), F32)
    kb = jnp.concatenate([zeros(512) + kb_a, zeros(512), zeros(512) + kb_b,
                          zeros(N_GAIN - C_BKIK)]).reshape(1, N_GAIN)
    bound_a = jnp.max(jnp.abs(a_q_g)) * kb_a
    bound_b = jnp.max(jnp.abs(b_q_g)) * kb_b
    rc, rs1, rs2 = _rope_tables(S, 256)
    gid = np.arange(256) // HEAD_DIM
    bd = jnp.asarray(gid[:, None] == gid[None, :], BF16)

    row = lambda b, i: (b, 0, 0)
    full = lambda b, i: (0, 0)
    heads = lambda n, w: pl.BlockSpec((1, n, ts, w), lambda b, i: (b, 0, i, 0))
    flat = lambda w: pl.BlockSpec((1, ts, w), lambda b, i: (b, i, 0))
    hshape = lambda n, w: jax.ShapeDtypeStruct((B, n, S, w), BF16)
    outs = pl.pallas_call(
        _proj_kernel,
        grid=(B, S // ts),
        in_specs=[pl.BlockSpec((1, ts, D), lambda b, i: (b, i, 0)),
                  pl.BlockSpec((1, 1, D), row), pl.BlockSpec((1, 1, D), row),
                  pl.BlockSpec((1, D), full),
                  pl.BlockSpec((D, C_END), full),
                  pl.BlockSpec((1, N_GAIN), full),
                  pl.BlockSpec((1, N_GAIN), full),
                  pl.BlockSpec((1, LANES), full),
                  pl.BlockSpec((ts, 256), lambda b, i: (i, 0)),
                  pl.BlockSpec((ts, 256), lambda b, i: (i, 0)),
                  pl.BlockSpec((ts, 256), lambda b, i: (i, 0)),
                  pl.BlockSpec((256, 256), full)],
        out_specs=[heads(A_HEADS, LANES)] * 4
        + [heads(B_HEADS, LANES), flat(LANES), flat(HEAD_DIM), heads(IDX_HEADS, HEAD_DIM),
           flat(2 * B_LATENT), heads(A_HEADS, 2 * A_V_DIM), flat(IDX_HEADS)],
        out_shape=[hshape(A_HEADS, LANES)] * 4
        + [hshape(B_HEADS, LANES), jax.ShapeDtypeStruct((B, S, LANES), BF16),
           jax.ShapeDtypeStruct((B, S, HEAD_DIM), BF16), hshape(IDX_HEADS, HEAD_DIM),
           jax.ShapeDtypeStruct((B, S, 2 * B_LATENT), BF16), hshape(A_HEADS, 2 * A_V_DIM),
           jax.ShapeDtypeStruct((B, S, IDX_HEADS), F32)],
        compiler_params=_cparams(2),
    )(x, sh1, sc1, norm_g.reshape(1, D), w_p, gain, kb, b_kv_g.reshape(1, B_LATENT), rc, rs1, rs2,
      bd)
    plain_a = (bound_a <= PLAIN_EXP_MAX_BOUND).astype(I32).reshape(1)
    plain_b = (bound_b <= PLAIN_EXP_MAX_BOUND).astype(I32).reshape(1)
    return outs, plain_a, plain_b


def _softmax_init(plain, m_ref, acc_ref):
    acc_ref[...] = jnp.zeros(acc_ref.shape, F32)
    if not plain:
        m_ref[...] = jnp.full(m_ref.shape, NEG_BIG, F32)


def _softmax_step(plain, s, v1, m_ref, acc_ref):
    if plain:
        acc_ref[...] += jnp.dot(jnp.exp(s).astype(BF16), v1, preferred_element_type=F32)
        return
    m_old = m_ref[...]
    m_new = jnp.maximum(m_old, jnp.max(s, axis=-1, keepdims=True))
    p = jnp.exp(s - m_new).astype(BF16)
    acc_ref[...] = (jnp.exp(m_old - m_new) * acc_ref[...]
                    + jnp.dot(p, v1, preferred_element_type=F32))
    m_ref[...] = m_new


def _softmax_result(acc_ref, width):
    acc = acc_ref[...]
    return acc[:, :width] / acc[:, width:width + 1]


def _chunk_mask(q0, k0, tq, tk):
    qc = (q0 + lax.broadcasted_iota(I32, (tq, tk), 0)) // CHUNK
    kc = (k0 + lax.broadcasted_iota(I32, (tq, tk), 1)) // CHUNK
    return kc <= qc


def _diff_attn_kernel(plain_ref, lam_ref, subg_ref, q1_ref, q2_ref, k1_ref, k2_ref, v_ref, o_ref,
                      m1_ref, acc1_ref, m2_ref, acc2_ref, *, lambda_init):
    tq, tk = ATT_TQ, ATT_TK
    i = pl.program_id(2)
    n_tiles = ((i + 1) * tq + tk - 1) // tk

    def attend(plain):
        q1 = q1_ref[0, 0]
        q2 = q2_ref[0, 0]
        _softmax_init(plain, m1_ref, acc1_ref)
        _softmax_init(plain, m2_ref, acc2_ref)

        def tile(j, masked):
            ks = pl.multiple_of(j * tk, tk)
            v1 = v_ref[0, 0, pl.ds(ks, tk), :]
            s1 = _dot_nt(q1, k1_ref[0, 0, pl.ds(ks, tk), :])
            s2 = _dot_nt(q2, k2_ref[0, 0, pl.ds(ks, tk), :])
            if masked:
                ok = _chunk_mask(i * tq, ks, tq, tk)
                s1 = jnp.where(ok, s1, NEG_BIG)
                s2 = jnp.where(ok, s2, NEG_BIG)
            _softmax_step(plain, s1, v1, m1_ref, acc1_ref)
            _softmax_step(plain, s2, v1, m2_ref, acc2_ref)

        def body(j, c):
            tile(j, False)
            return c

        lax.fori_loop(0, n_tiles - 1, body, 0)
        tile(n_tiles - 1, True)

    pl.when(plain_ref[0] == 1)(lambda: attend(True))
    pl.when(plain_ref[0] != 1)(lambda: attend(False))

    lv = lam_ref[...]
    lam = (jnp.exp(jnp.sum(lv[0:1] * lv[1:2], axis=-1, keepdims=True))
           - jnp.exp(jnp.sum(lv[2:3] * lv[3:4], axis=-1, keepdims=True)) + lambda_init)
    o = _softmax_result(acc1_ref, A_V_DIM) - lam * _softmax_result(acc2_ref, A_V_DIM)
    ms = jnp.mean(o * o, axis=-1, keepdims=True)
    o = o * lax.rsqrt(ms + RMS_EPS) * subg_ref[...] * (1.0 - lambda_init)
    o_ref[0] = o.astype(BF16)


def _diff_attention(plain, q1, q2, k1, k2, v1, a_lambda, sub_g, lambda_init):
    B, H, S, dq = q1.shape
    tq = ATT_TQ
    qspec = pl.BlockSpec((1, 1, tq, dq), lambda b, h, i, p: (b, h, i, 0))
    kspec = pl.BlockSpec((1, 1, S, dq), lambda b, h, i, p: (b, h, 0, 0))
    col = lambda: pltpu.VMEM((tq, 1), F32)
    acc = lambda: pltpu.VMEM((tq, 2 * A_V_DIM), F32)
    grid_spec = pltpu.PrefetchScalarGridSpec(
        num_scalar_prefetch=1,
        grid=(B, H, S // tq),
        in_specs=[pl.BlockSpec((4, HEAD_DIM), lambda b, h, i, p: (0, 0)),
                  pl.BlockSpec((1, A_V_DIM), lambda b, h, i, p: (0, 0)),
                  qspec, qspec, kspec, kspec,
                  pl.BlockSpec((1, 1, S, 2 * A_V_DIM), lambda b, h, i, p: (b, h, 0, 0))],
        out_specs=pl.BlockSpec((1, tq, A_V_DIM), lambda b, h, i, p: (b, i, h)),
        scratch_shapes=[col(), acc(), col(), acc()],
    )
    return pl.pallas_call(
        functools.partial(_diff_attn_kernel, lambda_init=lambda_init),
        grid_spec=grid_spec,
        out_shape=jax.ShapeDtypeStruct((B, S, H * A_V_DIM), BF16),
        compiler_params=_cparams(3),
    )(plain, a_lambda, sub_g.reshape(1, A_V_DIM), q1, q2, k1, k2, v1)


def _dsa_kernel(plain_ref, iq_ref, iw_ref, ik_ref, q_ref, k_ref, lat_ref, wuv_ref, tri_ref, o_ref,
                score_ref, m_ref, acc_ref, *, topk):
    tq, tk = DSA_TQ, DSA_TK
    nh = B_HEADS
    i = pl.program_id(1)
    n_tiles = ((i + 1) * tq + tk - 1) // tk

    iq = iq_ref[0].reshape(IDX_HEADS * tq, HEAD_DIM)
    iw = iw_ref[0]

    def score_tile(j, masked):
        ks = pl.multiple_of(j * tk, tk)
        rel = jnp.maximum(_dot_nt(iq, ik_ref[0, pl.ds(ks, tk), :]), 0.0)
        rel = rel.reshape(IDX_HEADS, tq, tk)
        score = rel[0] * iw[:, 0:1]
        for h in range(1, IDX_HEADS):
            score = score + rel[h] * iw[:, h:h + 1]
        if masked:
            score = jnp.where(_chunk_mask(i * tq, ks, tq, tk), score, -jnp.inf)
        score_ref[:, pl.ds(ks, tk)] = score

    def score_body(j, c):
        score_tile(j, False)
        return c

    lax.fori_loop(0, n_tiles - 1, score_body, 0)
    score_tile(n_tiles - 1, True)

    def image_to_float(t):
        return pltpu.bitcast(jnp.where(t < 0, t ^ 0x7FFFFFFF, t), F32)

    def count(pred, cand):
        def body(j, acc):
            ks = pl.multiple_of(j * tk, tk)
            hit = jnp.where(pred(score_ref[:, pl.ds(ks, tk)], cand), 1.0, 0.0)
            for c in range(tk // LANES):
                acc = acc + hit[:, c * LANES:(c + 1) * LANES]
            return acc
        acc = lax.fori_loop(0, n_tiles, body, jnp.zeros((tq, LANES), F32))
        return jnp.sum(acc, axis=-1, keepdims=True)

    def bit_body(step, t):
        bit = jnp.left_shift(jnp.int32(1), 31 - step)
        cand = t + bit
        n_ge = count(lambda s, c: s >= c, image_to_float(cand))
        return jnp.where(n_ge >= topk, cand, t)

    t = lax.fori_loop(0, 32, bit_body, jnp.full((tq, 1), INT_MIN, I32))
    thr = jnp.where(t == INT_MIN, jnp.finfo(F32).min, image_to_float(t))

    def min_where(pred, cand):
        def body(j, acc):
            ks = pl.multiple_of(j * tk, tk)
            s = score_ref[:, pl.ds(ks, tk)]
            s = jnp.where(pred(s, cand), s, jnp.inf)
            for c in range(tk // LANES):
                acc = jnp.minimum(acc, s[:, c * LANES:(c + 1) * LANES])
            return acc
        acc = lax.fori_loop(0, n_tiles, body, jnp.full((tq, LANES), jnp.inf, F32))
        return jnp.min(acc, axis=-1, keepdims=True)

    ge = lambda s, c: s >= c
    gt = lambda s, c: s > c
    low = min_where(ge, thr)
    nxt = min_where(gt, low)
    thr = jnp.where(count(ge, nxt) >= topk, nxt, jnp.where(low < jnp.inf, low, thr))

    n_ge = count(ge, thr)

    @pl.when(jnp.max(n_ge) > topk)
    def _():
        quota = topk - count(gt, thr)
        tri = tri_ref[...]

        def tie_body(j, carry):
            ks = pl.multiple_of(j * tk, tk)
            for c in range(tk // LANES):
                sl = pl.ds(ks + c * LANES, LANES)
                score = score_ref[:, sl]
                eq = score == thr
                eqf = jnp.where(eq, 1.0, 0.0)
                prefix = jnp.dot(eqf.astype(BF16), tri, preferred_element_type=F32) + carry
                score_ref[:, sl] = jnp.where(eq & (prefix > quota), -jnp.inf, score)
                carry = carry + jnp.sum(eqf, axis=-1, keepdims=True)
            return carry

        lax.fori_loop(0, n_tiles, tie_body, jnp.zeros((tq, 1), F32))

    def attend(plain):
        q = q_ref[0].reshape(nh * tq, LANES)
        _softmax_init(plain, m_ref, acc_ref)

        def attn_body(j, c):
            ks = pl.multiple_of(j * tk, tk)
            s = _dot_nt(q, k_ref[0, pl.ds(ks, tk), :]).reshape(nh, tq, tk)
            sel = score_ref[:, pl.ds(ks, tk)] >= thr
            s = jnp.where(sel[None], s, NEG_BIG).reshape(nh * tq, tk)
            _softmax_step(plain, s, lat_ref[0, pl.ds(ks, tk), :], m_ref, acc_ref)
            return c

        lax.fori_loop(0, n_tiles, attn_body, 0)

    pl.when(plain_ref[0] == 1)(lambda: attend(True))
    pl.when(plain_ref[0] != 1)(lambda: attend(False))

    o = _softmax_result(acc_ref, B_LATENT).astype(BF16)
    for h in range(nh):
        oh = jnp.dot(o[h * tq:(h + 1) * tq], wuv_ref[h], preferred_element_type=F32)
        o_ref[0, :, h * HEAD_DIM:(h + 1) * HEAD_DIM] = oh.astype(BF16)


def _dsa_attention(plain, iq, iw, ik, q, k, lat1, w_uv):
    B, nh, S, dq = q.shape
    dh = HEAD_DIM
    tq = DSA_TQ
    topk = min(IDX_TOPK_MAX, S // 4)
    lane = np.arange(LANES)
    tri = jnp.asarray(lane[:, None] <= lane[None, :], BF16)
    hspec = lambda w: pl.BlockSpec((1, nh, tq, w), lambda b, i, p: (b, 0, i, 0))
    kspec = lambda w: pl.BlockSpec((1, S, w), lambda b, i, p: (b, 0, 0))
    grid_spec = pltpu.PrefetchScalarGridSpec(
        num_scalar_prefetch=1,
        grid=(B, S // tq),
        in_specs=[hspec(dh),
                  pl.BlockSpec((1, tq, IDX_HEADS), lambda b, i, p: (b, i, 0)),
                  kspec(dh), hspec(dq), kspec(dq), kspec(2 * B_LATENT),
                  pl.BlockSpec((nh, B_LATENT, dh), lambda b, i, p: (0, 0, 0)),
                  pl.BlockSpec((LANES, LANES), lambda b, i, p: (0, 0))],
        out_specs=pl.BlockSpec((1, tq, nh * dh), lambda b, i, p: (b, i, 0)),
        scratch_shapes=[pltpu.VMEM((tq, S), F32),
                        pltpu.VMEM((nh * tq, 1), F32),
                        pltpu.VMEM((nh * tq, 2 * B_LATENT), F32)],
    )
    return pl.pallas_call(
        functools.partial(_dsa_kernel, topk=topk),
        grid_spec=grid_spec,
        out_shape=jax.ShapeDtypeStruct((B, S, nh * dh), BF16),
        compiler_params=_cparams(2),
    )(plain, iq, iw, ik, q, k, lat1, w_uv.astype(BF16), tri)


def _out_kernel(a_ref, b_ref, x_ref, g1_ref, sh_ref, sc_ref, ng_ref, woa_ref, wob_ref, rw_ref,
                rb_ref, tri_ref, x1_ref, h2_ref, idx_ref, gate_ref, rank_ref, cnt_ref, carry_ref):
    first = jnp.logical_and(pl.program_id(0) == 0, pl.program_id(1) == 0)

    @pl.when(first)
    def _():
        carry_ref[...] = jnp.zeros(carry_ref.shape, F32)

    mix = (jnp.dot(a_ref[0], woa_ref[...], preferred_element_type=F32)
           + jnp.dot(b_ref[0], wob_ref[...], preferred_element_type=F32))
    x1 = x_ref[0] + g1_ref[0] * mix
    x1_ref[0] = x1
    ms = jnp.mean(x1 * x1, axis=-1, keepdims=True)
    h2 = x1 * lax.rsqrt(ms + RMS_EPS) * ng_ref[...]
    h2 = h2 * (1.0 + sc_ref[0]) + sh_ref[0]
    h2_ref[0] = h2

    logits = jnp.dot(h2, rw_ref[...], preferred_element_type=F32,
                     precision=lax.Precision.HIGHEST) + rb_ref[...]
    ts = logits.shape[0]
    lane_i = lax.broadcasted_iota(I32, (ts, LANES), 1)
    lane = lane_i.astype(F32)
    neg_inf = jnp.float32(-jnp.inf)
    l = jnp.where(lane_i < N_EXPERTS, logits, neg_inf)
    vals, idxs = [], []
    for _ in range(TOP_K):
        m = jnp.max(l, axis=-1, keepdims=True)
        idx = jnp.min(jnp.where(l == m, lane, float(LANES)), axis=-1, keepdims=True)
        vals.append(m)
        idxs.append(idx)
        l = jnp.where(lane == idx, neg_inf, l)
    es = [jnp.exp(v - vals[0]) for v in vals]
    denom = es[0] + es[1] + es[2] + es[3]

    onehot = jnp.zeros((ts, LANES), F32)
    for idx in idxs:
        onehot = onehot + jnp.where(lane == idx, 1.0, 0.0)
    prefix = jnp.dot(tri_ref[...], onehot.astype(BF16), preferred_element_type=F32) + carry_ref[...]
    idx_out = jnp.zeros((ts, LANES), I32)
    gate_out = jnp.zeros((ts, LANES), F32)
    rank_out = jnp.zeros((ts, LANES), I32)
    for k in range(TOP_K):
        rank = jnp.sum(jnp.where(lane == idxs[k], prefix, 0.0), axis=-1, keepdims=True)
        idx_out = jnp.where(lane_i == k, idxs[k].astype(I32), idx_out)
        gate_out = jnp.where(lane_i == k, es[k] / denom, gate_out)
        rank_out = jnp.where(lane_i == k, rank.astype(I32), rank_out)
    idx_ref[0] = idx_out
    gate_ref[0] = gate_out
    rank_ref[0] = rank_out
    carry = carry_ref[...] + jnp.sum(onehot, axis=0, keepdims=True)
    carry_ref[...] = carry
    cnt_ref[...] = carry


def _out_and_route(a_out, b_out, x, g1, sh2, sc2, norm_g, w_out, router_w, router_b):
    B, S, D = x.shape
    ts = OUT_ROWS
    aw = a_out.shape[-1]
    w_bf = w_out.astype(BF16)
    rw = jnp.concatenate([router_w, jnp.zeros((D, LANES - N_EXPERTS), F32)], axis=1)
    rb = jnp.concatenate([router_b, jnp.zeros((LANES - N_EXPERTS,), F32)]).reshape(1, LANES)
    r = np.arange(ts)
    tri = jnp.asarray(r[:, None] > r[None, :], BF16)
    row = lambda b, i: (b, 0, 0)
    full = lambda b, i: (0, 0)
    tok = lambda w: pl.BlockSpec((1, ts, w), lambda b, i: (b, i, 0))
    return pl.pallas_call(
        _out_kernel,
        grid=(B, S // ts),
        in_specs=[tok(aw), tok(D - aw), tok(D),
                  pl.BlockSpec((1, 1, D), row), pl.BlockSpec((1, 1, D), row),
                  pl.BlockSpec((1, 1, D), row), pl.BlockSpec((1, D), full),
                  pl.BlockSpec((aw, D), full), pl.BlockSpec((D - aw, D), full),
                  pl.BlockSpec((D, LANES), full), pl.BlockSpec((1, LANES), full),
                  pl.BlockSpec((ts, ts), full)],
        out_specs=[tok(D), tok(D), tok(LANES), tok(LANES), tok(LANES),
                   pl.BlockSpec((1, LANES), full)],
        out_shape=[jax.ShapeDtypeStruct((B, S, D), F32), jax.ShapeDtypeStruct((B, S, D), F32),
                   jax.ShapeDtypeStruct((B, S, LANES), I32),
                   jax.ShapeDtypeStruct((B, S, LANES), F32),
                   jax.ShapeDtypeStruct((B, S, LANES), I32),
                   jax.ShapeDtypeStruct((1, LANES), F32)],
        scratch_shapes=[pltpu.VMEM((1, LANES), F32)],
        compiler_params=_cparams(2),
    )(a_out, b_out, x, g1, sh2, sc2, norm_g.reshape(1, D), w_bf[:aw], w_bf[aw:], rw, rb, tri)


def _row_copy_wait(src_ref, dst_ref, sem, rows):
    pltpu.make_async_copy(src_ref.at[pl.ds(0, rows), :], dst_ref.at[pl.ds(0, rows), :], sem).wait()


def _dispatch_kernel(dest_ref, h_ref, xin_ref, xbuf_ref, sem):
    del xin_ref
    rows = h_ref.shape[0]

    def body(r, c):
        for k in range(TOP_K):
            d = dest_ref[0, 0, r * TOP_K + k]
            pltpu.make_async_copy(h_ref.at[pl.ds(r, 1), :], xbuf_ref.at[pl.ds(d, 1), :], sem).start()
        return c

    lax.fori_loop(0, rows, body, 0)
    for _ in range(TOP_K):
        _row_copy_wait(h_ref, xbuf_ref, sem, rows)


def _dispatch(h2, dest, n_rows):
    T, D = h2.shape
    ts = MOVE_ROWS
    xbuf0 = jnp.zeros((n_rows, D), F32)
    return pl.pallas_call(
        _dispatch_kernel,
        grid=(T // ts,),
        in_specs=[pl.BlockSpec((1, 1, ts * TOP_K), lambda i: (i, 0, 0), memory_space=pltpu.SMEM),
                  pl.BlockSpec((ts, D), lambda i: (i, 0)),
                  pl.BlockSpec(memory_space=pl.ANY)],
        out_specs=pl.BlockSpec(memory_space=pl.ANY),
        out_shape=jax.ShapeDtypeStruct((n_rows, D), F32),
        scratch_shapes=[pltpu.SemaphoreType.DMA(())],
        input_output_aliases={2: 0},
        compiler_params=_cparams(1),
    )(dest.reshape(T // ts, 1, ts * TOP_K), h2, xbuf0)


def _moe_kernel(be_ref, nused_ref, x_ref, wg_ref, bg_ref, wu_ref, bu_ref, wd_ref, bd_ref, y_ref):
    del be_ref

    @pl.when(pl.program_id(0) < nused_ref[0])
    def _():
        xb = x_ref[...].astype(BF16)
        g = jnp.dot(xb, wg_ref[0], preferred_element_type=F32) + bg_ref[0]
        u = jnp.dot(xb, wu_ref[0], preferred_element_type=F32) + bu_ref[0]
        g = jnp.minimum(g, SWIGLU_LIMIT)
        u = jnp.clip(u, -SWIGLU_LIMIT, SWIGLU_LIMIT)
        a = g * (1.0 / (1.0 + jnp.exp(-SWIGLU_ALPHA * g))) * (u + 1.0)
        y_ref[...] = jnp.dot(a.astype(BF16), wd_ref[0], preferred_element_type=F32) + bd_ref[0]

    @pl.when(pl.program_id(0) >= nused_ref[0])
    def _():
        y_ref[...] = jnp.zeros(y_ref.shape, F32)


def _moe_experts(xbuf, block_expert, n_used, w_gate, b_gate, w_up, b_up, w_down, b_down):
    R, D = xbuf.shape
    E, _, F = w_gate.shape
    G = MOE_ROWS
    nb = R // G
    blk = lambda i, be, nu: (jnp.minimum(i, nu[0] - 1), 0)
    wsel = lambda i, be, nu: (be[jnp.minimum(i, nu[0] - 1)], 0, 0)
    grid_spec = pltpu.PrefetchScalarGridSpec(
        num_scalar_prefetch=2,
        grid=(nb,),
        in_specs=[pl.BlockSpec((G, D), blk),
                  pl.BlockSpec((1, D, F), wsel), pl.BlockSpec((1, 1, F), wsel),
                  pl.BlockSpec((1, D, F), wsel), pl.BlockSpec((1, 1, F), wsel),
                  pl.BlockSpec((1, F, D), wsel), pl.BlockSpec((1, 1, D), wsel)],
        out_specs=pl.BlockSpec((G, D), lambda i, be, nu: (i, 0)),
    )
    return pl.pallas_call(
        _moe_kernel,
        grid_spec=grid_spec,
        out_shape=jax.ShapeDtypeStruct((R, D), F32),
        compiler_params=_cparams(1),
    )(block_expert, n_used, xbuf, w_gate.astype(BF16), b_gate.reshape(E, 1, F),
      w_up.astype(BF16), b_up.reshape(E, 1, F), w_down.astype(BF16), b_down.reshape(E, 1, D))


def _combine_kernel(dest_ref, gate_ref, x1_ref, g2_ref, ybuf_ref, o_ref, buf_ref, sem):
    rows = x1_ref.shape[0]

    def body(r, c):
        for k in range(TOP_K):
            d = dest_ref[0, 0, r * TOP_K + k]
            pltpu.make_async_copy(ybuf_ref.at[pl.ds(d, 1), :], buf_ref.at[k, pl.ds(r, 1), :],
                                  sem).start()
        return c

    lax.fori_loop(0, rows, body, 0)
    for k in range(TOP_K):
        _row_copy_wait(ybuf_ref, buf_ref.at[k], sem, rows)
    gate = gate_ref[...]
    y = buf_ref[0] * gate[:, 0:1]
    for k in range(1, TOP_K):
        y = y + buf_ref[k] * gate[:, k:k + 1]
    o_ref[...] = x1_ref[...] + g2_ref[0] * y


def _combine(ybuf, dest, gates, x1, g2, seq):
    T, D = x1.shape
    ts = MOVE_ROWS
    per_seq = seq // ts
    return pl.pallas_call(
        _combine_kernel,
        grid=(T // ts,),
        in_specs=[pl.BlockSpec((1, 1, ts * TOP_K), lambda i: (i, 0, 0), memory_space=pltpu.SMEM),
                  pl.BlockSpec((ts, LANES), lambda i: (i, 0)),
                  pl.BlockSpec((ts, D), lambda i: (i, 0)),
                  pl.BlockSpec((1, 1, D), lambda i: (i // per_seq, 0, 0)),
                  pl.BlockSpec(memory_space=pl.ANY)],
        out_specs=pl.BlockSpec((ts, D), lambda i: (i, 0)),
        out_shape=jax.ShapeDtypeStruct((T, D), F32),
        scratch_shapes=[pltpu.VMEM((TOP_K, ts, D), F32), pltpu.SemaphoreType.DMA(())],
        compiler_params=_cparams(1),
    )(dest.reshape(T // ts, 1, ts * TOP_K), gates, x1, g2, ybuf)


def _layer(x, mod, lambda_init, norm1_g, norm2_g, w_in, w_out, a_q_norm_g, a_k_norm_g, a_lambda,
           a_sub_g, b_q_norm_g, b_k_norm_g, b_kv_norm_g, b_w_uv, router_w, router_b, w_gate,
           b_gate, w_up, b_up, w_down, b_down):
    B, S, D = x.shape
    T = B * S
    sh1, sc1, g1, sh2, sc2, g2 = [m.reshape(B, 1, D) for m in jnp.split(mod, 6, axis=-1)]

    (aq1, aq2, ak1, ak2, bq, bk, ik, iq, blat1, av1, iw), plain_a, plain_b = _project(
        x, sh1, sc1, norm1_g, w_in, a_q_norm_g, a_k_norm_g, b_q_norm_g, b_k_norm_g, b_kv_norm_g)
    a_out = _diff_attention(plain_a, aq1, aq2, ak1, ak2, av1, a_lambda, a_sub_g, lambda_init)
    b_out = _dsa_attention(plain_b, iq, iw, ik, bq, bk, blat1, b_w_uv)

    x1, h2, top_idx, gates, rank, counts = _out_and_route(
        a_out, b_out, x, g1, sh2, sc2, norm2_g, w_out, router_w, router_b)

    G = MOE_ROWS
    counts = counts[0, :N_EXPERTS].astype(I32)
    padded = ((counts + G - 1) // G) * G
    pcum = jnp.cumsum(padded)
    poff = pcum - padded
    nb = (T * TOP_K) // G + N_EXPERTS
    starts = jnp.arange(nb, dtype=I32) * G
    block_expert = jnp.minimum(
        jnp.sum((pcum[None, :] <= starts[:, None]).astype(I32), axis=1), N_EXPERTS - 1)
    n_used = (pcum[-1:] // G).astype(I32)
    top_idx = top_idx.reshape(T, LANES)[:, :TOP_K]
    dest = (poff[top_idx] + rank.reshape(T, LANES)[:, :TOP_K]).astype(I32).reshape(T * TOP_K)

    xbuf = _dispatch(h2.reshape(T, D), dest, nb * G)
    ybuf = _moe_experts(xbuf, block_expert.astype(I32), n_used, w_gate, b_gate, w_up, b_up,
                        w_down, b_down)
    out = _combine(ybuf, dest, gates.reshape(T, LANES), x1.reshape(T, D), g2, S)
    return out.reshape(B, S, D)


def kernel(x, c, norm1_g, norm2_g, w_ada, b_ada, w_in, w_out, a_q_norm_g, a_k_norm_g, a_lambda,
           a_sub_g, b_q_norm_g, b_k_norm_g, b_kv_norm_g, b_w_uv, router_w, router_b, w_gate,
           b_gate, w_up, b_up, w_down, b_down):
    depth = w_in.shape[0]
    for l in range(depth):
        lambda_init = 0.8 - 0.6 * math.exp(-0.3 * l)
        mod = _ada(c, w_ada[l], b_ada[l])
        x = _layer(x, mod, lambda_init, norm1_g[l], norm2_g[l], w_in[l], w_out[l], a_q_norm_g[l],
                   a_k_norm_g[l], a_lambda[l], a_sub_g[l], b_q_norm_g[l], b_k_norm_g[l],
                   b_kv_norm_g[l], b_w_uv[l], router_w[l], router_b[l], w_gate[l], b_gate[l],
                   w_up[l], b_up[l], w_down[l], b_down[l])
    return x
```

```python
import functools
import math

import numpy as np
import jax
import jax.numpy as jnp
from jax import lax
from jax.experimental import pallas as pl
from jax.experimental.pallas import tpu as pltpu

F32 = jnp.float32
BF16 = jnp.bfloat16
I32 = jnp.int32

CHUNK = 64
HEAD_DIM = 64
ROT_DIM = HEAD_DIM // 4
ROPE_THETA = 500000.0
RMS_EPS = 1e-6
A_HEADS = 4
A_V_DIM = 2 * HEAD_DIM
B_HEADS = 8
B_LATENT = 128
IDX_HEADS = 8
IDX_TOPK_MAX = 256
N_EXPERTS = 32
TOP_K = 4
SWIGLU_LIMIT = 7.0
SWIGLU_ALPHA = 1.702

LANES = 128
INT_MIN = -(2 ** 31)
NEG_BIG = -1e30
VMEM_LIMIT = 56 * 1024 * 1024

NORM_SLACK = 1.01
PLAIN_EXP_MAX_BOUND = 30.0

PROJ_ROWS = 256
ATT_TQ = 512
ATT_TK = 512
DSA_TQ = 256
DSA_TK = 512
COUNT_ROWS = 128
OUT_ROWS = 512
MOE_ROWS = 512
MOVE_ROWS = 256


def _cparams(n_axes):
    return pltpu.CompilerParams(
        dimension_semantics=("arbitrary",) * n_axes, vmem_limit_bytes=VMEM_LIMIT)


def _dot_nt(a, b):
    return lax.dot_general(a, b, (((1,), (1,)), ((), ())), preferred_element_type=F32)


def _ada_kernel(c_ref, w_ref, b_ref, o_ref):
    c = c_ref[...]
    sc = c / (1.0 + jnp.exp(-c))
    o_ref[...] = jnp.dot(sc, w_ref[...], preferred_element_type=F32,
                         precision=lax.Precision.HIGHEST) + b_ref[...]


def _ada(c, w, b):
    B, D = c.shape
    N = w.shape[1]
    return pl.pallas_call(
        _ada_kernel,
        grid=(N // D,),
        in_specs=[pl.BlockSpec((B, D), lambda j: (0, 0)),
                  pl.BlockSpec((D, D), lambda j: (0, j)),
                  pl.BlockSpec((1, D), lambda j: (0, j))],
        out_specs=pl.BlockSpec((B, D), lambda j: (0, j)),
        out_shape=jax.ShapeDtypeStruct((B, N), F32),
        compiler_params=_cparams(1),
    )(c, w, b.reshape(1, N))


C_QK = 0
C_BQ = 1024
C_BKIK = 1536
C_IQ = 1664
C_LAT = 2176
C_AV = 2304
C_IW = 2816
C_END = 2944
N_GAIN = C_IQ


def _group_sumsq(p, bd):
    sq = p * p
    hi = sq.astype(BF16)
    lo = (sq - hi.astype(F32)).astype(BF16)
    return (jnp.dot(hi, bd, preferred_element_type=F32)
            + jnp.dot(lo, bd, preferred_element_type=F32))


def _rope(y, c, s1, s2):
    w = y.shape[1]
    return y * c + pltpu.roll(y, w - ROT_DIM // 2, 1) * s1 + pltpu.roll(y, ROT_DIM // 2, 1) * s2


def _proj_kernel(x_ref, sh_ref, sc_ref, g_ref, w_ref, gain_ref, kb_ref, latg_ref, rc_ref, rs1_ref,
                 rs2_ref, bd_ref,
                 aq1_ref, aq2_ref, ak1_ref, ak2_ref, bq_ref, bk_ref, ik_ref, iq_ref, lat_ref,
                 av_ref, iw_ref):
    x = x_ref[0]
    ms = jnp.mean(x * x, axis=-1, keepdims=True)
    h = x * lax.rsqrt(ms + RMS_EPS) * g_ref[...]
    h = (h * (1.0 + sc_ref[0]) + sh_ref[0]).astype(BF16)

    rc, rs1, rs2 = rc_ref[...], rs1_ref[...], rs2_ref[...]
    bd = bd_ref[...]
    ts = x.shape[0]
    lane = lax.broadcasted_iota(I32, (ts, LANES), 1)
    ones = jnp.ones((ts, LANES), F32)

    def proj(c0, width):
        return jnp.dot(h, w_ref[:, c0:c0 + width], preferred_element_type=F32)

    def normed(p, c0):
        width = p.shape[1]
        ss = _group_sumsq(p, bd[:width, :width])
        return p * lax.rsqrt(ss * (1.0 / HEAD_DIM) + RMS_EPS) * gain_ref[:, c0:c0 + width]

    def store_slots(ref, y, extra, first_head=0):
        for pair in range(y.shape[1] // LANES):
            z = y[:, pair * LANES:(pair + 1) * LANES]
            e = extra[:, pair * LANES:(pair + 1) * LANES]
            even = jnp.where(lane < HEAD_DIM, z,
                             jnp.where(lane == HEAD_DIM, pltpu.roll(e, HEAD_DIM, 1), 0.0))
            odd = jnp.where(lane < HEAD_DIM, pltpu.roll(z, HEAD_DIM, 1),
                            jnp.where(lane == HEAD_DIM, e, 0.0))
            ref[0, first_head + 2 * pair] = even.astype(BF16)
            ref[0, first_head + 2 * pair + 1] = odd.astype(BF16)

    def query(c0):
        y = _rope(normed(proj(c0, 256), c0), rc, rs1, rs2)
        return y, -jnp.sqrt(_group_sumsq(y, bd)) * kb_ref[:, c0:c0 + 256]

    def key(c0):
        return _rope(normed(proj(c0, 256), c0), rc, rs1, rs2), jnp.ones((ts, 256), F32)

    store_slots(aq1_ref, *query(0))
    store_slots(aq2_ref, *query(256))
    store_slots(ak1_ref, *key(512))
    store_slots(ak2_ref, *key(768))
    for half in range(2):
        store_slots(bq_ref, *query(C_BQ + 256 * half), first_head=4 * half)
        y = _rope(proj(C_IQ + 256 * half, 256), rc, rs1, rs2)
        for j in range(4):
            iq_ref[0, 4 * half + j] = y[:, j * HEAD_DIM:(j + 1) * HEAD_DIM].astype(BF16)

    p = proj(C_BKIK, LANES)
    y = _rope(jnp.where(lane < HEAD_DIM, normed(p, C_BKIK), p),
              rc[:, :LANES], rs1[:, :LANES], rs2[:, :LANES])
    bk_ref[0] = jnp.where(lane < HEAD_DIM, y, jnp.where(lane == HEAD_DIM, 1.0, 0.0)).astype(BF16)
    ik_ref[0] = y[:, HEAD_DIM:].astype(BF16)

    p = proj(C_LAT, LANES)
    ms = jnp.mean(p * p, axis=-1, keepdims=True)
    lat = p * lax.rsqrt(ms + RMS_EPS) * latg_ref[...]
    lat_ref[0] = jnp.concatenate([lat, ones], axis=1).astype(BF16)

    for j in range(A_HEADS):
        av_ref[0, j] = jnp.concatenate([proj(C_AV + j * A_V_DIM, A_V_DIM), ones],
                                       axis=1).astype(BF16)

    p = proj(C_IW, LANES)
    iw_ref[0] = p[:, :IDX_HEADS] * (IDX_HEADS ** -0.5 * HEAD_DIM ** -0.5)


def _rope_tables(S, width):
    half = ROT_DIM // 2
    pos = jnp.arange(S, dtype=F32)
    inv = ROPE_THETA ** (-jnp.arange(0, ROT_DIM, 2, dtype=F32) / ROT_DIM)
    ang = pos[:, None] * inv[None, :]
    cos, sin = jnp.cos(ang), jnp.sin(ang)
    zeros = jnp.zeros((S, HEAD_DIM - ROT_DIM), F32)
    c = jnp.concatenate([cos, cos, zeros + 1.0], axis=1)
    s1 = jnp.concatenate([-sin, jnp.zeros((S, half), F32), zeros], axis=1)
    s2 = jnp.concatenate([jnp.zeros((S, half), F32), sin, zeros], axis=1)
    reps = width // HEAD_DIM
    return tuple(jnp.tile(t, (1, reps)) for t in (c, s1, s2))


def _project(x, sh1, sc1, norm_g, w_in, a_q_g, a_k_g, b_q_g, b_k_g, b_kv_g):
    B, S, D = x.shape
    ts = PROJ_ROWS
    sizes = (256, 256, 256, 256, 512, 512, 64, 128, 512, 64, 8)
    offs = np.concatenate([[0], np.cumsum(sizes)])
    seg = lambda i: w_in[:, offs[i]:offs[i + 1]]
    w_p = jnp.concatenate(
        [seg(0), seg(1), seg(2), seg(3), seg(5), seg(6), seg(9), seg(8), seg(7), seg(4), seg(10),
         jnp.zeros((D, C_END - C_IW - IDX_HEADS), F32)], axis=1).astype(BF16)
    scale = HEAD_DIM ** -0.5
    gain = jnp.concatenate(
        [jnp.tile(a_q_g * scale, 2 * A_HEADS), jnp.tile(a_k_g, 2 * A_HEADS),
         jnp.tile(b_q_g * scale, B_HEADS), b_k_g, jnp.ones((HEAD_DIM,), F32)]).reshape(1, N_GAIN)
    kb_a = 8.0 * NORM_SLACK * jnp.max(jnp.abs(a_k_g))
    kb_b = 8.0 * NORM_SLACK * jnp.max(jnp.abs(b_k_g))
    zeros = lambda n: jnp.zeros((n,), F32)
    kb = jnp.concatenate([zeros(512) + kb_a, zeros(512), zeros(512) + kb_b,
                          zeros(N_GAIN - C_BKIK)]).reshape(1, N_GAIN)
    bound_a = jnp.max(jnp.abs(a_q_g)) * kb_a
    bound_b = jnp.max(jnp.abs(b_q_g)) * kb_b
    rc, rs1, rs2 = _rope_tables(S, 256)
    gid = np.arange(256) // HEAD_DIM
    bd = jnp.asarray(gid[:, None] == gid[None, :], BF16)

    row = lambda b, i: (b, 0, 0)
    full = lambda b, i: (0, 0)
    heads = lambda n, w: pl.BlockSpec((1, n, ts, w), lambda b, i: (b, 0, i, 0))
    flat = lambda w: pl.BlockSpec((1, ts, w), lambda b, i: (b, i, 0))
    hshape = lambda n, w: jax.ShapeDtypeStruct((B, n, S, w), BF16)
    outs = pl.pallas_call(
        _proj_kernel,
        grid=(B, S // ts),
        in_specs=[pl.BlockSpec((1, ts, D), lambda b, i: (b, i, 0)),
                  pl.BlockSpec((1, 1, D), row), pl.BlockSpec((1, 1, D), row),
                  pl.BlockSpec((1, D), full),
                  pl.BlockSpec((D, C_END), full),
                  pl.BlockSpec((1, N_GAIN), full),
                  pl.BlockSpec((1, N_GAIN), full),
                  pl.BlockSpec((1, LANES), full),
                  pl.BlockSpec((ts, 256), lambda b, i: (i, 0)),
                  pl.BlockSpec((ts, 256), lambda b, i: (i, 0)),
                  pl.BlockSpec((ts, 256), lambda b, i: (i, 0)),
                  pl.BlockSpec((256, 256), full)],
        out_specs=[heads(A_HEADS, LANES)] * 4
        + [heads(B_HEADS, LANES), flat(LANES), flat(HEAD_DIM), heads(IDX_HEADS, HEAD_DIM),
           flat(2 * B_LATENT), heads(A_HEADS, 2 * A_V_DIM), flat(IDX_HEADS)],
        out_shape=[hshape(A_HEADS, LANES)] * 4
        + [hshape(B_HEADS, LANES), jax.ShapeDtypeStruct((B, S, LANES), BF16),
           jax.ShapeDtypeStruct((B, S, HEAD_DIM), BF16), hshape(IDX_HEADS, HEAD_DIM),
           jax.ShapeDtypeStruct((B, S, 2 * B_LATENT), BF16), hshape(A_HEADS, 2 * A_V_DIM),
           jax.ShapeDtypeStruct((B, S, IDX_HEADS), F32)],
        compiler_params=_cparams(2),
    )(x, sh1, sc1, norm_g.reshape(1, D), w_p, gain, kb, b_kv_g.reshape(1, B_LATENT), rc, rs1, rs2,
      bd)
    plain_a = (bound_a <= PLAIN_EXP_MAX_BOUND).astype(I32).reshape(1)
    plain_b = (bound_b <= PLAIN_EXP_MAX_BOUND).astype(I32).reshape(1)
    return outs, plain_a, plain_b


def _softmax_init(plain, m_ref, acc_ref):
    acc_ref[...] = jnp.zeros(acc_ref.shape, F32)
    if not plain:
        m_ref[...] = jnp.full(m_ref.shape, NEG_BIG, F32)


def _softmax_step(plain, s, v1, m_ref, acc_ref):
    if plain:
        acc_ref[...] += jnp.dot(jnp.exp(s).astype(BF16), v1, preferred_element_type=F32)
        return
    m_old = m_ref[...]
    m_new = jnp.maximum(m_old, jnp.max(s, axis=-1, keepdims=True))
    p = jnp.exp(s - m_new).astype(BF16)
    acc_ref[...] = (jnp.exp(m_old - m_new) * acc_ref[...]
                    + jnp.dot(p, v1, preferred_element_type=F32))
    m_ref[...] = m_new


def _softmax_result(acc_ref, width):
    acc = acc_ref[...]
    return acc[:, :width] / acc[:, width:width + 1]


def _chunk_mask(q0, k0, tq, tk):
    qc = (q0 + lax.broadcasted_iota(I32, (tq, tk), 0)) // CHUNK
    kc = (k0 + lax.broadcasted_iota(I32, (tq, tk), 1)) // CHUNK
    return kc <= qc


def _diff_attn_kernel(plain_ref, lam_ref, subg_ref, q1_ref, q2_ref, k1_ref, k2_ref, v_ref, o_ref,
                      m1_ref, acc1_ref, m2_ref, acc2_ref, *, lambda_init):
    tq, tk = ATT_TQ, ATT_TK
    i = pl.program_id(2)
    n_tiles = ((i + 1) * tq + tk - 1) // tk

    def attend(plain):
        q1 = q1_ref[0, 0]
        q2 = q2_ref[0, 0]
        _softmax_init(plain, m1_ref, acc1_ref)
        _softmax_init(plain, m2_ref, acc2_ref)

        def tile(j, masked):
            ks = pl.multiple_of(j * tk, tk)
            v1 = v_ref[0, 0, pl.ds(ks, tk), :]
            s1 = _dot_nt(q1, k1_ref[0, 0, pl.ds(ks, tk), :])
            s2 = _dot_nt(q2, k2_ref[0, 0, pl.ds(ks, tk), :])
            if masked:
                ok = _chunk_mask(i * tq, ks, tq, tk)
                s1 = jnp.where(ok, s1, NEG_BIG)
                s2 = jnp.where(ok, s2, NEG_BIG)
            _softmax_step(plain, s1, v1, m1_ref, acc1_ref)
            _softmax_step(plain, s2, v1, m2_ref, acc2_ref)

        def body(j, c):
            tile(j, False)
            return c

        lax.fori_loop(0, n_tiles - 1, body, 0)
        tile(n_tiles - 1, True)

    pl.when(plain_ref[0] == 1)(lambda: attend(True))
    pl.when(plain_ref[0] != 1)(lambda: attend(False))

    lv = lam_ref[...]
    lam = (jnp.exp(jnp.sum(lv[0:1] * lv[1:2], axis=-1, keepdims=True))
           - jnp.exp(jnp.sum(lv[2:3] * lv[3:4], axis=-1, keepdims=True)) + lambda_init)
    o = _softmax_result(acc1_ref, A_V_DIM) - lam * _softmax_result(acc2_ref, A_V_DIM)
    ms = jnp.mean(o * o, axis=-1, keepdims=True)
    o = o * lax.rsqrt(ms + RMS_EPS) * subg_ref[...] * (1.0 - lambda_init)
    o_ref[0] = o.astype(BF16)


def _diff_attention(plain, q1, q2, k1, k2, v1, a_lambda, sub_g, lambda_init):
    B, H, S, dq = q1.shape
    tq = ATT_TQ
    qspec = pl.BlockSpec((1, 1, tq, dq), lambda b, h, i, p: (b, h, i, 0))
    kspec = pl.BlockSpec((1, 1, S, dq), lambda b, h, i, p: (b, h, 0, 0))
    col = lambda: pltpu.VMEM((tq, 1), F32)
    acc = lambda: pltpu.VMEM((tq, 2 * A_V_DIM), F32)
    grid_spec = pltpu.PrefetchScalarGridSpec(
        num_scalar_prefetch=1,
        grid=(B, H, S // tq),
        in_specs=[pl.BlockSpec((4, HEAD_DIM), lambda b, h, i, p: (0, 0)),
                  pl.BlockSpec((1, A_V_DIM), lambda b, h, i, p: (0, 0)),
                  qspec, qspec, kspec, kspec,
                  pl.BlockSpec((1, 1, S, 2 * A_V_DIM), lambda b, h, i, p: (b, h, 0, 0))],
        out_specs=pl.BlockSpec((1, tq, A_V_DIM), lambda b, h, i, p: (b, i, h)),
        scratch_shapes=[col(), acc(), col(), acc()],
    )
    return pl.pallas_call(
        functools.partial(_diff_attn_kernel, lambda_init=lambda_init),
        grid_spec=grid_spec,
        out_shape=jax.ShapeDtypeStruct((B, S, H * A_V_DIM), BF16),
        compiler_params=_cparams(3),
    )(plain, a_lambda, sub_g.reshape(1, A_V_DIM), q1, q2, k1, k2, v1)


def _dsa_kernel(plain_ref, iq_ref, iw_ref, ik_ref, q_ref, k_ref, lat_ref, wuv_ref, tri_ref, o_ref,
                score_ref, thr_ref, m_ref, acc_ref, *, topk):
    tq, tk = DSA_TQ, DSA_TK
    nh = B_HEADS
    i = pl.program_id(1)
    n_tiles = ((i + 1) * tq + tk - 1) // tk

    iq = iq_ref[0].reshape(IDX_HEADS * tq, HEAD_DIM)
    iw = iw_ref[0]

    def score_tile(j, masked):
        ks = pl.multiple_of(j * tk, tk)
        rel = jnp.maximum(_dot_nt(iq, ik_ref[0, pl.ds(ks, tk), :]), 0.0)
        rel = rel.reshape(IDX_HEADS, tq, tk)
        score = rel[0] * iw[:, 0:1]
        for h in range(1, IDX_HEADS):
            score = score + rel[h] * iw[:, h:h + 1]
        if masked:
            score = jnp.where(_chunk_mask(i * tq, ks, tq, tk), score, -jnp.inf)
        score_ref[:, pl.ds(ks, tk)] = score

    def score_body(j, c):
        score_tile(j, False)
        return c

    lax.fori_loop(0, n_tiles - 1, score_body, 0)
    score_tile(n_tiles - 1, True)

    def image_to_float(t):
        return pltpu.bitcast(jnp.where(t < 0, t ^ 0x7FFFFFFF, t), F32)

    def count(pred, cand):
        parts = []
        for r0 in range(0, tq, COUNT_ROWS):
            rows = pl.ds(r0, COUNT_ROWS)
            cand_r = cand[r0:r0 + COUNT_ROWS]

            def body(j, acc, rows=rows, cand_r=cand_r):
                ks = pl.multiple_of(j * tk, tk)
                hit = jnp.where(pred(score_ref[rows, pl.ds(ks, tk)], cand_r), 1.0, 0.0)
                for c in range(tk // LANES):
                    acc = acc + hit[:, c * LANES:(c + 1) * LANES]
                return acc
            parts.append(lax.fori_loop(0, n_tiles, body, jnp.zeros((COUNT_ROWS, LANES), F32)))
        acc = jnp.concatenate(parts, axis=0)
        return jnp.sum(acc, axis=-1, keepdims=True)

    ge = lambda s, c: s >= c
    gt = lambda s, c: s > c

    def bit_body(state):
        step, t, settled, _ = state
        bit = jnp.left_shift(jnp.int32(1), 31 - step)
        cand = t + bit
        n_ge = count(ge, image_to_float(cand))
        t = jnp.where(settled > 0.0, t, jnp.where(n_ge >= topk, cand, t))
        settled = jnp.where(n_ge == topk, 1.0, settled)
        return step + 1, t, settled, (jnp.min(settled) > 0.0).astype(I32)

    def bit_cond(state):
        step, _, _, all_settled = state
        return jnp.logical_and(step < 32, all_settled == 0)

    _, t, _, all_settled = lax.while_loop(
        bit_cond, bit_body,
        (jnp.int32(0), jnp.full((tq, 1), INT_MIN, I32), jnp.zeros((tq, 1), F32), jnp.int32(0)))
    thr_ref[...] = jnp.where(t == INT_MIN, jnp.finfo(F32).min, image_to_float(t))

    def min_where(pred, cand):
        def body(j, acc):
            ks = pl.multiple_of(j * tk, tk)
            s = score_ref[:, pl.ds(ks, tk)]
            s = jnp.where(pred(s, cand), s, jnp.inf)
            for c in range(tk // LANES):
                acc = jnp.minimum(acc, s[:, c * LANES:(c + 1) * LANES])
            return acc
        acc = lax.fori_loop(0, n_tiles, body, jnp.full((tq, LANES), jnp.inf, F32))
        return jnp.min(acc, axis=-1, keepdims=True)

    @pl.when(all_settled == 0)
    def _():
        thr0 = thr_ref[...]
        low = min_where(ge, thr0)
        nxt = min_where(gt, low)
        thr = jnp.where(count(ge, nxt) >= topk, nxt, jnp.where(low < jnp.inf, low, thr0))
        thr_ref[...] = thr

        @pl.when(jnp.max(count(ge, thr)) > topk)
        def _():
            quota = topk - count(gt, thr)
            tri = tri_ref[...]

            def tie_body(j, carry):
                ks = pl.multiple_of(j * tk, tk)
                for c in range(tk // LANES):
                    sl = pl.ds(ks + c * LANES, LANES)
                    score = score_ref[:, sl]
                    eq = score == thr
                    eqf = jnp.where(eq, 1.0, 0.0)
                    prefix = jnp.dot(eqf.astype(BF16), tri, preferred_element_type=F32) + carry
                    score_ref[:, sl] = jnp.where(eq & (prefix > quota), -jnp.inf, score)
                    carry = carry + jnp.sum(eqf, axis=-1, keepdims=True)
                return carry

            lax.fori_loop(0, n_tiles, tie_body, jnp.zeros((tq, 1), F32))

    thr = thr_ref[...]

    def attend(plain):
        q = q_ref[0].reshape(nh * tq, LANES)
        _softmax_init(plain, m_ref, acc_ref)

        def attn_body(j, c):
            ks = pl.multiple_of(j * tk, tk)
            s = _dot_nt(q, k_ref[0, pl.ds(ks, tk), :]).reshape(nh, tq, tk)
            sel = score_ref[:, pl.ds(ks, tk)] >= thr
            s = jnp.where(sel[None], s, NEG_BIG).reshape(nh * tq, tk)
            _softmax_step(plain, s, lat_ref[0, pl.ds(ks, tk), :], m_ref, acc_ref)
            return c

        lax.fori_loop(0, n_tiles, attn_body, 0)

    pl.when(plain_ref[0] == 1)(lambda: attend(True))
    pl.when(plain_ref[0] != 1)(lambda: attend(False))

    o = _softmax_result(acc_ref, B_LATENT).astype(BF16)
    for h in range(nh):
        oh = jnp.dot(o[h * tq:(h + 1) * tq], wuv_ref[h], preferred_element_type=F32)
        o_ref[0, :, h * HEAD_DIM:(h + 1) * HEAD_DIM] = oh.astype(BF16)


def _dsa_attention(plain, iq, iw, ik, q, k, lat1, w_uv):
    B, nh, S, dq = q.shape
    dh = HEAD_DIM
    tq = DSA_TQ
    topk = min(IDX_TOPK_MAX, S // 4)
    lane = np.arange(LANES)
    tri = jnp.asarray(lane[:, None] <= lane[None, :], BF16)
    hspec = lambda w: pl.BlockSpec((1, nh, tq, w), lambda b, i, p: (b, 0, i, 0))
    kspec = lambda w: pl.BlockSpec((1, S, w), lambda b, i, p: (b, 0, 0))
    grid_spec = pltpu.PrefetchScalarGridSpec(
        num_scalar_prefetch=1,
        grid=(B, S // tq),
        in_specs=[hspec(dh),
                  pl.BlockSpec((1, tq, IDX_HEADS), lambda b, i, p: (b, i, 0)),
                  kspec(dh), hspec(dq), kspec(dq), kspec(2 * B_LATENT),
                  pl.BlockSpec((nh, B_LATENT, dh), lambda b, i, p: (0, 0, 0)),
                  pl.BlockSpec((LANES, LANES), lambda b, i, p: (0, 0))],
        out_specs=pl.BlockSpec((1, tq, nh * dh), lambda b, i, p: (b, i, 0)),
        scratch_shapes=[pltpu.VMEM((tq, S), F32),
                        pltpu.VMEM((tq, 1), F32),
                        pltpu.VMEM((nh * tq, 1), F32),
                        pltpu.VMEM((nh * tq, 2 * B_LATENT), F32)],
    )
    return pl.pallas_call(
        functools.partial(_dsa_kernel, topk=topk),
        grid_spec=grid_spec,
        out_shape=jax.ShapeDtypeStruct((B, S, nh * dh), BF16),
        compiler_params=_cparams(2),
    )(plain, iq, iw, ik, q, k, lat1, w_uv.astype(BF16), tri)


def _out_kernel(a_ref, b_ref, x_ref, g1_ref, sh_ref, sc_ref, ng_ref, woa_ref, wob_ref, rw_ref,
                rb_ref, tri_ref, x1_ref, h2_ref, idx_ref, gate_ref, rank_ref, cnt_ref, carry_ref):
    first = jnp.logical_and(pl.program_id(0) == 0, pl.program_id(1) == 0)

    @pl.when(first)
    def _():
        carry_ref[...] = jnp.zeros(carry_ref.shape, F32)

    mix = (jnp.dot(a_ref[0], woa_ref[...], preferred_element_type=F32)
           + jnp.dot(b_ref[0], wob_ref[...], preferred_element_type=F32))
    x1 = x_ref[0] + g1_ref[0] * mix
    x1_ref[0] = x1
    ms = jnp.mean(x1 * x1, axis=-1, keepdims=True)
    h2 = x1 * lax.rsqrt(ms + RMS_EPS) * ng_ref[...]
    h2 = h2 * (1.0 + sc_ref[0]) + sh_ref[0]
    h2_ref[0] = h2

    logits = jnp.dot(h2, rw_ref[...], preferred_element_type=F32,
                     precision=lax.Precision.HIGHEST) + rb_ref[...]
    ts = logits.shape[0]
    lane_i = lax.broadcasted_iota(I32, (ts, LANES), 1)
    lane = lane_i.astype(F32)
    neg_inf = jnp.float32(-jnp.inf)
    l = jnp.where(lane_i < N_EXPERTS, logits, neg_inf)
    vals, idxs = [], []
    for _ in range(TOP_K):
        m = jnp.max(l, axis=-1, keepdims=True)
        idx = jnp.min(jnp.where(l == m, lane, float(LANES)), axis=-1, keepdims=True)
        vals.append(m)
        idxs.append(idx)
        l = jnp.where(lane == idx, neg_inf, l)
    es = [jnp.exp(v - vals[0]) for v in vals]
    denom = es[0] + es[1] + es[2] + es[3]

    onehot = jnp.zeros((ts, LANES), F32)
    for idx in idxs:
        onehot = onehot + jnp.where(lane == idx, 1.0, 0.0)
    prefix = jnp.dot(tri_ref[...], onehot.astype(BF16), preferred_element_type=F32) + carry_ref[...]
    idx_out = jnp.zeros((ts, LANES), I32)
    gate_out = jnp.zeros((ts, LANES), F32)
    rank_out = jnp.zeros((ts, LANES), I32)
    for k in range(TOP_K):
        rank = jnp.sum(jnp.where(lane == idxs[k], prefix, 0.0), axis=-1, keepdims=True)
        idx_out = jnp.where(lane_i == k, idxs[k].astype(I32), idx_out)
        gate_out = jnp.where(lane_i == k, es[k] / denom, gate_out)
        rank_out = jnp.where(lane_i == k, rank.astype(I32), rank_out)
    idx_ref[0] = idx_out
    gate_ref[0] = gate_out
    rank_ref[0] = rank_out
    carry = carry_ref[...] + jnp.sum(onehot, axis=0, keepdims=True)
    carry_ref[...] = carry
    cnt_ref[...] = carry


def _out_and_route(a_out, b_out, x, g1, sh2, sc2, norm_g, w_out, router_w, router_b):
    B, S, D = x.shape
    ts = OUT_ROWS
    aw = a_out.shape[-1]
    w_bf = w_out.astype(BF16)
    rw = jnp.concatenate([router_w, jnp.zeros((D, LANES - N_EXPERTS), F32)], axis=1)
    rb = jnp.concatenate([router_b, jnp.zeros((LANES - N_EXPERTS,), F32)]).reshape(1, LANES)
    r = np.arange(ts)
    tri = jnp.asarray(r[:, None] > r[None, :], BF16)
    row = lambda b, i: (b, 0, 0)
    full = lambda b, i: (0, 0)
    tok = lambda w: pl.BlockSpec((1, ts, w), lambda b, i: (b, i, 0))
    return pl.pallas_call(
        _out_kernel,
        grid=(B, S // ts),
        in_specs=[tok(aw), tok(D - aw), tok(D),
                  pl.BlockSpec((1, 1, D), row), pl.BlockSpec((1, 1, D), row),
                  pl.BlockSpec((1, 1, D), row), pl.BlockSpec((1, D), full),
                  pl.BlockSpec((aw, D), full), pl.BlockSpec((D - aw, D), full),
                  pl.BlockSpec((D, LANES), full), pl.BlockSpec((1, LANES), full),
                  pl.BlockSpec((ts, ts), full)],
        out_specs=[tok(D), tok(D), tok(LANES), tok(LANES), tok(LANES),
                   pl.BlockSpec((1, LANES), full)],
        out_shape=[jax.ShapeDtypeStruct((B, S, D), F32), jax.ShapeDtypeStruct((B, S, D), F32),
                   jax.ShapeDtypeStruct((B, S, LANES), I32),
                   jax.ShapeDtypeStruct((B, S, LANES), F32),
                   jax.ShapeDtypeStruct((B, S, LANES), I32),
                   jax.ShapeDtypeStruct((1, LANES), F32)],
        scratch_shapes=[pltpu.VMEM((1, LANES), F32)],
        compiler_params=_cparams(2),
    )(a_out, b_out, x, g1, sh2, sc2, norm_g.reshape(1, D), w_bf[:aw], w_bf[aw:], rw, rb, tri)


def _row_copy_wait(src_ref, dst_ref, sem, rows):
    pltpu.make_async_copy(src_ref.at[pl.ds(0, rows), :], dst_ref.at[pl.ds(0, rows), :], sem).wait()


def _dispatch_kernel(dest_ref, h_ref, xin_ref, xbuf_ref, sem):
    del xin_ref
    rows = h_ref.shape[0]

    def body(r, c):
        for k in range(TOP_K):
            d = dest_ref[0, 0, r * TOP_K + k]
            pltpu.make_async_copy(h_ref.at[pl.ds(r, 1), :], xbuf_ref.at[pl.ds(d, 1), :], sem).start()
        return c

    lax.fori_loop(0, rows, body, 0)
    for _ in range(TOP_K):
        _row_copy_wait(h_ref, xbuf_ref, sem, rows)


def _dispatch(h2, dest, n_rows):
    T, D = h2.shape
    ts = MOVE_ROWS
    xbuf0 = jnp.zeros((n_rows, D), F32)
    return pl.pallas_call(
        _dispatch_kernel,
        grid=(T // ts,),
        in_specs=[pl.BlockSpec((1, 1, ts * TOP_K), lambda i: (i, 0, 0), memory_space=pltpu.SMEM),
                  pl.BlockSpec((ts, D), lambda i: (i, 0)),
                  pl.BlockSpec(memory_space=pl.ANY)],
        out_specs=pl.BlockSpec(memory_space=pl.ANY),
        out_shape=jax.ShapeDtypeStruct((n_rows, D), F32),
        scratch_shapes=[pltpu.SemaphoreType.DMA(())],
        input_output_aliases={2: 0},
        compiler_params=_cparams(1),
    )(dest.reshape(T // ts, 1, ts * TOP_K), h2, xbuf0)


def _moe_kernel(be_ref, nused_ref, x_ref, wg_ref, bg_ref, wu_ref, bu_ref, wd_ref, bd_ref, y_ref,
                wg_s, wu_s, wd_s):
    i = pl.program_id(0)
    used = i < nused_ref[0]
    new_expert = jnp.logical_or(i == 0, be_ref[i] != be_ref[jnp.maximum(i - 1, 0)])

    @pl.when(jnp.logical_and(used, new_expert))
    def _():
        wg_s[...] = wg_ref[0].astype(BF16)
        wu_s[...] = wu_ref[0].astype(BF16)
        wd_s[...] = wd_ref[0].astype(BF16)

    @pl.when(used)
    def _():
        xb = x_ref[...].astype(BF16)
        g = jnp.dot(xb, wg_s[...], preferred_element_type=F32) + bg_ref[0]
        u = jnp.dot(xb, wu_s[...], preferred_element_type=F32) + bu_ref[0]
        g = jnp.minimum(g, SWIGLU_LIMIT)
        u = jnp.clip(u, -SWIGLU_LIMIT, SWIGLU_LIMIT)
        a = g * (1.0 / (1.0 + jnp.exp(-SWIGLU_ALPHA * g))) * (u + 1.0)
        y_ref[...] = jnp.dot(a.astype(BF16), wd_s[...], preferred_element_type=F32) + bd_ref[0]

    @pl.when(jnp.logical_not(used))
    def _():
        y_ref[...] = jnp.zeros(y_ref.shape, F32)


def _moe_experts(xbuf, block_expert, n_used, w_gate, b_gate, w_up, b_up, w_down, b_down):
    R, D = xbuf.shape
    E, _, F = w_gate.shape
    G = MOE_ROWS
    nb = R // G
    last = lambda i, nu: jnp.maximum(jnp.minimum(i, nu[0] - 1), 0)
    blk = lambda i, be, nu: (last(i, nu), 0)
    wsel = lambda i, be, nu: (be[last(i, nu)], 0, 0)
    grid_spec = pltpu.PrefetchScalarGridSpec(
        num_scalar_prefetch=2,
        grid=(nb,),
        in_specs=[pl.BlockSpec((G, D), blk),
                  pl.BlockSpec((1, D, F), wsel), pl.BlockSpec((1, 1, F), wsel),
                  pl.BlockSpec((1, D, F), wsel), pl.BlockSpec((1, 1, F), wsel),
                  pl.BlockSpec((1, F, D), wsel), pl.BlockSpec((1, 1, D), wsel)],
        out_specs=pl.BlockSpec((G, D), lambda i, be, nu: (i, 0)),
        scratch_shapes=[pltpu.VMEM((D, F), BF16), pltpu.VMEM((D, F), BF16),
                        pltpu.VMEM((F, D), BF16)],
    )
    return pl.pallas_call(
        _moe_kernel,
        grid_spec=grid_spec,
        out_shape=jax.ShapeDtypeStruct((R, D), F32),
        compiler_params=_cparams(1),
    )(block_expert, n_used, xbuf, w_gate, b_gate.reshape(E, 1, F),
      w_up, b_up.reshape(E, 1, F), w_down, b_down.reshape(E, 1, D))


def _combine_kernel(dest_ref, gate_ref, x1_ref, g2_ref, ybuf_ref, o_ref, buf_ref, sem):
    rows = x1_ref.shape[0]

    def body(r, c):
        for k in range(TOP_K):
            d = dest_ref[0, 0, r * TOP_K + k]
            pltpu.make_async_copy(ybuf_ref.at[pl.ds(d, 1), :], buf_ref.at[k, pl.ds(r, 1), :],
                                  sem).start()
        return c

    lax.fori_loop(0, rows, body, 0)
    for k in range(TOP_K):
        _row_copy_wait(ybuf_ref, buf_ref.at[k], sem, rows)
    gate = gate_ref[...]
    y = buf_ref[0] * gate[:, 0:1]
    for k in range(1, TOP_K):
        y = y + buf_ref[k] * gate[:, k:k + 1]
    o_ref[...] = x1_ref[...] + g2_ref[0] * y


def _combine(ybuf, dest, gates, x1, g2, seq):
    T, D = x1.shape
    ts = MOVE_ROWS
    per_seq = seq // ts
    return pl.pallas_call(
        _combine_kernel,
        grid=(T // ts,),
        in_specs=[pl.BlockSpec((1, 1, ts * TOP_K), lambda i: (i, 0, 0), memory_space=pltpu.SMEM),
                  pl.BlockSpec((ts, LANES), lambda i: (i, 0)),
                  pl.BlockSpec((ts, D), lambda i: (i, 0)),
                  pl.BlockSpec((1, 1, D), lambda i: (i // per_seq, 0, 0)),
                  pl.BlockSpec(memory_space=pl.ANY)],
        out_specs=pl.BlockSpec((ts, D), lambda i: (i, 0)),
        out_shape=jax.ShapeDtypeStruct((T, D), F32),
        scratch_shapes=[pltpu.VMEM((TOP_K, ts, D), F32), pltpu.SemaphoreType.DMA(())],
        compiler_params=_cparams(1),
    )(dest.reshape(T // ts, 1, ts * TOP_K), gates, x1, g2, ybuf)


def _layer(x, mod, lambda_init, norm1_g, norm2_g, w_in, w_out, a_q_norm_g, a_k_norm_g, a_lambda,
           a_sub_g, b_q_norm_g, b_k_norm_g, b_kv_norm_g, b_w_uv, router_w, router_b, w_gate,
           b_gate, w_up, b_up, w_down, b_down):
    B, S, D = x.shape
    T = B * S
    sh1, sc1, g1, sh2, sc2, g2 = [m.reshape(B, 1, D) for m in jnp.split(mod, 6, axis=-1)]

    (aq1, aq2, ak1, ak2, bq, bk, ik, iq, blat1, av1, iw), plain_a, plain_b = _project(
        x, sh1, sc1, norm1_g, w_in, a_q_norm_g, a_k_norm_g, b_q_norm_g, b_k_norm_g, b_kv_norm_g)
    a_out = _diff_attention(plain_a, aq1, aq2, ak1, ak2, av1, a_lambda, a_sub_g, lambda_init)
    b_out = _dsa_attention(plain_b, iq, iw, ik, bq, bk, blat1, b_w_uv)

    x1, h2, top_idx, gates, rank, counts = _out_and_route(
        a_out, b_out, x, g1, sh2, sc2, norm2_g, w_out, router_w, router_b)

    G = MOE_ROWS
    counts = counts[0, :N_EXPERTS].astype(I32)
    padded = ((counts + G - 1) // G) * G
    pcum = jnp.cumsum(padded)
    poff = pcum - padded
    nb = (T * TOP_K) // G + N_EXPERTS
    starts = jnp.arange(nb, dtype=I32) * G
    block_expert = jnp.minimum(
        jnp.sum((pcum[None, :] <= starts[:, None]).astype(I32), axis=1), N_EXPERTS - 1)
    n_used = (pcum[-1:] // G).astype(I32)
    top_idx = top_idx.reshape(T, LANES)[:, :TOP_K]
    dest = (poff[top_idx] + rank.reshape(T, LANES)[:, :TOP_K]).astype(I32).reshape(T * TOP_K)

    xbuf = _dispatch(h2.reshape(T, D), dest, nb * G)
    ybuf = _moe_experts(xbuf, block_expert.astype(I32), n_used, w_gate, b_gate, w_up, b_up,
                        w_down, b_down)
    out = _combine(ybuf, dest, gates.reshape(T, LANES), x1.reshape(T, D), g2, S)
    return out.reshape(B, S, D)


def kernel(x, c, norm1_g, norm2_g, w_ada, b_ada, w_in, w_out, a_q_norm_g, a_k_norm_g, a_lambda,
           a_sub_g, b_q_norm_g, b_k_norm_g, b_kv_norm_g, b_w_uv, router_w, router_b, w_gate,
           b_gate, w_up, b_up, w_down, b_down):
    depth = w_in.shape[0]
    for l in range(depth):
        lambda_init = 0.8 - 0.6 * math.exp(-0.3 * l)
        mod = _ada(c, w_ada[l], b_ada[l])
        x = _layer(x, mod, lambda_init, norm1_g[l], norm2_g[l], w_in[l], w_out[l], a_q_norm_g[l],
                   a_k_norm_g[l], a_lambda[l], a_sub_g[l], b_q_norm_g[l], b_k_norm_g[l],
                   b_kv_norm_g[l], b_w_uv[l], router_w[l], router_b[l], w_gate[l], b_gate[l],
                   w_up[l], b_up[l], w_down[l], b_down[l])
    return x
```

```python
import functools
import math

import numpy as np
import jax
import jax.numpy as jnp
from jax import lax
from jax.experimental import pallas as pl
from jax.experimental.pallas import tpu as pltpu

F32 = jnp.float32
BF16 = jnp.bfloat16
I32 = jnp.int32

CHUNK = 64
HEAD_DIM = 64
ROT_DIM = HEAD_DIM // 4
ROPE_THETA = 500000.0
RMS_EPS = 1e-6
A_HEADS = 4
A_V_DIM = 2 * HEAD_DIM
B_HEADS = 8
B_LATENT = 128
IDX_HEADS = 8
IDX_TOPK_MAX = 256
N_EXPERTS = 32
TOP_K = 4
SWIGLU_LIMIT = 7.0
SWIGLU_ALPHA = 1.702

LANES = 128
INT_MIN = -(2 ** 31)
NEG_BIG = -1e30
VMEM_LIMIT = 56 * 1024 * 1024

NORM_SLACK = 1.01
PLAIN_EXP_MAX_BOUND = 30.0

PROJ_ROWS = 512
ATT_TQ = 512
ATT_TK = 512
DSA_TQ = 256
DSA_TK = 512
COUNT_ROWS = 128
OUT_ROWS = 512
MOE_ROWS = 512
DISPATCH_ROWS = 1024
COMBINE_ROWS = 512


def _cparams(n_axes):
    return pltpu.CompilerParams(
        dimension_semantics=("arbitrary",) * n_axes, vmem_limit_bytes=VMEM_LIMIT)


def _dot_nt(a, b):
    return lax.dot_general(a, b, (((1,), (1,)), ((), ())), preferred_element_type=F32)


def _ada_kernel(c_ref, w_ref, b_ref, o_ref):
    c = c_ref[...]
    sc = c / (1.0 + jnp.exp(-c))
    o_ref[...] = jnp.dot(sc, w_ref[...], preferred_element_type=F32,
                         precision=lax.Precision.HIGHEST) + b_ref[...]


def _ada(c, w, b):
    B, D = c.shape
    N = w.shape[1]
    return pl.pallas_call(
        _ada_kernel,
        grid=(N // D,),
        in_specs=[pl.BlockSpec((B, D), lambda j: (0, 0)),
                  pl.BlockSpec((D, D), lambda j: (0, j)),
                  pl.BlockSpec((1, D), lambda j: (0, j))],
        out_specs=pl.BlockSpec((B, D), lambda j: (0, j)),
        out_shape=jax.ShapeDtypeStruct((B, N), F32),
        compiler_params=_cparams(1),
    )(c, w, b.reshape(1, N))


C_QK = 0
C_BQ = 1024
C_BKIK = 1536
C_IQ = 1664
C_LAT = 2176
C_AV = 2304
C_IW = 2816
C_END = 2944
N_GAIN = C_IQ


def _group_sumsq(p, bd):
    sq = p * p
    hi = sq.astype(BF16)
    lo = (sq - hi.astype(F32)).astype(BF16)
    return (jnp.dot(hi, bd, preferred_element_type=F32)
            + jnp.dot(lo, bd, preferred_element_type=F32))


def _rope(y, c, s1, s2):
    w = y.shape[1]
    return y * c + pltpu.roll(y, w - ROT_DIM // 2, 1) * s1 + pltpu.roll(y, ROT_DIM // 2, 1) * s2


def _proj_kernel(x_ref, sh_ref, sc_ref, g_ref, w_ref, gain_ref, kb_ref, latg_ref, rc_ref, rs1_ref,
                 rs2_ref, bd_ref,
                 aq1_ref, aq2_ref, ak1_ref, ak2_ref, bq_ref, bk_ref, ik_ref, iq_ref, lat_ref,
                 av_ref, iw_ref):
    x = x_ref[0]
    ms = jnp.mean(x * x, axis=-1, keepdims=True)
    h = x * lax.rsqrt(ms + RMS_EPS) * g_ref[...]
    h = (h * (1.0 + sc_ref[0]) + sh_ref[0]).astype(BF16)

    rc, rs1, rs2 = rc_ref[...], rs1_ref[...], rs2_ref[...]
    bd = bd_ref[...]
    ts = x.shape[0]
    lane = lax.broadcasted_iota(I32, (ts, LANES), 1)
    ones = jnp.ones((ts, LANES), F32)

    def proj(c0, width):
        return jnp.dot(h, w_ref[:, c0:c0 + width], preferred_element_type=F32)

    def normed(p, c0):
        width = p.shape[1]
        ss = _group_sumsq(p, bd[:width, :width])
        return p * lax.rsqrt(ss * (1.0 / HEAD_DIM) + RMS_EPS) * gain_ref[:, c0:c0 + width]

    def store_slots(ref, y, extra, first_head=0):
        for pair in range(y.shape[1] // LANES):
            z = y[:, pair * LANES:(pair + 1) * LANES]
            e = extra[:, pair * LANES:(pair + 1) * LANES]
            even = jnp.where(lane < HEAD_DIM, z,
                             jnp.where(lane == HEAD_DIM, pltpu.roll(e, HEAD_DIM, 1), 0.0))
            odd = jnp.where(lane < HEAD_DIM, pltpu.roll(z, HEAD_DIM, 1),
                            jnp.where(lane == HEAD_DIM, e, 0.0))
            ref[0, first_head + 2 * pair] = even.astype(BF16)
            ref[0, first_head + 2 * pair + 1] = odd.astype(BF16)

    def query(c0):
        y = _rope(normed(proj(c0, 256), c0), rc, rs1, rs2)
        norm = jnp.sqrt(jnp.dot((y * y).astype(BF16), bd, preferred_element_type=F32))
        return y, -norm * kb_ref[:, c0:c0 + 256]

    def key(c0):
        return _rope(normed(proj(c0, 256), c0), rc, rs1, rs2), jnp.ones((ts, 256), F32)

    store_slots(aq1_ref, *query(0))
    store_slots(aq2_ref, *query(256))
    store_slots(ak1_ref, *key(512))
    store_slots(ak2_ref, *key(768))
    for half in range(2):
        store_slots(bq_ref, *query(C_BQ + 256 * half), first_head=4 * half)
        y = _rope(proj(C_IQ + 256 * half, 256), rc, rs1, rs2)
        for j in range(4):
            iq_ref[0, 4 * half + j] = y[:, j * HEAD_DIM:(j + 1) * HEAD_DIM].astype(BF16)

    p = proj(C_BKIK, LANES)
    y = _rope(jnp.where(lane < HEAD_DIM, normed(p, C_BKIK), p),
              rc[:, :LANES], rs1[:, :LANES], rs2[:, :LANES])
    bk_ref[0] = jnp.where(lane < HEAD_DIM, y, jnp.where(lane == HEAD_DIM, 1.0, 0.0)).astype(BF16)
    ik_ref[0] = y[:, HEAD_DIM:].astype(BF16)

    p = proj(C_LAT, LANES)
    ms = jnp.mean(p * p, axis=-1, keepdims=True)
    lat = p * lax.rsqrt(ms + RMS_EPS) * latg_ref[...]
    lat_ref[0] = jnp.concatenate([lat, ones], axis=1).astype(BF16)

    for j in range(A_HEADS):
        av_ref[0, j] = jnp.concatenate([proj(C_AV + j * A_V_DIM, A_V_DIM), ones],
                                       axis=1).astype(BF16)

    p = proj(C_IW, LANES)
    iw_ref[0] = p[:, :IDX_HEADS] * (IDX_HEADS ** -0.5 * HEAD_DIM ** -0.5)


def _rope_tables(S, width):
    half = ROT_DIM // 2
    pos = jnp.arange(S, dtype=F32)
    inv = ROPE_THETA ** (-jnp.arange(0, ROT_DIM, 2, dtype=F32) / ROT_DIM)
    ang = pos[:, None] * inv[None, :]
    cos, sin = jnp.cos(ang), jnp.sin(ang)
    zeros = jnp.zeros((S, HEAD_DIM - ROT_DIM), F32)
    c = jnp.concatenate([cos, cos, zeros + 1.0], axis=1)
    s1 = jnp.concatenate([-sin, jnp.zeros((S, half), F32), zeros], axis=1)
    s2 = jnp.concatenate([jnp.zeros((S, half), F32), sin, zeros], axis=1)
    reps = width // HEAD_DIM
    return tuple(jnp.tile(t, (1, reps)) for t in (c, s1, s2))


def _project(x, sh1, sc1, norm_g, w_in, a_q_g, a_k_g, b_q_g, b_k_g, b_kv_g):
    B, S, D = x.shape
    ts = PROJ_ROWS
    sizes = (256, 256, 256, 256, 512, 512, 64, 128, 512, 64, 8)
    offs = np.concatenate([[0], np.cumsum(sizes)])
    seg = lambda i: w_in[:, offs[i]:offs[i + 1]]
    w_p = jnp.concatenate(
        [seg(0), seg(1), seg(2), seg(3), seg(5), seg(6), seg(9), seg(8), seg(7), seg(4), seg(10),
         jnp.zeros((D, C_END - C_IW - IDX_HEADS), F32)], axis=1).astype(BF16)
    scale = HEAD_DIM ** -0.5
    gain = jnp.concatenate(
        [jnp.tile(a_q_g * scale, 2 * A_HEADS), jnp.tile(a_k_g, 2 * A_HEADS),
         jnp.tile(b_q_g * scale, B_HEADS), b_k_g, jnp.ones((HEAD_DIM,), F32)]).reshape(1, N_GAIN)
    kb_a = 8.0 * NORM_SLACK * jnp.max(jnp.abs(a_k_g))
    kb_b = 8.0 * NORM_SLACK * jnp.max(jnp.abs(b_k_g))
    zeros = lambda n: jnp.zeros((n,), F32)
    kb = jnp.concatenate([zeros(512) + kb_a, zeros(512), zeros(512) + kb_b,
                          zeros(N_GAIN - C_BKIK)]).reshape(1, N_GAIN)
    bound_a = jnp.max(jnp.abs(a_q_g)) * kb_a
    bound_b = jnp.max(jnp.abs(b_q_g)) * kb_b
    rc, rs1, rs2 = _rope_tables(S, 256)
    gid = np.arange(256) // HEAD_DIM
    bd = jnp.asarray(gid[:, None] == gid[None, :], BF16)

    row = lambda b, i: (b, 0, 0)
    full = lambda b, i: (0, 0)
    heads = lambda n, w: pl.BlockSpec((1, n, ts, w), lambda b, i: (b, 0, i, 0))
    flat = lambda w: pl.BlockSpec((1, ts, w), lambda b, i: (b, i, 0))
    hshape = lambda n, w: jax.ShapeDtypeStruct((B, n, S, w), BF16)
    outs = pl.pallas_call(
        _proj_kernel,
        grid=(B, S // ts),
        in_specs=[pl.BlockSpec((1, ts, D), lambda b, i: (b, i, 0)),
                  pl.BlockSpec((1, 1, D), row), pl.BlockSpec((1, 1, D), row),
                  pl.BlockSpec((1, D), full),
                  pl.BlockSpec((D, C_END), full),
                  pl.BlockSpec((1, N_GAIN), full),
                  pl.BlockSpec((1, N_GAIN), full),
                  pl.BlockSpec((1, LANES), full),
                  pl.BlockSpec((ts, 256), lambda b, i: (i, 0)),
                  pl.BlockSpec((ts, 256), lambda b, i: (i, 0)),
                  pl.BlockSpec((ts, 256), lambda b, i: (i, 0)),
                  pl.BlockSpec((256, 256), full)],
        out_specs=[heads(A_HEADS, LANES)] * 4
        + [heads(B_HEADS, LANES), flat(LANES), flat(HEAD_DIM), heads(IDX_HEADS, HEAD_DIM),
           flat(2 * B_LATENT), heads(A_HEADS, 2 * A_V_DIM), flat(IDX_HEADS)],
        out_shape=[hshape(A_HEADS, LANES)] * 4
        + [hshape(B_HEADS, LANES), jax.ShapeDtypeStruct((B, S, LANES), BF16),
           jax.ShapeDtypeStruct((B, S, HEAD_DIM), BF16), hshape(IDX_HEADS, HEAD_DIM),
           jax.ShapeDtypeStruct((B, S, 2 * B_LATENT), BF16), hshape(A_HEADS, 2 * A_V_DIM),
           jax.ShapeDtypeStruct((B, S, IDX_HEADS), F32)],
        compiler_params=_cparams(2),
    )(x, sh1, sc1, norm_g.reshape(1, D), w_p, gain, kb, b_kv_g.reshape(1, B_LATENT), rc, rs1, rs2,
      bd)
    plain_a = (bound_a <= PLAIN_EXP_MAX_BOUND).astype(I32).reshape(1)
    plain_b = (bound_b <= PLAIN_EXP_MAX_BOUND).astype(I32).reshape(1)
    return outs, plain_a, plain_b


def _softmax_init(plain, m_ref, acc_ref):
    acc_ref[...] = jnp.zeros(acc_ref.shape, F32)
    if not plain:
        m_ref[...] = jnp.full(m_ref.shape, NEG_BIG, F32)


def _softmax_step(plain, s, v1, m_ref, acc_ref):
    if plain:
        acc_ref[...] += jnp.dot(jnp.exp(s).astype(BF16), v1, preferred_element_type=F32)
        return
    m_old = m_ref[...]
    m_new = jnp.maximum(m_old, jnp.max(s, axis=-1, keepdims=True))
    p = jnp.exp(s - m_new).astype(BF16)
    acc_ref[...] = (jnp.exp(m_old - m_new) * acc_ref[...]
                    + jnp.dot(p, v1, preferred_element_type=F32))
    m_ref[...] = m_new


def _softmax_result(acc_ref, width):
    acc = acc_ref[...]
    return acc[:, :width] / acc[:, width:width + 1]


def _chunk_mask(q0, k0, tq, tk):
    qc = (q0 + lax.broadcasted_iota(I32, (tq, tk), 0)) // CHUNK
    kc = (k0 + lax.broadcasted_iota(I32, (tq, tk), 1)) // CHUNK
    return kc <= qc


def _diff_attn_kernel(plain_ref, lam_ref, subg_ref, q1_ref, q2_ref, k1_ref, k2_ref, v_ref, o_ref,
                      m1_ref, acc1_ref, m2_ref, acc2_ref, *, lambda_init):
    tq, tk = ATT_TQ, ATT_TK
    i = pl.program_id(2)
    n_tiles = ((i + 1) * tq + tk - 1) // tk

    def attend(plain):
        q1 = q1_ref[0, 0]
        q2 = q2_ref[0, 0]
        _softmax_init(plain, m1_ref, acc1_ref)
        _softmax_init(plain, m2_ref, acc2_ref)

        def tile(j, masked):
            ks = pl.multiple_of(j * tk, tk)
            v1 = v_ref[0, 0, pl.ds(ks, tk), :]
            s1 = _dot_nt(q1, k1_ref[0, 0, pl.ds(ks, tk), :])
            s2 = _dot_nt(q2, k2_ref[0, 0, pl.ds(ks, tk), :])
            if masked:
                ok = _chunk_mask(i * tq, ks, tq, tk)
                s1 = jnp.where(ok, s1, NEG_BIG)
                s2 = jnp.where(ok, s2, NEG_BIG)
            _softmax_step(plain, s1, v1, m1_ref, acc1_ref)
            _softmax_step(plain, s2, v1, m2_ref, acc2_ref)

        def body(j, c):
            tile(j, False)
            return c

        lax.fori_loop(0, n_tiles - 1, body, 0)
        tile(n_tiles - 1, True)

    pl.when(plain_ref[0] == 1)(lambda: attend(True))
    pl.when(plain_ref[0] != 1)(lambda: attend(False))

    lv = lam_ref[...]
    lam = (jnp.exp(jnp.sum(lv[0:1] * lv[1:2], axis=-1, keepdims=True))
           - jnp.exp(jnp.sum(lv[2:3] * lv[3:4], axis=-1, keepdims=True)) + lambda_init)
    o = _softmax_result(acc1_ref, A_V_DIM) - lam * _softmax_result(acc2_ref, A_V_DIM)
    ms = jnp.mean(o * o, axis=-1, keepdims=True)
    o = o * lax.rsqrt(ms + RMS_EPS) * subg_ref[...] * (1.0 - lambda_init)
    o_ref[0] = o.astype(BF16)


def _diff_attention(plain, q1, q2, k1, k2, v1, a_lambda, sub_g, lambda_init):
    B, H, S, dq = q1.shape
    tq = ATT_TQ
    qspec = pl.BlockSpec((1, 1, tq, dq), lambda b, h, i, p: (b, h, i, 0))
    kspec = pl.BlockSpec((1, 1, S, dq), lambda b, h, i, p: (b, h, 0, 0))
    col = lambda: pltpu.VMEM((tq, 1), F32)
    acc = lambda: pltpu.VMEM((tq, 2 * A_V_DIM), F32)
    grid_spec = pltpu.PrefetchScalarGridSpec(
        num_scalar_prefetch=1,
        grid=(B, H, S // tq),
        in_specs=[pl.BlockSpec((4, HEAD_DIM), lambda b, h, i, p: (0, 0)),
                  pl.BlockSpec((1, A_V_DIM), lambda b, h, i, p: (0, 0)),
                  qspec, qspec, kspec, kspec,
                  pl.BlockSpec((1, 1, S, 2 * A_V_DIM), lambda b, h, i, p: (b, h, 0, 0))],
        out_specs=pl.BlockSpec((1, tq, A_V_DIM), lambda b, h, i, p: (b, i, h)),
        scratch_shapes=[col(), acc(), col(), acc()],
    )
    return pl.pallas_call(
        functools.partial(_diff_attn_kernel, lambda_init=lambda_init),
        grid_spec=grid_spec,
        out_shape=jax.ShapeDtypeStruct((B, S, H * A_V_DIM), BF16),
        compiler_params=_cparams(3),
    )(plain, a_lambda, sub_g.reshape(1, A_V_DIM), q1, q2, k1, k2, v1)


def _dsa_kernel(plain_ref, iq_ref, iw_ref, ik_ref, q_ref, k_ref, lat_ref, wuv_ref, tri_ref, o_ref,
                score_ref, thr_ref, m_ref, acc_ref, *, topk):
    tq, tk = DSA_TQ, DSA_TK
    nh = B_HEADS
    i = pl.program_id(1)
    n_tiles = ((i + 1) * tq + tk - 1) // tk

    iq = iq_ref[0].reshape(IDX_HEADS * tq, HEAD_DIM)
    iw = iw_ref[0]

    def score_tile(j, masked):
        ks = pl.multiple_of(j * tk, tk)
        rel = jnp.maximum(_dot_nt(iq, ik_ref[0, pl.ds(ks, tk), :]), 0.0)
        rel = rel.reshape(IDX_HEADS, tq, tk)
        score = rel[0] * iw[:, 0:1]
        for h in range(1, IDX_HEADS):
            score = score + rel[h] * iw[:, h:h + 1]
        if masked:
            score = jnp.where(_chunk_mask(i * tq, ks, tq, tk), score, -jnp.inf)
        score_ref[:, pl.ds(ks, tk)] = score

    def score_body(j, c):
        score_tile(j, False)
        return c

    lax.fori_loop(0, n_tiles - 1, score_body, 0)
    score_tile(n_tiles - 1, True)

    def image_to_float(t):
        return pltpu.bitcast(jnp.where(t < 0, t ^ 0x7FFFFFFF, t), F32)

    def count(pred, cand):
        parts = []
        for r0 in range(0, tq, COUNT_ROWS):
            rows = pl.ds(r0, COUNT_ROWS)
            cand_r = cand[r0:r0 + COUNT_ROWS]

            def body(j, acc, rows=rows, cand_r=cand_r):
                ks = pl.multiple_of(j * tk, tk)
                hit = jnp.where(pred(score_ref[rows, pl.ds(ks, tk)], cand_r), 1.0, 0.0)
                for c in range(tk // LANES):
                    acc = acc + hit[:, c * LANES:(c + 1) * LANES]
                return acc
            parts.append(lax.fori_loop(0, n_tiles, body, jnp.zeros((COUNT_ROWS, LANES), F32)))
        acc = jnp.concatenate(parts, axis=0)
        return jnp.sum(acc, axis=-1, keepdims=True)

    ge = lambda s, c: s >= c
    gt = lambda s, c: s > c

    def all_of(flags):
        return (jnp.min(flags) > 0.0).astype(I32)

    zero = jnp.zeros((tq, 1), F32)
    n_ge0 = count(ge, zero)
    tie0 = jnp.where(jnp.logical_and(count(gt, zero) < topk, n_ge0 > topk), 1.0, 0.0)
    t0 = jnp.where(n_ge0 >= topk, 0, INT_MIN)
    settled0 = jnp.where(n_ge0 == topk, 1.0, tie0)

    def bit_body(state):
        step, t, settled, _ = state
        bit = jnp.left_shift(jnp.int32(1), 31 - step)
        cand = t + bit
        n_ge = count(ge, image_to_float(cand))
        t = jnp.where(settled > 0.0, t, jnp.where(n_ge >= topk, cand, t))
        settled = jnp.where(n_ge == topk, 1.0, settled)
        return step + 1, t, settled, all_of(settled)

    def bit_cond(state):
        step, _, _, all_settled = state
        return jnp.logical_and(step < 32, all_settled == 0)

    _, t, _, all_settled = lax.while_loop(
        bit_cond, bit_body, (jnp.int32(1), t0, settled0, all_of(settled0)))
    thr_ref[...] = jnp.where(t == INT_MIN, jnp.finfo(F32).min, image_to_float(t))

    def min_where(pred, cand):
        def body(j, acc):
            ks = pl.multiple_of(j * tk, tk)
            s = score_ref[:, pl.ds(ks, tk)]
            s = jnp.where(pred(s, cand), s, jnp.inf)
            for c in range(tk // LANES):
                acc = jnp.minimum(acc, s[:, c * LANES:(c + 1) * LANES])
            return acc
        acc = lax.fori_loop(0, n_tiles, body, jnp.full((tq, LANES), jnp.inf, F32))
        return jnp.min(acc, axis=-1, keepdims=True)

    @pl.when(all_settled == 0)
    def _():
        thr0 = thr_ref[...]
        low = min_where(ge, thr0)
        nxt = min_where(gt, low)
        thr_ref[...] = jnp.where(count(ge, nxt) >= topk, nxt,
                                 jnp.where(low < jnp.inf, low, thr0))

    @pl.when(jnp.logical_or(all_settled == 0, jnp.max(tie0) > 0.0))
    def _():
        thr = thr_ref[...]
        quota = topk - count(gt, thr)

        def tie_body(j, carry):
            sl = pl.ds(pl.multiple_of(j * tk, tk), tk)
            score = score_ref[:, sl]
            eq = score == thr
            prefix = carry + jnp.dot(jnp.where(eq, 1.0, 0.0).astype(BF16), tri_ref[...],
                                     preferred_element_type=F32)
            score_ref[:, sl] = jnp.where(eq & (prefix > quota), -jnp.inf, score)
            return prefix[:, tk - 1:tk]

        lax.fori_loop(0, n_tiles, tie_body, jnp.zeros((tq, 1), F32))

    thr = thr_ref[...]

    def attend(plain):
        q = q_ref[0].reshape(nh * tq, LANES)
        _softmax_init(plain, m_ref, acc_ref)

        def attn_body(j, c):
            ks = pl.multiple_of(j * tk, tk)
            s = _dot_nt(q, k_ref[0, pl.ds(ks, tk), :]).reshape(nh, tq, tk)
            sel = score_ref[:, pl.ds(ks, tk)] >= thr
            s = jnp.where(sel[None], s, NEG_BIG).reshape(nh * tq, tk)
            _softmax_step(plain, s, lat_ref[0, pl.ds(ks, tk), :], m_ref, acc_ref)
            return c

        lax.fori_loop(0, n_tiles, attn_body, 0)

    pl.when(plain_ref[0] == 1)(lambda: attend(True))
    pl.when(plain_ref[0] != 1)(lambda: attend(False))

    o = _softmax_result(acc_ref, B_LATENT).astype(BF16)
    for h in range(nh):
        oh = jnp.dot(o[h * tq:(h + 1) * tq], wuv_ref[h], preferred_element_type=F32)
        o_ref[0, :, h * HEAD_DIM:(h + 1) * HEAD_DIM] = oh.astype(BF16)


def _dsa_attention(plain, iq, iw, ik, q, k, lat1, w_uv):
    B, nh, S, dq = q.shape
    dh = HEAD_DIM
    tq = DSA_TQ
    topk = min(IDX_TOPK_MAX, S // 4)
    col = np.arange(DSA_TK)
    tri = jnp.asarray(col[:, None] <= col[None, :], BF16)
    hspec = lambda w: pl.BlockSpec((1, nh, tq, w), lambda b, i, p: (b, 0, i, 0))
    kspec = lambda w: pl.BlockSpec((1, S, w), lambda b, i, p: (b, 0, 0))
    grid_spec = pltpu.PrefetchScalarGridSpec(
        num_scalar_prefetch=1,
        grid=(B, S // tq),
        in_specs=[hspec(dh),
                  pl.BlockSpec((1, tq, IDX_HEADS), lambda b, i, p: (b, i, 0)),
                  kspec(dh), hspec(dq), kspec(dq), kspec(2 * B_LATENT),
                  pl.BlockSpec((nh, B_LATENT, dh), lambda b, i, p: (0, 0, 0)),
                  pl.BlockSpec((DSA_TK, DSA_TK), lambda b, i, p: (0, 0))],
        out_specs=pl.BlockSpec((1, tq, nh * dh), lambda b, i, p: (b, i, 0)),
        scratch_shapes=[pltpu.VMEM((tq, S), F32),
                        pltpu.VMEM((tq, 1), F32),
                        pltpu.VMEM((nh * tq, 1), F32),
                        pltpu.VMEM((nh * tq, 2 * B_LATENT), F32)],
    )
    return pl.pallas_call(
        functools.partial(_dsa_kernel, topk=topk),
        grid_spec=grid_spec,
        out_shape=jax.ShapeDtypeStruct((B, S, nh * dh), BF16),
        compiler_params=_cparams(2),
    )(plain, iq, iw, ik, q, k, lat1, w_uv.astype(BF16), tri)


def _out_kernel(a_ref, b_ref, x_ref, g1_ref, sh_ref, sc_ref, ng_ref, woa_ref, wob_ref, rw_ref,
                rb_ref, tri_ref, x1_ref, h2_ref, idx_ref, gate_ref, rank_ref, cnt_ref, carry_ref):
    first = jnp.logical_and(pl.program_id(0) == 0, pl.program_id(1) == 0)

    @pl.when(first)
    def _():
        carry_ref[...] = jnp.zeros(carry_ref.shape, F32)

    mix = (jnp.dot(a_ref[0], woa_ref[...], preferred_element_type=F32)
           + jnp.dot(b_ref[0], wob_ref[...], preferred_element_type=F32))
    x1 = x_ref[0] + g1_ref[0] * mix
    x1_ref[0] = x1
    ms = jnp.mean(x1 * x1, axis=-1, keepdims=True)
    h2 = x1 * lax.rsqrt(ms + RMS_EPS) * ng_ref[...]
    h2 = h2 * (1.0 + sc_ref[0]) + sh_ref[0]
    h2_ref[0] = h2

    logits = jnp.dot(h2, rw_ref[...], preferred_element_type=F32,
                     precision=lax.Precision.HIGHEST) + rb_ref[...]
    ts = logits.shape[0]
    lane_i = lax.broadcasted_iota(I32, (ts, LANES), 1)
    lane = lane_i.astype(F32)
    neg_inf = jnp.float32(-jnp.inf)
    l = jnp.where(lane_i < N_EXPERTS, logits, neg_inf)
    vals, idxs = [], []
    for _ in range(TOP_K):
        m = jnp.max(l, axis=-1, keepdims=True)
        idx = jnp.min(jnp.where(l == m, lane, float(LANES)), axis=-1, keepdims=True)
        vals.append(m)
        idxs.append(idx)
        l = jnp.where(lane == idx, neg_inf, l)
    es = [jnp.exp(v - vals[0]) for v in vals]
    denom = es[0] + es[1] + es[2] + es[3]

    onehot = jnp.zeros((ts, LANES), F32)
    for idx in idxs:
        onehot = onehot + jnp.where(lane == idx, 1.0, 0.0)
    prefix = jnp.dot(tri_ref[...], onehot.astype(BF16), preferred_element_type=F32) + carry_ref[...]
    idx_out = jnp.zeros((ts, LANES), I32)
    gate_out = jnp.zeros((ts, LANES), F32)
    rank_out = jnp.zeros((ts, LANES), I32)
    for k in range(TOP_K):
        rank = jnp.sum(jnp.where(lane == idxs[k], prefix, 0.0), axis=-1, keepdims=True)
        idx_out = jnp.where(lane_i == k, idxs[k].astype(I32), idx_out)
        gate_out = jnp.where(lane_i == k, es[k] / denom, gate_out)
        rank_out = jnp.where(lane_i == k, rank.astype(I32), rank_out)
    idx_ref[0] = idx_out
    gate_ref[0] = gate_out
    rank_ref[0] = rank_out
    carry = carry_ref[...] + jnp.sum(onehot, axis=0, keepdims=True)
    carry_ref[...] = carry
    cnt_ref[...] = carry


def _out_and_route(a_out, b_out, x, g1, sh2, sc2, norm_g, w_out, router_w, router_b):
    B, S, D = x.shape
    ts = OUT_ROWS
    aw = a_out.shape[-1]
    w_bf = w_out.astype(BF16)
    rw = jnp.concatenate([router_w, jnp.zeros((D, LANES - N_EXPERTS), F32)], axis=1)
    rb = jnp.concatenate([router_b, jnp.zeros((LANES - N_EXPERTS,), F32)]).reshape(1, LANES)
    r = np.arange(ts)
    tri = jnp.asarray(r[:, None] > r[None, :], BF16)
    row = lambda b, i: (b, 0, 0)
    full = lambda b, i: (0, 0)
    tok = lambda w: pl.BlockSpec((1, ts, w), lambda b, i: (b, i, 0))
    return pl.pallas_call(
        _out_kernel,
        grid=(B, S // ts),
        in_specs=[tok(aw), tok(D - aw), tok(D),
                  pl.BlockSpec((1, 1, D), row), pl.BlockSpec((1, 1, D), row),
                  pl.BlockSpec((1, 1, D), row), pl.BlockSpec((1, D), full),
                  pl.BlockSpec((aw, D), full), pl.BlockSpec((D - aw, D), full),
                  pl.BlockSpec((D, LANES), full), pl.BlockSpec((1, LANES), full),
                  pl.BlockSpec((ts, ts), full)],
        out_specs=[tok(D), tok(D), tok(LANES), tok(LANES), tok(LANES),
                   pl.BlockSpec((1, LANES), full)],
        out_shape=[jax.ShapeDtypeStruct((B, S, D), F32), jax.ShapeDtypeStruct((B, S, D), F32),
                   jax.ShapeDtypeStruct((B, S, LANES), I32),
                   jax.ShapeDtypeStruct((B, S, LANES), F32),
                   jax.ShapeDtypeStruct((B, S, LANES), I32),
                   jax.ShapeDtypeStruct((1, LANES), F32)],
        scratch_shapes=[pltpu.VMEM((1, LANES), F32)],
        compiler_params=_cparams(2),
    )(a_out, b_out, x, g1, sh2, sc2, norm_g.reshape(1, D), w_bf[:aw], w_bf[aw:], rw, rb, tri)


def _row_copy_wait(src_ref, dst_ref, sem, rows):
    pltpu.make_async_copy(src_ref.at[pl.ds(0, rows), :], dst_ref.at[pl.ds(0, rows), :], sem).wait()


def _dispatch_kernel(dest_ref, h_ref, xin_ref, xbuf_ref, sem):
    del xin_ref
    rows = h_ref.shape[0]

    def body(r, c):
        for k in range(TOP_K):
            d = dest_ref[0, 0, r * TOP_K + k]
            pltpu.make_async_copy(h_ref.at[pl.ds(r, 1), :], xbuf_ref.at[pl.ds(d, 1), :], sem).start()
        return c

    lax.fori_loop(0, rows, body, 0)
    for _ in range(TOP_K):
        _row_copy_wait(h_ref, xbuf_ref, sem, rows)


def _dispatch(h2, dest, n_rows):
    T, D = h2.shape
    ts = min(DISPATCH_ROWS, T)
    xbuf0 = jnp.zeros((n_rows, D), F32)
    return pl.pallas_call(
        _dispatch_kernel,
        grid=(T // ts,),
        in_specs=[pl.BlockSpec((1, 1, ts * TOP_K), lambda i: (i, 0, 0), memory_space=pltpu.SMEM),
                  pl.BlockSpec((ts, D), lambda i: (i, 0)),
                  pl.BlockSpec(memory_space=pl.ANY)],
        out_specs=pl.BlockSpec(memory_space=pl.ANY),
        out_shape=jax.ShapeDtypeStruct((n_rows, D), F32),
        scratch_shapes=[pltpu.SemaphoreType.DMA(())],
        input_output_aliases={2: 0},
        compiler_params=_cparams(1),
    )(dest.reshape(T // ts, 1, ts * TOP_K), h2, xbuf0)


def _moe_kernel(be_ref, nused_ref, x_ref, wg_ref, bg_ref, wu_ref, bu_ref, wd_ref, bd_ref, y_ref,
                wg_s, wu_s, wd_s):
    i = pl.program_id(0)
    used = i < nused_ref[0]
    new_expert = jnp.logical_or(i == 0, be_ref[i] != be_ref[jnp.maximum(i - 1, 0)])

    @pl.when(jnp.logical_and(used, new_expert))
    def _():
        wg_s[...] = wg_ref[0].astype(BF16)
        wu_s[...] = wu_ref[0].astype(BF16)
        wd_s[...] = wd_ref[0].astype(BF16)

    @pl.when(used)
    def _():
        xb = x_ref[...].astype(BF16)
        g = jnp.dot(xb, wg_s[...], preferred_element_type=F32) + bg_ref[0]
        u = jnp.dot(xb, wu_s[...], preferred_element_type=F32) + bu_ref[0]
        g = jnp.minimum(g, SWIGLU_LIMIT)
        u = jnp.clip(u, -SWIGLU_LIMIT, SWIGLU_LIMIT)
        a = g * (1.0 / (1.0 + jnp.exp(-SWIGLU_ALPHA * g))) * (u + 1.0)
        y_ref[...] = jnp.dot(a.astype(BF16), wd_s[...], preferred_element_type=F32) + bd_ref[0]

    @pl.when(jnp.logical_not(used))
    def _():
        y_ref[...] = jnp.zeros(y_ref.shape, F32)


def _moe_experts(xbuf, block_expert, n_used, w_gate, b_gate, w_up, b_up, w_down, b_down):
    R, D = xbuf.shape
    E, _, F = w_gate.shape
    G = MOE_ROWS
    nb = R // G
    last = lambda i, nu: jnp.maximum(jnp.minimum(i, nu[0] - 1), 0)
    blk = lambda i, be, nu: (last(i, nu), 0)
    wsel = lambda i, be, nu: (be[last(i, nu)], 0, 0)
    grid_spec = pltpu.PrefetchScalarGridSpec(
        num_scalar_prefetch=2,
        grid=(nb,),
        in_specs=[pl.BlockSpec((G, D), blk),
                  pl.BlockSpec((1, D, F), wsel), pl.BlockSpec((1, 1, F), wsel),
                  pl.BlockSpec((1, D, F), wsel), pl.BlockSpec((1, 1, F), wsel),
                  pl.BlockSpec((1, F, D), wsel), pl.BlockSpec((1, 1, D), wsel)],
        out_specs=pl.BlockSpec((G, D), lambda i, be, nu: (i, 0)),
        scratch_shapes=[pltpu.VMEM((D, F), BF16), pltpu.VMEM((D, F), BF16),
                        pltpu.VMEM((F, D), BF16)],
    )
    return pl.pallas_call(
        _moe_kernel,
        grid_spec=grid_spec,
        out_shape=jax.ShapeDtypeStruct((R, D), F32),
        compiler_params=_cparams(1),
    )(block_expert, n_used, xbuf, w_gate, b_gate.reshape(E, 1, F),
      w_up, b_up.reshape(E, 1, F), w_down, b_down.reshape(E, 1, D))


def _combine_kernel(dest_ref, gate_ref, x1_ref, g2_ref, ybuf_ref, o_ref, buf_ref, sem):
    rows = x1_ref.shape[0]

    def body(r, c):
        for k in range(TOP_K):
            d = dest_ref[0, 0, r * TOP_K + k]
            pltpu.make_async_copy(ybuf_ref.at[pl.ds(d, 1), :], buf_ref.at[k, pl.ds(r, 1), :],
                                  sem).start()
        return c

    lax.fori_loop(0, rows, body, 0)
    for k in range(TOP_K):
        _row_copy_wait(ybuf_ref, buf_ref.at[k], sem, rows)
    gate = gate_ref[...]
    y = buf_ref[0] * gate[:, 0:1]
    for k in range(1, TOP_K):
        y = y + buf_ref[k] * gate[:, k:k + 1]
    o_ref[...] = x1_ref[...] + g2_ref[0] * y


def _combine(ybuf, dest, gates, x1, g2, seq):
    T, D = x1.shape
    ts = COMBINE_ROWS
    per_seq = seq // ts
    return pl.pallas_call(
        _combine_kernel,
        grid=(T // ts,),
        in_specs=[pl.BlockSpec((1, 1, ts * TOP_K), lambda i: (i, 0, 0), memory_space=pltpu.SMEM),
                  pl.BlockSpec((ts, LANES), lambda i: (i, 0)),
                  pl.BlockSpec((ts, D), lambda i: (i, 0)),
                  pl.BlockSpec((1, 1, D), lambda i: (i // per_seq, 0, 0)),
                  pl.BlockSpec(memory_space=pl.ANY)],
        out_specs=pl.BlockSpec((ts, D), lambda i: (i, 0)),
        out_shape=jax.ShapeDtypeStruct((T, D), F32),
        scratch_shapes=[pltpu.VMEM((TOP_K, ts, D), F32), pltpu.SemaphoreType.DMA(())],
        compiler_params=_cparams(1),
    )(dest.reshape(T // ts, 1, ts * TOP_K), gates, x1, g2, ybuf)


def _layer(x, mod, lambda_init, norm1_g, norm2_g, w_in, w_out, a_q_norm_g, a_k_norm_g, a_lambda,
           a_sub_g, b_q_norm_g, b_k_norm_g, b_kv_norm_g, b_w_uv, router_w, router_b, w_gate,
           b_gate, w_up, b_up, w_down, b_down):
    B, S, D = x.shape
    T = B * S
    sh1, sc1, g1, sh2, sc2, g2 = [m.reshape(B, 1, D) for m in jnp.split(mod, 6, axis=-1)]

    (aq1, aq2, ak1, ak2, bq, bk, ik, iq, blat1, av1, iw), plain_a, plain_b = _project(
        x, sh1, sc1, norm1_g, w_in, a_q_norm_g, a_k_norm_g, b_q_norm_g, b_k_norm_g, b_kv_norm_g)
    a_out = _diff_attention(plain_a, aq1, aq2, ak1, ak2, av1, a_lambda, a_sub_g, lambda_init)
    b_out = _dsa_attention(plain_b, iq, iw, ik, bq, bk, blat1, b_w_uv)

    x1, h2, top_idx, gates, rank, counts = _out_and_route(
        a_out, b_out, x, g1, sh2, sc2, norm2_g, w_out, router_w, router_b)

    G = MOE_ROWS
    counts = counts[0, :N_EXPERTS].astype(I32)
    padded = ((counts + G - 1) // G) * G
    pcum = jnp.cumsum(padded)
    poff = pcum - padded
    nb = (T * TOP_K) // G + N_EXPERTS
    starts = jnp.arange(nb, dtype=I32) * G
    block_expert = jnp.minimum(
        jnp.sum((pcum[None, :] <= starts[:, None]).astype(I32), axis=1), N_EXPERTS - 1)
    n_used = (pcum[-1:] // G).astype(I32)
    top_idx = top_idx.reshape(T, LANES)[:, :TOP_K]
    dest = (poff[top_idx] + rank.reshape(T, LANES)[:, :TOP_K]).astype(I32).reshape(T * TOP_K)

    xbuf = _dispatch(h2.reshape(T, D), dest, nb * G)
    ybuf = _moe_experts(xbuf, block_expert.astype(I32), n_used, w_gate, b_gate, w_up, b_up,
                        w_down, b_down)
    out = _combine(ybuf, dest, gates.reshape(T, LANES), x1.reshape(T, D), g2, S)
    return out.reshape(B, S, D)


def kernel(x, c, norm1_g, norm2_g, w_ada, b_ada, w_in, w_out, a_q_norm_g, a_k_norm_g, a_lambda,
           a_sub_g, b_q_norm_g, b_k_norm_g, b_kv_norm_g, b_w_uv, router_w, router_b, w_gate,
           b_gate, w_up, b_up, w_down, b_down):
    depth = w_in.shape[0]
    for l in range(depth):
        lambda_init = 0.8 - 0.6 * math.exp(-0.3 * l)
        mod = _ada(c, w_ada[l], b_ada[l])
        x = _layer(x, mod, lambda_init, norm1_g[l], norm2_g[l], w_in[l], w_out[l], a_q_norm_g[l],
                   a_k_norm_g[l], a_lambda[l], a_sub_g[l], b_q_norm_g[l], b_k_norm_g[l],
                   b_kv_norm_g[l], b_w_uv[l], router_w[l], router_b[l], w_gate[l], b_gate[l],
                   w_up[l], b_up[l], w_down[l], b_down[l])
    return x
```

```python
import functools
import math

import numpy as np
import jax
import jax.numpy as jnp
from jax import lax
from jax.experimental import pallas as pl
from jax.experimental.pallas import tpu as pltpu

F32 = jnp.float32
BF16 = jnp.bfloat16
I32 = jnp.int32

CHUNK = 64
HEAD_DIM = 64
ROT_DIM = HEAD_DIM // 4
ROPE_THETA = 500000.0
RMS_EPS = 1e-6
A_HEADS = 4
A_V_DIM = 2 * HEAD_DIM
B_HEADS = 8
B_LATENT = 128
IDX_HEADS = 8
IDX_TOPK_MAX = 256
N_EXPERTS = 32
TOP_K = 4
SWIGLU_LIMIT = 7.0
SWIGLU_ALPHA = 1.702

LANES = 128
INT_MIN = -(2 ** 31)
NEG_BIG = -1e30
VMEM_LIMIT = 56 * 1024 * 1024

NORM_SLACK = 1.01
PLAIN_EXP_MAX_BOUND = 30.0

PROJ_ROWS = 512
ATT_TQ = 512
ATT_TK = 512
DSA_TQ = 256
DSA_TK = 512
COUNT_ROWS = 128
OUT_ROWS = 512
MOE_ROWS = 512
COMBINE_ROWS = 512


def _cparams(n_axes):
    return pltpu.CompilerParams(
        dimension_semantics=("arbitrary",) * n_axes, vmem_limit_bytes=VMEM_LIMIT)


def _dot_nt(a, b):
    return lax.dot_general(a, b, (((1,), (1,)), ((), ())), preferred_element_type=F32)


def _ada_kernel(c_ref, w_ref, b_ref, o_ref):
    c = c_ref[...]
    sc = c / (1.0 + jnp.exp(-c))
    o_ref[...] = jnp.dot(sc, w_ref[...], preferred_element_type=F32,
                         precision=lax.Precision.HIGHEST) + b_ref[...]


def _ada(c, w, b):
    B, D = c.shape
    N = w.shape[1]
    return pl.pallas_call(
        _ada_kernel,
        grid=(N // D,),
        in_specs=[pl.BlockSpec((B, D), lambda j: (0, 0)),
                  pl.BlockSpec((D, D), lambda j: (0, j)),
                  pl.BlockSpec((1, D), lambda j: (0, j))],
        out_specs=pl.BlockSpec((B, D), lambda j: (0, j)),
        out_shape=jax.ShapeDtypeStruct((B, N), F32),
        compiler_params=_cparams(1),
    )(c, w, b.reshape(1, N))


C_QK = 0
C_BQ = 1024
C_BKIK = 1536
C_IQ = 1664
C_LAT = 2176
C_AV = 2304
C_IW = 2816
C_END = 2944
N_GAIN = C_IQ


def _group_sumsq(p, bd):
    sq = p * p
    hi = sq.astype(BF16)
    lo = (sq - hi.astype(F32)).astype(BF16)
    return (jnp.dot(hi, bd, preferred_element_type=F32)
            + jnp.dot(lo, bd, preferred_element_type=F32))


def _rope(y, c, s1, s2):
    w = y.shape[1]
    return y * c + pltpu.roll(y, w - ROT_DIM // 2, 1) * s1 + pltpu.roll(y, ROT_DIM // 2, 1) * s2


def _proj_kernel(x_ref, sh_ref, sc_ref, g_ref, w_ref, gain_ref, kb_ref, latg_ref, rc_ref, rs1_ref,
                 rs2_ref, bd_ref,
                 aq1_ref, aq2_ref, ak1_ref, ak2_ref, bq_ref, bk_ref, ik_ref, iq_ref, lat_ref,
                 av_ref, iw_ref):
    x = x_ref[0]
    ms = jnp.mean(x * x, axis=-1, keepdims=True)
    h = x * lax.rsqrt(ms + RMS_EPS) * g_ref[...]
    h = (h * (1.0 + sc_ref[0]) + sh_ref[0]).astype(BF16)

    rc, rs1, rs2 = rc_ref[...], rs1_ref[...], rs2_ref[...]
    bd = bd_ref[...]
    ts = x.shape[0]
    lane = lax.broadcasted_iota(I32, (ts, LANES), 1)
    ones = jnp.ones((ts, LANES), F32)

    def proj(c0, width):
        return jnp.dot(h, w_ref[:, c0:c0 + width], preferred_element_type=F32)

    def normed(p, c0):
        width = p.shape[1]
        ss = _group_sumsq(p, bd[:width, :width])
        return p * lax.rsqrt(ss * (1.0 / HEAD_DIM) + RMS_EPS) * gain_ref[:, c0:c0 + width]

    def store_slots(ref, y, extra, first_head=0):
        for pair in range(y.shape[1] // LANES):
            z = y[:, pair * LANES:(pair + 1) * LANES]
            e = extra[:, pair * LANES:(pair + 1) * LANES]
            even = jnp.where(lane < HEAD_DIM, z,
                             jnp.where(lane == HEAD_DIM, pltpu.roll(e, HEAD_DIM, 1), 0.0))
            odd = jnp.where(lane < HEAD_DIM, pltpu.roll(z, HEAD_DIM, 1),
                            jnp.where(lane == HEAD_DIM, e, 0.0))
            ref[0, first_head + 2 * pair] = even.astype(BF16)
            ref[0, first_head + 2 * pair + 1] = odd.astype(BF16)

    def query(c0):
        y = _rope(normed(proj(c0, 256), c0), rc, rs1, rs2)
        norm = jnp.sqrt(jnp.dot((y * y).astype(BF16), bd, preferred_element_type=F32))
        return y, -norm * kb_ref[:, c0:c0 + 256]

    def key(c0):
        return _rope(normed(proj(c0, 256), c0), rc, rs1, rs2), jnp.ones((ts, 256), F32)

    store_slots(aq1_ref, *query(0))
    store_slots(aq2_ref, *query(256))
    store_slots(ak1_ref, *key(512))
    store_slots(ak2_ref, *key(768))
    for half in range(2):
        store_slots(bq_ref, *query(C_BQ + 256 * half), first_head=4 * half)
        y = _rope(proj(C_IQ + 256 * half, 256), rc, rs1, rs2)
        for j in range(4):
            iq_ref[0, 4 * half + j] = y[:, j * HEAD_DIM:(j + 1) * HEAD_DIM].astype(BF16)

    p = proj(C_BKIK, LANES)
    y = _rope(jnp.where(lane < HEAD_DIM, normed(p, C_BKIK), p),
              rc[:, :LANES], rs1[:, :LANES], rs2[:, :LANES])
    bk_ref[0] = jnp.where(lane < HEAD_DIM, y, jnp.where(lane == HEAD_DIM, 1.0, 0.0)).astype(BF16)
    ik_ref[0] = y[:, HEAD_DIM:].astype(BF16)

    p = proj(C_LAT, LANES)
    ms = jnp.mean(p * p, axis=-1, keepdims=True)
    lat = p * lax.rsqrt(ms + RMS_EPS) * latg_ref[...]
    lat_ref[0] = jnp.concatenate([lat, ones], axis=1).astype(BF16)

    for j in range(A_HEADS):
        av_ref[0, j] = jnp.concatenate([proj(C_AV + j * A_V_DIM, A_V_DIM), ones],
                                       axis=1).astype(BF16)

    p = proj(C_IW, LANES)
    iw_ref[0] = p[:, :IDX_HEADS] * (IDX_HEADS ** -0.5 * HEAD_DIM ** -0.5)


def _rope_tables(S, width):
    half = ROT_DIM // 2
    pos = jnp.arange(S, dtype=F32)
    inv = ROPE_THETA ** (-jnp.arange(0, ROT_DIM, 2, dtype=F32) / ROT_DIM)
    ang = pos[:, None] * inv[None, :]
    cos, sin = jnp.cos(ang), jnp.sin(ang)
    zeros = jnp.zeros((S, HEAD_DIM - ROT_DIM), F32)
    c = jnp.concatenate([cos, cos, zeros + 1.0], axis=1)
    s1 = jnp.concatenate([-sin, jnp.zeros((S, half), F32), zeros], axis=1)
    s2 = jnp.concatenate([jnp.zeros((S, half), F32), sin, zeros], axis=1)
    reps = width // HEAD_DIM
    return tuple(jnp.tile(t, (1, reps)) for t in (c, s1, s2))


def _project(x, sh1, sc1, norm_g, w_in, a_q_g, a_k_g, b_q_g, b_k_g, b_kv_g):
    B, S, D = x.shape
    ts = PROJ_ROWS
    sizes = (256, 256, 256, 256, 512, 512, 64, 128, 512, 64, 8)
    offs = np.concatenate([[0], np.cumsum(sizes)])
    seg = lambda i: w_in[:, offs[i]:offs[i + 1]]
    w_p = jnp.concatenate(
        [seg(0), seg(1), seg(2), seg(3), seg(5), seg(6), seg(9), seg(8), seg(7), seg(4), seg(10),
         jnp.zeros((D, C_END - C_IW - IDX_HEADS), F32)], axis=1).astype(BF16)
    scale = HEAD_DIM ** -0.5
    gain = jnp.concatenate(
        [jnp.tile(a_q_g * scale, 2 * A_HEADS), jnp.tile(a_k_g, 2 * A_HEADS),
         jnp.tile(b_q_g * scale, B_HEADS), b_k_g, jnp.ones((HEAD_DIM,), F32)]).reshape(1, N_GAIN)
    kb_a = 8.0 * NORM_SLACK * jnp.max(jnp.abs(a_k_g))
    kb_b = 8.0 * NORM_SLACK * jnp.max(jnp.abs(b_k_g))
    zeros = lambda n: jnp.zeros((n,), F32)
    kb = jnp.concatenate([zeros(512) + kb_a, zeros(512), zeros(512) + kb_b,
                          zeros(N_GAIN - C_BKIK)]).reshape(1, N_GAIN)
    bound_a = jnp.max(jnp.abs(a_q_g)) * kb_a
    bound_b = jnp.max(jnp.abs(b_q_g)) * kb_b
    rc, rs1, rs2 = _rope_tables(S, 256)
    gid = np.arange(256) // HEAD_DIM
    bd = jnp.asarray(gid[:, None] == gid[None, :], BF16)

    row = lambda b, i: (b, 0, 0)
    full = lambda b, i: (0, 0)
    heads = lambda n, w: pl.BlockSpec((1, n, ts, w), lambda b, i: (b, 0, i, 0))
    flat = lambda w: pl.BlockSpec((1, ts, w), lambda b, i: (b, i, 0))
    hshape = lambda n, w: jax.ShapeDtypeStruct((B, n, S, w), BF16)
    outs = pl.pallas_call(
        _proj_kernel,
        grid=(B, S // ts),
        in_specs=[pl.BlockSpec((1, ts, D), lambda b, i: (b, i, 0)),
                  pl.BlockSpec((1, 1, D), row), pl.BlockSpec((1, 1, D), row),
                  pl.BlockSpec((1, D), full),
                  pl.BlockSpec((D, C_END), full),
                  pl.BlockSpec((1, N_GAIN), full),
                  pl.BlockSpec((1, N_GAIN), full),
                  pl.BlockSpec((1, LANES), full),
                  pl.BlockSpec((ts, 256), lambda b, i: (i, 0)),
                  pl.BlockSpec((ts, 256), lambda b, i: (i, 0)),
                  pl.BlockSpec((ts, 256), lambda b, i: (i, 0)),
                  pl.BlockSpec((256, 256), full)],
        out_specs=[heads(A_HEADS, LANES)] * 4
        + [heads(B_HEADS, LANES), flat(LANES), flat(HEAD_DIM), heads(IDX_HEADS, HEAD_DIM),
           flat(2 * B_LATENT), heads(A_HEADS, 2 * A_V_DIM), flat(IDX_HEADS)],
        out_shape=[hshape(A_HEADS, LANES)] * 4
        + [hshape(B_HEADS, LANES), jax.ShapeDtypeStruct((B, S, LANES), BF16),
           jax.ShapeDtypeStruct((B, S, HEAD_DIM), BF16), hshape(IDX_HEADS, HEAD_DIM),
           jax.ShapeDtypeStruct((B, S, 2 * B_LATENT), BF16), hshape(A_HEADS, 2 * A_V_DIM),
           jax.ShapeDtypeStruct((B, S, IDX_HEADS), F32)],
        compiler_params=_cparams(2),
    )(x, sh1, sc1, norm_g.reshape(1, D), w_p, gain, kb, b_kv_g.reshape(1, B_LATENT), rc, rs1, rs2,
      bd)
    plain_a = (bound_a <= PLAIN_EXP_MAX_BOUND).astype(I32).reshape(1)
    plain_b = (bound_b <= PLAIN_EXP_MAX_BOUND).astype(I32).reshape(1)
    return outs, plain_a, plain_b


def _softmax_init(plain, m_ref, acc_ref):
    acc_ref[...] = jnp.zeros(acc_ref.shape, F32)
    if not plain:
        m_ref[...] = jnp.full(m_ref.shape, NEG_BIG, F32)


def _softmax_step(plain, s, v1, m_ref, acc_ref):
    if plain:
        acc_ref[...] += jnp.dot(jnp.exp(s).astype(BF16), v1, preferred_element_type=F32)
        return
    m_old = m_ref[...]
    m_new = jnp.maximum(m_old, jnp.max(s, axis=-1, keepdims=True))
    p = jnp.exp(s - m_new).astype(BF16)
    acc_ref[...] = (jnp.exp(m_old - m_new) * acc_ref[...]
                    + jnp.dot(p, v1, preferred_element_type=F32))
    m_ref[...] = m_new


def _softmax_result(acc_ref, width):
    acc = acc_ref[...]
    return acc[:, :width] / acc[:, width:width + 1]


def _chunk_mask(q0, k0, tq, tk):
    qc = (q0 + lax.broadcasted_iota(I32, (tq, tk), 0)) // CHUNK
    kc = (k0 + lax.broadcasted_iota(I32, (tq, tk), 1)) // CHUNK
    return kc <= qc


def _diff_attn_kernel(plain_ref, lam_ref, subg_ref, q1_ref, q2_ref, k1_ref, k2_ref, v_ref, o_ref,
                      m1_ref, acc1_ref, m2_ref, acc2_ref, *, lambda_init):
    tq, tk = ATT_TQ, ATT_TK
    i = pl.program_id(2)
    n_tiles = ((i + 1) * tq + tk - 1) // tk

    def attend(plain):
        q1 = q1_ref[0, 0]
        q2 = q2_ref[0, 0]
        _softmax_init(plain, m1_ref, acc1_ref)
        _softmax_init(plain, m2_ref, acc2_ref)

        def tile(j, masked):
            ks = pl.multiple_of(j * tk, tk)
            v1 = v_ref[0, 0, pl.ds(ks, tk), :]
            s1 = _dot_nt(q1, k1_ref[0, 0, pl.ds(ks, tk), :])
            s2 = _dot_nt(q2, k2_ref[0, 0, pl.ds(ks, tk), :])
            if masked:
                ok = _chunk_mask(i * tq, ks, tq, tk)
                s1 = jnp.where(ok, s1, NEG_BIG)
                s2 = jnp.where(ok, s2, NEG_BIG)
            _softmax_step(plain, s1, v1, m1_ref, acc1_ref)
            _softmax_step(plain, s2, v1, m2_ref, acc2_ref)

        def body(j, c):
            tile(j, False)
            return c

        lax.fori_loop(0, n_tiles - 1, body, 0)
        tile(n_tiles - 1, True)

    pl.when(plain_ref[0] == 1)(lambda: attend(True))
    pl.when(plain_ref[0] != 1)(lambda: attend(False))

    lv = lam_ref[...]
    lam = (jnp.exp(jnp.sum(lv[0:1] * lv[1:2], axis=-1, keepdims=True))
           - jnp.exp(jnp.sum(lv[2:3] * lv[3:4], axis=-1, keepdims=True)) + lambda_init)
    o = _softmax_result(acc1_ref, A_V_DIM) - lam * _softmax_result(acc2_ref, A_V_DIM)
    ms = jnp.mean(o * o, axis=-1, keepdims=True)
    o = o * lax.rsqrt(ms + RMS_EPS) * subg_ref[...] * (1.0 - lambda_init)
    o_ref[0] = o.astype(BF16)


def _diff_attention(plain, q1, q2, k1, k2, v1, a_lambda, sub_g, lambda_init):
    B, H, S, dq = q1.shape
    tq = ATT_TQ
    qspec = pl.BlockSpec((1, 1, tq, dq), lambda b, h, i, p: (b, h, i, 0))
    kspec = pl.BlockSpec((1, 1, S, dq), lambda b, h, i, p: (b, h, 0, 0))
    col = lambda: pltpu.VMEM((tq, 1), F32)
    acc = lambda: pltpu.VMEM((tq, 2 * A_V_DIM), F32)
    grid_spec = pltpu.PrefetchScalarGridSpec(
        num_scalar_prefetch=1,
        grid=(B, H, S // tq),
        in_specs=[pl.BlockSpec((4, HEAD_DIM), lambda b, h, i, p: (0, 0)),
                  pl.BlockSpec((1, A_V_DIM), lambda b, h, i, p: (0, 0)),
                  qspec, qspec, kspec, kspec,
                  pl.BlockSpec((1, 1, S, 2 * A_V_DIM), lambda b, h, i, p: (b, h, 0, 0))],
        out_specs=pl.BlockSpec((1, tq, A_V_DIM), lambda b, h, i, p: (b, i, h)),
        scratch_shapes=[col(), acc(), col(), acc()],
    )
    return pl.pallas_call(
        functools.partial(_diff_attn_kernel, lambda_init=lambda_init),
        grid_spec=grid_spec,
        out_shape=jax.ShapeDtypeStruct((B, S, H * A_V_DIM), BF16),
        compiler_params=_cparams(3),
    )(plain, a_lambda, sub_g.reshape(1, A_V_DIM), q1, q2, k1, k2, v1)


def _dsa_kernel(plain_ref, iq_ref, iw_ref, ik_ref, q_ref, k_ref, lat_ref, wuv_ref, tri_ref, o_ref,
                score_ref, thr_ref, m_ref, acc_ref, *, topk):
    tq, tk = DSA_TQ, DSA_TK
    nh = B_HEADS
    i = pl.program_id(1)
    n_tiles = ((i + 1) * tq + tk - 1) // tk

    iq = iq_ref[0].reshape(IDX_HEADS * tq, HEAD_DIM)
    iw = iw_ref[0]

    def score_tile(j, masked):
        ks = pl.multiple_of(j * tk, tk)
        rel = jnp.maximum(_dot_nt(iq, ik_ref[0, pl.ds(ks, tk), :]), 0.0)
        rel = rel.reshape(IDX_HEADS, tq, tk)
        score = rel[0] * iw[:, 0:1]
        for h in range(1, IDX_HEADS):
            score = score + rel[h] * iw[:, h:h + 1]
        if masked:
            score = jnp.where(_chunk_mask(i * tq, ks, tq, tk), score, -jnp.inf)
        score_ref[:, pl.ds(ks, tk)] = score

    def score_body(j, c):
        score_tile(j, False)
        return c

    lax.fori_loop(0, n_tiles - 1, score_body, 0)
    score_tile(n_tiles - 1, True)

    def image_to_float(t):
        return pltpu.bitcast(jnp.where(t < 0, t ^ 0x7FFFFFFF, t), F32)

    def count(pred, cand):
        parts = []
        for r0 in range(0, tq, COUNT_ROWS):
            rows = pl.ds(r0, COUNT_ROWS)
            cand_r = cand[r0:r0 + COUNT_ROWS]

            def body(j, acc, rows=rows, cand_r=cand_r):
                ks = pl.multiple_of(j * tk, tk)
                hit = jnp.where(pred(score_ref[rows, pl.ds(ks, tk)], cand_r), 1.0, 0.0)
                for c in range(tk // LANES):
                    acc = acc + hit[:, c * LANES:(c + 1) * LANES]
                return acc
            parts.append(lax.fori_loop(0, n_tiles, body, jnp.zeros((COUNT_ROWS, LANES), F32)))
        acc = jnp.concatenate(parts, axis=0)
        return jnp.sum(acc, axis=-1, keepdims=True)

    ge = lambda s, c: s >= c
    gt = lambda s, c: s > c

    def all_of(flags):
        return (jnp.min(flags) > 0.0).astype(I32)

    zero = jnp.zeros((tq, 1), F32)
    n_ge0 = count(ge, zero)
    tie0 = jnp.where(jnp.logical_and(count(gt, zero) < topk, n_ge0 > topk), 1.0, 0.0)
    t0 = jnp.where(n_ge0 >= topk, 0, INT_MIN)
    settled0 = jnp.where(n_ge0 == topk, 1.0, tie0)

    def bit_body(state):
        step, t, settled, _ = state
        bit = jnp.left_shift(jnp.int32(1), 31 - step)
        cand = t + bit
        n_ge = count(ge, image_to_float(cand))
        t = jnp.where(settled > 0.0, t, jnp.where(n_ge >= topk, cand, t))
        settled = jnp.where(n_ge == topk, 1.0, settled)
        return step + 1, t, settled, all_of(settled)

    def bit_cond(state):
        step, _, _, all_settled = state
        return jnp.logical_and(step < 32, all_settled == 0)

    _, t, _, all_settled = lax.while_loop(
        bit_cond, bit_body, (jnp.int32(1), t0, settled0, all_of(settled0)))
    thr_ref[...] = jnp.where(t == INT_MIN, jnp.finfo(F32).min, image_to_float(t))

    def min_where(pred, cand):
        def body(j, acc):
            ks = pl.multiple_of(j * tk, tk)
            s = score_ref[:, pl.ds(ks, tk)]
            s = jnp.where(pred(s, cand), s, jnp.inf)
            for c in range(tk // LANES):
                acc = jnp.minimum(acc, s[:, c * LANES:(c + 1) * LANES])
            return acc
        acc = lax.fori_loop(0, n_tiles, body, jnp.full((tq, LANES), jnp.inf, F32))
        return jnp.min(acc, axis=-1, keepdims=True)

    @pl.when(all_settled == 0)
    def _():
        thr0 = thr_ref[...]
        low = min_where(ge, thr0)
        nxt = min_where(gt, low)
        thr_ref[...] = jnp.where(count(ge, nxt) >= topk, nxt,
                                 jnp.where(low < jnp.inf, low, thr0))

    @pl.when(jnp.logical_or(all_settled == 0, jnp.max(tie0) > 0.0))
    def _():
        thr = thr_ref[...]
        quota = topk - count(gt, thr)

        def tie_body(j, carry):
            sl = pl.ds(pl.multiple_of(j * tk, tk), tk)
            score = score_ref[:, sl]
            eq = score == thr
            prefix = carry + jnp.dot(jnp.where(eq, 1.0, 0.0).astype(BF16), tri_ref[...],
                                     preferred_element_type=F32)
            score_ref[:, sl] = jnp.where(eq & (prefix > quota), -jnp.inf, score)
            return prefix[:, tk - 1:tk]

        lax.fori_loop(0, n_tiles, tie_body, jnp.zeros((tq, 1), F32))

    thr = thr_ref[...]

    def attend(plain):
        q = q_ref[0].reshape(nh * tq, LANES)
        _softmax_init(plain, m_ref, acc_ref)

        def attn_body(j, c):
            ks = pl.multiple_of(j * tk, tk)
            s = _dot_nt(q, k_ref[0, pl.ds(ks, tk), :]).reshape(nh, tq, tk)
            sel = score_ref[:, pl.ds(ks, tk)] >= thr
            s = jnp.where(sel[None], s, NEG_BIG).reshape(nh * tq, tk)
            _softmax_step(plain, s, lat_ref[0, pl.ds(ks, tk), :], m_ref, acc_ref)
            return c

        lax.fori_loop(0, n_tiles, attn_body, 0)

    pl.when(plain_ref[0] == 1)(lambda: attend(True))
    pl.when(plain_ref[0] != 1)(lambda: attend(False))

    o = _softmax_result(acc_ref, B_LATENT).astype(BF16)
    for h in range(nh):
        oh = jnp.dot(o[h * tq:(h + 1) * tq], wuv_ref[h], preferred_element_type=F32)
        o_ref[0, :, h * HEAD_DIM:(h + 1) * HEAD_DIM] = oh.astype(BF16)


def _dsa_attention(plain, iq, iw, ik, q, k, lat1, w_uv):
    B, nh, S, dq = q.shape
    dh = HEAD_DIM
    tq = DSA_TQ
    topk = min(IDX_TOPK_MAX, S // 4)
    col = np.arange(DSA_TK)
    tri = jnp.asarray(col[:, None] <= col[None, :], BF16)
    hspec = lambda w: pl.BlockSpec((1, nh, tq, w), lambda b, i, p: (b, 0, i, 0))
    kspec = lambda w: pl.BlockSpec((1, S, w), lambda b, i, p: (b, 0, 0))
    grid_spec = pltpu.PrefetchScalarGridSpec(
        num_scalar_prefetch=1,
        grid=(B, S // tq),
        in_specs=[hspec(dh),
                  pl.BlockSpec((1, tq, IDX_HEADS), lambda b, i, p: (b, i, 0)),
                  kspec(dh), hspec(dq), kspec(dq), kspec(2 * B_LATENT),
                  pl.BlockSpec((nh, B_LATENT, dh), lambda b, i, p: (0, 0, 0)),
                  pl.BlockSpec((DSA_TK, DSA_TK), lambda b, i, p: (0, 0))],
        out_specs=pl.BlockSpec((1, tq, nh * dh), lambda b, i, p: (b, i, 0)),
        scratch_shapes=[pltpu.VMEM((tq, S), F32),
                        pltpu.VMEM((tq, 1), F32),
                        pltpu.VMEM((nh * tq, 1), F32),
                        pltpu.VMEM((nh * tq, 2 * B_LATENT), F32)],
    )
    return pl.pallas_call(
        functools.partial(_dsa_kernel, topk=topk),
        grid_spec=grid_spec,
        out_shape=jax.ShapeDtypeStruct((B, S, nh * dh), BF16),
        compiler_params=_cparams(2),
    )(plain, iq, iw, ik, q, k, lat1, w_uv.astype(BF16), tri)


def _out_kernel(a_ref, b_ref, x_ref, g1_ref, sh_ref, sc_ref, ng_ref, woa_ref, wob_ref, rw_ref,
                rb_ref, tri_ref, x1_ref, h2_ref, idx_ref, gate_ref, rank_ref, cnt_ref, carry_ref):
    first = jnp.logical_and(pl.program_id(0) == 0, pl.program_id(1) == 0)

    @pl.when(first)
    def _():
        carry_ref[...] = jnp.zeros(carry_ref.shape, F32)

    mix = (jnp.dot(a_ref[0], woa_ref[...], preferred_element_type=F32)
           + jnp.dot(b_ref[0], wob_ref[...], preferred_element_type=F32))
    x1 = x_ref[0] + g1_ref[0] * mix
    x1_ref[0] = x1
    ms = jnp.mean(x1 * x1, axis=-1, keepdims=True)
    h2 = x1 * lax.rsqrt(ms + RMS_EPS) * ng_ref[...]
    h2 = h2 * (1.0 + sc_ref[0]) + sh_ref[0]
    h2_ref[0] = h2

    logits = jnp.dot(h2, rw_ref[...], preferred_element_type=F32,
                     precision=lax.Precision.HIGHEST) + rb_ref[...]
    ts = logits.shape[0]
    lane_i = lax.broadcasted_iota(I32, (ts, LANES), 1)
    lane = lane_i.astype(F32)
    neg_inf = jnp.float32(-jnp.inf)
    l = jnp.where(lane_i < N_EXPERTS, logits, neg_inf)
    vals, idxs = [], []
    for _ in range(TOP_K):
        m = jnp.max(l, axis=-1, keepdims=True)
        idx = jnp.min(jnp.where(l == m, lane, float(LANES)), axis=-1, keepdims=True)
        vals.append(m)
        idxs.append(idx)
        l = jnp.where(lane == idx, neg_inf, l)
    es = [jnp.exp(v - vals[0]) for v in vals]
    denom = es[0] + es[1] + es[2] + es[3]

    onehot = jnp.zeros((ts, LANES), F32)
    for idx in idxs:
        onehot = onehot + jnp.where(lane == idx, 1.0, 0.0)
    prefix = jnp.dot(tri_ref[...], onehot.astype(BF16), preferred_element_type=F32) + carry_ref[...]
    idx_out = jnp.zeros((ts, LANES), I32)
    gate_out = jnp.zeros((ts, LANES), F32)
    rank_out = jnp.zeros((ts, LANES), I32)
    for k in range(TOP_K):
        rank = jnp.sum(jnp.where(lane == idxs[k], prefix, 0.0), axis=-1, keepdims=True)
        idx_out = jnp.where(lane_i == k, idxs[k].astype(I32), idx_out)
        gate_out = jnp.where(lane_i == k, es[k] / denom, gate_out)
        rank_out = jnp.where(lane_i == k, rank.astype(I32), rank_out)
    idx_ref[0] = idx_out
    gate_ref[0] = gate_out
    rank_ref[0] = rank_out
    carry = carry_ref[...] + jnp.sum(onehot, axis=0, keepdims=True)
    carry_ref[...] = carry
    cnt_ref[...] = carry


def _out_and_route(a_out, b_out, x, g1, sh2, sc2, norm_g, w_out, router_w, router_b):
    B, S, D = x.shape
    ts = OUT_ROWS
    aw = a_out.shape[-1]
    w_bf = w_out.astype(BF16)
    rw = jnp.concatenate([router_w, jnp.zeros((D, LANES - N_EXPERTS), F32)], axis=1)
    rb = jnp.concatenate([router_b, jnp.zeros((LANES - N_EXPERTS,), F32)]).reshape(1, LANES)
    r = np.arange(ts)
    tri = jnp.asarray(r[:, None] > r[None, :], BF16)
    row = lambda b, i: (b, 0, 0)
    full = lambda b, i: (0, 0)
    tok = lambda w: pl.BlockSpec((1, ts, w), lambda b, i: (b, i, 0))
    return pl.pallas_call(
        _out_kernel,
        grid=(B, S // ts),
        in_specs=[tok(aw), tok(D - aw), tok(D),
                  pl.BlockSpec((1, 1, D), row), pl.BlockSpec((1, 1, D), row),
                  pl.BlockSpec((1, 1, D), row), pl.BlockSpec((1, D), full),
                  pl.BlockSpec((aw, D), full), pl.BlockSpec((D - aw, D), full),
                  pl.BlockSpec((D, LANES), full), pl.BlockSpec((1, LANES), full),
                  pl.BlockSpec((ts, ts), full)],
        out_specs=[tok(D), tok(D), tok(LANES), tok(LANES), tok(LANES),
                   pl.BlockSpec((1, LANES), full)],
        out_shape=[jax.ShapeDtypeStruct((B, S, D), F32), jax.ShapeDtypeStruct((B, S, D), F32),
                   jax.ShapeDtypeStruct((B, S, LANES), I32),
                   jax.ShapeDtypeStruct((B, S, LANES), F32),
                   jax.ShapeDtypeStruct((B, S, LANES), I32),
                   jax.ShapeDtypeStruct((1, LANES), F32)],
        scratch_shapes=[pltpu.VMEM((1, LANES), F32)],
        compiler_params=_cparams(2),
    )(a_out, b_out, x, g1, sh2, sc2, norm_g.reshape(1, D), w_bf[:aw], w_bf[aw:], rw, rb, tri)


def _row_copy_wait(src_ref, dst_ref, sem, rows):
    pltpu.make_async_copy(src_ref.at[pl.ds(0, rows), :], dst_ref.at[pl.ds(0, rows), :], sem).wait()


def _moe_kernel(be_ref, nused_ref, tok0_ref, tokn_ref, h_ref, wg_ref, bg_ref, wu_ref, bu_ref,
                wd_ref, bd_ref, y_ref, x_even, x_odd, wg_s, wu_s, wd_s, sem):
    i = pl.program_id(0)
    n_used = nused_ref[0]
    used = i < n_used
    even = lax.rem(i, 2) == 0
    rows = x_even.shape[0]
    new_expert = jnp.logical_or(i == 0, be_ref[i] != be_ref[jnp.maximum(i - 1, 0)])
    bufs = ((x_even, sem.at[0]), (x_odd, sem.at[1]))

    def row_gather(tok_ref, r, buf):
        x_ref, s = buf
        return pltpu.make_async_copy(h_ref.at[pl.ds(tok_ref[0, 0, r], 1), :],
                                     x_ref.at[pl.ds(r, 1), :], s)

    def wait_block(buf):
        x_ref, s = buf
        pltpu.make_async_copy(h_ref.at[pl.ds(0, rows), :], x_ref, s).wait()

    @pl.when(i == 0)
    def _():
        def body(r, c):
            row_gather(tok0_ref, r, bufs[0]).start()
            return c
        lax.fori_loop(0, rows, body, 0)

    @pl.when(jnp.logical_and(used, new_expert))
    def _():
        wg_s[...] = wg_ref[0].astype(BF16)
        wu_s[...] = wu_ref[0].astype(BF16)
        wd_s[...] = wd_ref[0].astype(BF16)

    def expert_block(cur, nxt):
        wait_block(cur)
        for r in range(rows):
            row_gather(tokn_ref, r, nxt).start()
        xb = cur[0][...].astype(BF16)
        g = jnp.dot(xb, wg_s[...], preferred_element_type=F32) + bg_ref[0]
        u = jnp.dot(xb, wu_s[...], preferred_element_type=F32) + bu_ref[0]
        g = jnp.minimum(g, SWIGLU_LIMIT)
        u = jnp.clip(u, -SWIGLU_LIMIT, SWIGLU_LIMIT)
        a = g * (1.0 / (1.0 + jnp.exp(-SWIGLU_ALPHA * g))) * (u + 1.0)
        y_ref[...] = jnp.dot(a.astype(BF16), wd_s[...], preferred_element_type=F32) + bd_ref[0]

    pl.when(jnp.logical_and(used, even))(lambda: expert_block(bufs[0], bufs[1]))
    pl.when(jnp.logical_and(used, jnp.logical_not(even)))(lambda: expert_block(bufs[1], bufs[0]))

    pl.when(jnp.logical_and(i == n_used, even))(lambda: wait_block(bufs[0]))
    pl.when(jnp.logical_and(i == n_used, jnp.logical_not(even)))(lambda: wait_block(bufs[1]))

    @pl.when(jnp.logical_not(used))
    def _():
        y_ref[...] = jnp.zeros(y_ref.shape, F32)


def _moe_experts(h2, tok_of_row, block_expert, n_used, w_gate, b_gate, w_up, b_up, w_down,
                 b_down):
    T, D = h2.shape
    E, _, F = w_gate.shape
    nb, _, G = tok_of_row.shape
    last = lambda i, nu: jnp.maximum(jnp.minimum(i, nu[0] - 1), 0)
    wsel = lambda i, be, nu: (be[last(i, nu)], 0, 0)
    smem = lambda index_map: pl.BlockSpec((1, 1, G), index_map, memory_space=pltpu.SMEM)
    grid_spec = pltpu.PrefetchScalarGridSpec(
        num_scalar_prefetch=2,
        grid=(nb + 1,),
        in_specs=[smem(lambda i, be, nu: (0, 0, 0)),
                  smem(lambda i, be, nu: (last(i + 1, nu), 0, 0)),
                  pl.BlockSpec(memory_space=pl.ANY),
                  pl.BlockSpec((1, D, F), wsel), pl.BlockSpec((1, 1, F), wsel),
                  pl.BlockSpec((1, D, F), wsel), pl.BlockSpec((1, 1, F), wsel),
                  pl.BlockSpec((1, F, D), wsel), pl.BlockSpec((1, 1, D), wsel)],
        out_specs=pl.BlockSpec((G, D), lambda i, be, nu: (i, 0)),
        scratch_shapes=[pltpu.VMEM((G, D), F32), pltpu.VMEM((G, D), F32),
                        pltpu.VMEM((D, F), BF16), pltpu.VMEM((D, F), BF16),
                        pltpu.VMEM((F, D), BF16), pltpu.SemaphoreType.DMA((2,))],
    )
    return pl.pallas_call(
        _moe_kernel,
        grid_spec=grid_spec,
        out_shape=jax.ShapeDtypeStruct(((nb + 1) * G, D), F32),
        compiler_params=_cparams(1),
    )(block_expert, n_used, tok_of_row, tok_of_row, h2, w_gate, b_gate.reshape(E, 1, F),
      w_up, b_up.reshape(E, 1, F), w_down, b_down.reshape(E, 1, D))


def _combine_kernel(dest_ref, gate_ref, x1_ref, g2_ref, ybuf_ref, o_ref, buf_ref, sem):
    rows = x1_ref.shape[0]

    def body(r, c):
        for k in range(TOP_K):
            d = dest_ref[0, 0, r * TOP_K + k]
            pltpu.make_async_copy(ybuf_ref.at[pl.ds(d, 1), :], buf_ref.at[k, pl.ds(r, 1), :],
                                  sem).start()
        return c

    lax.fori_loop(0, rows, body, 0)
    for k in range(TOP_K):
        _row_copy_wait(ybuf_ref, buf_ref.at[k], sem, rows)
    gate = gate_ref[...]
    y = buf_ref[0] * gate[:, 0:1]
    for k in range(1, TOP_K):
        y = y + buf_ref[k] * gate[:, k:k + 1]
    o_ref[...] = x1_ref[...] + g2_ref[0] * y


def _combine(ybuf, dest, gates, x1, g2, seq):
    T, D = x1.shape
    ts = COMBINE_ROWS
    per_seq = seq // ts
    return pl.pallas_call(
        _combine_kernel,
        grid=(T // ts,),
        in_specs=[pl.BlockSpec((1, 1, ts * TOP_K), lambda i: (i, 0, 0), memory_space=pltpu.SMEM),
                  pl.BlockSpec((ts, LANES), lambda i: (i, 0)),
                  pl.BlockSpec((ts, D), lambda i: (i, 0)),
                  pl.BlockSpec((1, 1, D), lambda i: (i // per_seq, 0, 0)),
                  pl.BlockSpec(memory_space=pl.ANY)],
        out_specs=pl.BlockSpec((ts, D), lambda i: (i, 0)),
        out_shape=jax.ShapeDtypeStruct((T, D), F32),
        scratch_shapes=[pltpu.VMEM((TOP_K, ts, D), F32), pltpu.SemaphoreType.DMA(())],
        compiler_params=_cparams(1),
    )(dest.reshape(T // ts, 1, ts * TOP_K), gates, x1, g2, ybuf)


def _layer(x, mod, lambda_init, norm1_g, norm2_g, w_in, w_out, a_q_norm_g, a_k_norm_g, a_lambda,
           a_sub_g, b_q_norm_g, b_k_norm_g, b_kv_norm_g, b_w_uv, router_w, router_b, w_gate,
           b_gate, w_up, b_up, w_down, b_down):
    B, S, D = x.shape
    T = B * S
    sh1, sc1, g1, sh2, sc2, g2 = [m.reshape(B, 1, D) for m in jnp.split(mod, 6, axis=-1)]

    (aq1, aq2, ak1, ak2, bq, bk, ik, iq, blat1, av1, iw), plain_a, plain_b = _project(
        x, sh1, sc1, norm1_g, w_in, a_q_norm_g, a_k_norm_g, b_q_norm_g, b_k_norm_g, b_kv_norm_g)
    a_out = _diff_attention(plain_a, aq1, aq2, ak1, ak2, av1, a_lambda, a_sub_g, lambda_init)
    b_out = _dsa_attention(plain_b, iq, iw, ik, bq, bk, blat1, b_w_uv)

    x1, h2, top_idx, gates, rank, counts = _out_and_route(
        a_out, b_out, x, g1, sh2, sc2, norm2_g, w_out, router_w, router_b)

    G = MOE_ROWS
    counts = counts[0, :N_EXPERTS].astype(I32)
    padded = ((counts + G - 1) // G) * G
    pcum = jnp.cumsum(padded)
    poff = pcum - padded
    nb = (T * TOP_K) // G + N_EXPERTS
    starts = jnp.arange(nb + 1, dtype=I32) * G
    block_expert = jnp.minimum(
        jnp.sum((pcum[None, :] <= starts[:, None]).astype(I32), axis=1), N_EXPERTS - 1)
    n_used = (pcum[-1:] // G).astype(I32)
    top_idx = top_idx.reshape(T, LANES)[:, :TOP_K]
    dest = (poff[top_idx] + rank.reshape(T, LANES)[:, :TOP_K]).astype(I32).reshape(T * TOP_K)
    tok_of_row = jnp.zeros((nb * G,), I32).at[dest].set(
        jnp.arange(T * TOP_K, dtype=I32) // TOP_K, unique_indices=True)

    ybuf = _moe_experts(h2.reshape(T, D), tok_of_row.reshape(nb, 1, G), block_expert.astype(I32),
                        n_used, w_gate, b_gate, w_up, b_up, w_down, b_down)
    out = _combine(ybuf, dest, gates.reshape(T, LANES), x1.reshape(T, D), g2, S)
    return out.reshape(B, S, D)


def kernel(x, c, norm1_g, norm2_g, w_ada, b_ada, w_in, w_out, a_q_norm_g, a_k_norm_g, a_lambda,
           a_sub_g, b_q_norm_g, b_k_norm_g, b_kv_norm_g, b_w_uv, router_w, router_b, w_gate,
           b_gate, w_up, b_up, w_down, b_down):
    depth = w_in.shape[0]
    for l in range(depth):
        lambda_init = 0.8 - 0.6 * math.exp(-0.3 * l)
        mod = _ada(c, w_ada[l], b_ada[l])
        x = _layer(x, mod, lambda_init, norm1_g[l], norm2_g[l], w_in[l], w_out[l], a_q_norm_g[l],
                   a_k_norm_g[l], a_lambda[l], a_sub_g[l], b_q_norm_g[l], b_k_norm_g[l],
                   b_kv_norm_g[l], b_w_uv[l], router_w[l], router_b[l], w_gate[l], b_gate[l],
                   w_up[l], b_up[l], w_down[l], b_down[l])
    return x
```

```python
import functools
import math

import numpy as np
import jax
import jax.numpy as jnp
from jax import lax
from jax.experimental import pallas as pl
from jax.experimental.pallas import tpu as pltpu

F32 = jnp.float32
BF16 = jnp.bfloat16
I32 = jnp.int32

CHUNK = 64
HEAD_DIM = 64
ROT_DIM = HEAD_DIM // 4
ROPE_THETA = 500000.0
RMS_EPS = 1e-6
A_HEADS = 4
A_V_DIM = 2 * HEAD_DIM
B_HEADS = 8
B_LATENT = 128
IDX_HEADS = 8
IDX_TOPK_MAX = 256
N_EXPERTS = 32
TOP_K = 4
SWIGLU_LIMIT = 7.0
SWIGLU_ALPHA = 1.702

LANES = 128
INT_MIN = -(2 ** 31)
NEG_BIG = -1e30
VMEM_LIMIT = 56 * 1024 * 1024

NORM_SLACK = 1.01
PLAIN_EXP_MAX_BOUND = 30.0

PROJ_ROWS = 512
ATT_TQ = 512
ATT_TK = 512
DSA_TQ = 256
DSA_TK = 512
COUNT_ROWS = 128
COUNT_CHAINS = 8
WALK_UNROLL = 4
OUT_ROWS = 512
MOE_ROWS = 512
DMA_QUEUES = 2
DISPATCH_ROWS = 1024
COMBINE_ROWS = 512


def _cparams(n_axes):
    return pltpu.CompilerParams(
        dimension_semantics=("arbitrary",) * n_axes, vmem_limit_bytes=VMEM_LIMIT)


def _dot_nt(a, b):
    return lax.dot_general(a, b, (((1,), (1,)), ((), ())), preferred_element_type=F32)


def _ada_kernel(c_ref, w_ref, b_ref, o_ref):
    c = c_ref[...]
    sc = c / (1.0 + jnp.exp(-c))
    o_ref[...] = jnp.dot(sc, w_ref[...], preferred_element_type=F32,
                         precision=lax.Precision.HIGHEST) + b_ref[...]


def _ada(c, w, b):
    B, D = c.shape
    N = w.shape[1]
    return pl.pallas_call(
        _ada_kernel,
        grid=(N // D,),
        in_specs=[pl.BlockSpec((B, D), lambda j: (0, 0)),
                  pl.BlockSpec((D, D), lambda j: (0, j)),
                  pl.BlockSpec((1, D), lambda j: (0, j))],
        out_specs=pl.BlockSpec((B, D), lambda j: (0, j)),
        out_shape=jax.ShapeDtypeStruct((B, N), F32),
        compiler_params=_cparams(1),
    )(c, w, b.reshape(1, N))


C_QK = 0
C_BQ = 1024
C_BKIK = 1536
C_IQ = 1664
C_LAT = 2176
C_AV = 2304
C_IW = 2816
C_END = 2944
N_GAIN = C_IQ


def _group_sumsq(p, bd):
    sq = p * p
    hi = sq.astype(BF16)
    lo = (sq - hi.astype(F32)).astype(BF16)
    return (jnp.dot(hi, bd, preferred_element_type=F32)
            + jnp.dot(lo, bd, preferred_element_type=F32))


def _rope(y, c, s1, s2):
    w = y.shape[1]
    return y * c + pltpu.roll(y, w - ROT_DIM // 2, 1) * s1 + pltpu.roll(y, ROT_DIM // 2, 1) * s2


def _proj_kernel(x_ref, sh_ref, sc_ref, g_ref, w_ref, gain_ref, kb_ref, latg_ref, rc_ref, rs1_ref,
                 rs2_ref, bd_ref,
                 aq1_ref, aq2_ref, ak1_ref, ak2_ref, bq_ref, bk_ref, ik_ref, iq_ref, lat_ref,
                 av_ref, iw_ref):
    x = x_ref[0]
    ms = jnp.mean(x * x, axis=-1, keepdims=True)
    h = x * lax.rsqrt(ms + RMS_EPS) * g_ref[...]
    h = (h * (1.0 + sc_ref[0]) + sh_ref[0]).astype(BF16)

    rc, rs1, rs2 = rc_ref[...], rs1_ref[...], rs2_ref[...]
    bd = bd_ref[...]
    ts = x.shape[0]
    lane = lax.broadcasted_iota(I32, (ts, LANES), 1)
    ones = jnp.ones((ts, LANES), F32)

    def proj(c0, width):
        return jnp.dot(h, w_ref[:, c0:c0 + width], preferred_element_type=F32)

    def normed(p, c0):
        width = p.shape[1]
        ss = _group_sumsq(p, bd[:width, :width])
        return p * lax.rsqrt(ss * (1.0 / HEAD_DIM) + RMS_EPS) * gain_ref[:, c0:c0 + width]

    def store_slots(ref, y, extra, first_head=0):
        for pair in range(y.shape[1] // LANES):
            z = y[:, pair * LANES:(pair + 1) * LANES]
            e = extra[:, pair * LANES:(pair + 1) * LANES]
            even = jnp.where(lane < HEAD_DIM, z,
                             jnp.where(lane == HEAD_DIM, pltpu.roll(e, HEAD_DIM, 1), 0.0))
            odd = jnp.where(lane < HEAD_DIM, pltpu.roll(z, HEAD_DIM, 1),
                            jnp.where(lane == HEAD_DIM, e, 0.0))
            ref[0, first_head + 2 * pair] = even.astype(BF16)
            ref[0, first_head + 2 * pair + 1] = odd.astype(BF16)

    def query(c0):
        y = _rope(normed(proj(c0, 256), c0), rc, rs1, rs2)
        norm = jnp.sqrt(jnp.dot((y * y).astype(BF16), bd, preferred_element_type=F32))
        return y, -norm * kb_ref[:, c0:c0 + 256]

    def key(c0):
        return _rope(normed(proj(c0, 256), c0), rc, rs1, rs2), jnp.ones((ts, 256), F32)

    store_slots(aq1_ref, *query(0))
    store_slots(aq2_ref, *query(256))
    store_slots(ak1_ref, *key(512))
    store_slots(ak2_ref, *key(768))
    for half in range(2):
        store_slots(bq_ref, *query(C_BQ + 256 * half), first_head=4 * half)
        y = _rope(proj(C_IQ + 256 * half, 256), rc, rs1, rs2)
        for j in range(4):
            iq_ref[0, 4 * half + j] = y[:, j * HEAD_DIM:(j + 1) * HEAD_DIM].astype(BF16)

    p = proj(C_BKIK, LANES)
    y = _rope(jnp.where(lane < HEAD_DIM, normed(p, C_BKIK), p),
              rc[:, :LANES], rs1[:, :LANES], rs2[:, :LANES])
    bk_ref[0] = jnp.where(lane < HEAD_DIM, y, jnp.where(lane == HEAD_DIM, 1.0, 0.0)).astype(BF16)
    ik_ref[0] = y[:, HEAD_DIM:].astype(BF16)

    p = proj(C_LAT, LANES)
    ms = jnp.mean(p * p, axis=-1, keepdims=True)
    lat = p * lax.rsqrt(ms + RMS_EPS) * latg_ref[...]
    lat_ref[0] = jnp.concatenate([lat, ones], axis=1).astype(BF16)

    for j in range(A_HEADS):
        av_ref[0, j] = jnp.concatenate([proj(C_AV + j * A_V_DIM, A_V_DIM), ones],
                                       axis=1).astype(BF16)

    p = proj(C_IW, LANES)
    iw_ref[0] = p[:, :IDX_HEADS] * (IDX_HEADS ** -0.5 * HEAD_DIM ** -0.5)


def _rope_tables(S, width):
    half = ROT_DIM // 2
    pos = jnp.arange(S, dtype=F32)
    inv = ROPE_THETA ** (-jnp.arange(0, ROT_DIM, 2, dtype=F32) / ROT_DIM)
    ang = pos[:, None] * inv[None, :]
    cos, sin = jnp.cos(ang), jnp.sin(ang)
    zeros = jnp.zeros((S, HEAD_DIM - ROT_DIM), F32)
    c = jnp.concatenate([cos, cos, zeros + 1.0], axis=1)
    s1 = jnp.concatenate([-sin, jnp.zeros((S, half), F32), zeros], axis=1)
    s2 = jnp.concatenate([jnp.zeros((S, half), F32), sin, zeros], axis=1)
    reps = width // HEAD_DIM
    return tuple(jnp.tile(t, (1, reps)) for t in (c, s1, s2))


def _project(x, sh1, sc1, norm_g, w_in, a_q_g, a_k_g, b_q_g, b_k_g, b_kv_g):
    B, S, D = x.shape
    ts = PROJ_ROWS
    sizes = (256, 256, 256, 256, 512, 512, 64, 128, 512, 64, 8)
    offs = np.concatenate([[0], np.cumsum(sizes)])
    seg = lambda i: w_in[:, offs[i]:offs[i + 1]]
    w_p = jnp.concatenate(
        [seg(0), seg(1), seg(2), seg(3), seg(5), seg(6), seg(9), seg(8), seg(7), seg(4), seg(10),
         jnp.zeros((D, C_END - C_IW - IDX_HEADS), F32)], axis=1).astype(BF16)
    scale = HEAD_DIM ** -0.5
    gain = jnp.concatenate(
        [jnp.tile(a_q_g * scale, 2 * A_HEADS), jnp.tile(a_k_g, 2 * A_HEADS),
         jnp.tile(b_q_g * scale, B_HEADS), b_k_g, jnp.ones((HEAD_DIM,), F32)]).reshape(1, N_GAIN)
    kb_a = 8.0 * NORM_SLACK * jnp.max(jnp.abs(a_k_g))
    kb_b = 8.0 * NORM_SLACK * jnp.max(jnp.abs(b_k_g))
    zeros = lambda n: jnp.zeros((n,), F32)
    kb = jnp.concatenate([zeros(512) + kb_a, zeros(512), zeros(512) + kb_b,
                          zeros(N_GAIN - C_BKIK)]).reshape(1, N_GAIN)
    bound_a = jnp.max(jnp.abs(a_q_g)) * kb_a
    bound_b = jnp.max(jnp.abs(b_q_g)) * kb_b
    rc, rs1, rs2 = _rope_tables(S, 256)
    gid = np.arange(256) // HEAD_DIM
    bd = jnp.asarray(gid[:, None] == gid[None, :], BF16)

    row = lambda b, i: (b, 0, 0)
    full = lambda b, i: (0, 0)
    heads = lambda n, w: pl.BlockSpec((1, n, ts, w), lambda b, i: (b, 0, i, 0))
    flat = lambda w: pl.BlockSpec((1, ts, w), lambda b, i: (b, i, 0))
    hshape = lambda n, w: jax.ShapeDtypeStruct((B, n, S, w), BF16)
    outs = pl.pallas_call(
        _proj_kernel,
        grid=(B, S // ts),
        in_specs=[pl.BlockSpec((1, ts, D), lambda b, i: (b, i, 0)),
                  pl.BlockSpec((1, 1, D), row), pl.BlockSpec((1, 1, D), row),
                  pl.BlockSpec((1, D), full),
                  pl.BlockSpec((D, C_END), full),
                  pl.BlockSpec((1, N_GAIN), full),
                  pl.BlockSpec((1, N_GAIN), full),
                  pl.BlockSpec((1, LANES), full),
                  pl.BlockSpec((ts, 256), lambda b, i: (i, 0)),
                  pl.BlockSpec((ts, 256), lambda b, i: (i, 0)),
                  pl.BlockSpec((ts, 256), lambda b, i: (i, 0)),
                  pl.BlockSpec((256, 256), full)],
        out_specs=[heads(A_HEADS, LANES)] * 4
        + [heads(B_HEADS, LANES), flat(LANES), flat(HEAD_DIM), heads(IDX_HEADS, HEAD_DIM),
           flat(2 * B_LATENT), heads(A_HEADS, 2 * A_V_DIM), flat(IDX_HEADS)],
        out_shape=[hshape(A_HEADS, LANES)] * 4
        + [hshape(B_HEADS, LANES), jax.ShapeDtypeStruct((B, S, LANES), BF16),
           jax.ShapeDtypeStruct((B, S, HEAD_DIM), BF16), hshape(IDX_HEADS, HEAD_DIM),
           jax.ShapeDtypeStruct((B, S, 2 * B_LATENT), BF16), hshape(A_HEADS, 2 * A_V_DIM),
           jax.ShapeDtypeStruct((B, S, IDX_HEADS), F32)],
        compiler_params=_cparams(2),
    )(x, sh1, sc1, norm_g.reshape(1, D), w_p, gain, kb, b_kv_g.reshape(1, B_LATENT), rc, rs1, rs2,
      bd)
    plain_a = (bound_a <= PLAIN_EXP_MAX_BOUND).astype(I32).reshape(1)
    plain_b = (bound_b <= PLAIN_EXP_MAX_BOUND).astype(I32).reshape(1)
    return outs, plain_a, plain_b


def _softmax_init(plain, m_ref, acc_ref):
    acc_ref[...] = jnp.zeros(acc_ref.shape, F32)
    if not plain:
        m_ref[...] = jnp.full(m_ref.shape, NEG_BIG, F32)


def _softmax_step(plain, s, v1, m_ref, acc_ref):
    if plain:
        acc_ref[...] += jnp.dot(jnp.exp(s).astype(BF16), v1, preferred_element_type=F32)
        return
    m_old = m_ref[...]
    m_new = jnp.maximum(m_old, jnp.max(s, axis=-1, keepdims=True))
    p = jnp.exp(s - m_new).astype(BF16)
    acc_ref[...] = (jnp.exp(m_old - m_new) * acc_ref[...]
                    + jnp.dot(p, v1, preferred_element_type=F32))
    m_ref[...] = m_new


def _softmax_result(acc_ref, width):
    acc = acc_ref[...]
    return acc[:, :width] / acc[:, width:width + 1]


def _chunk_mask(q0, k0, tq, tk):
    qc = (q0 + lax.broadcasted_iota(I32, (tq, tk), 0)) // CHUNK
    kc = (k0 + lax.broadcasted_iota(I32, (tq, tk), 1)) // CHUNK
    return kc <= qc


def _diff_attn_kernel(plain_ref, lam_ref, subg_ref, q1_ref, q2_ref, k1_ref, k2_ref, v_ref, o_ref,
                      m1_ref, acc1_ref, m2_ref, acc2_ref, *, lambda_init):
    tq, tk = ATT_TQ, ATT_TK
    i = pl.program_id(2)
    n_tiles = ((i + 1) * tq + tk - 1) // tk

    def attend(plain):
        q1 = q1_ref[0, 0]
        q2 = q2_ref[0, 0]
        _softmax_init(plain, m1_ref, acc1_ref)
        _softmax_init(plain, m2_ref, acc2_ref)

        def tile(j, masked):
            ks = pl.multiple_of(j * tk, tk)
            v1 = v_ref[0, 0, pl.ds(ks, tk), :]
            s1 = _dot_nt(q1, k1_ref[0, 0, pl.ds(ks, tk), :])
            s2 = _dot_nt(q2, k2_ref[0, 0, pl.ds(ks, tk), :])
            if masked:
                ok = _chunk_mask(i * tq, ks, tq, tk)
                s1 = jnp.where(ok, s1, NEG_BIG)
                s2 = jnp.where(ok, s2, NEG_BIG)
            _softmax_step(plain, s1, v1, m1_ref, acc1_ref)
            _softmax_step(plain, s2, v1, m2_ref, acc2_ref)

        def body(j, c):
            tile(j, False)
            return c

        lax.fori_loop(0, n_tiles - 1, body, 0)
        tile(n_tiles - 1, True)

    pl.when(plain_ref[0] == 1)(lambda: attend(True))
    pl.when(plain_ref[0] != 1)(lambda: attend(False))

    lv = lam_ref[...]
    lam = (jnp.exp(jnp.sum(lv[0:1] * lv[1:2], axis=-1, keepdims=True))
           - jnp.exp(jnp.sum(lv[2:3] * lv[3:4], axis=-1, keepdims=True)) + lambda_init)
    o = _softmax_result(acc1_ref, A_V_DIM) - lam * _softmax_result(acc2_ref, A_V_DIM)
    ms = jnp.mean(o * o, axis=-1, keepdims=True)
    o = o * lax.rsqrt(ms + RMS_EPS) * subg_ref[...] * (1.0 - lambda_init)
    o_ref[0] = o.astype(BF16)


def _diff_attention(plain, q1, q2, k1, k2, v1, a_lambda, sub_g, lambda_init):
    B, H, S, dq = q1.shape
    tq = ATT_TQ
    qspec = pl.BlockSpec((1, 1, tq, dq), lambda b, h, i, p: (b, h, i, 0))
    kspec = pl.BlockSpec((1, 1, S, dq), lambda b, h, i, p: (b, h, 0, 0))
    col = lambda: pltpu.VMEM((tq, 1), F32)
    acc = lambda: pltpu.VMEM((tq, 2 * A_V_DIM), F32)
    grid_spec = pltpu.PrefetchScalarGridSpec(
        num_scalar_prefetch=1,
        grid=(B, H, S // tq),
        in_specs=[pl.BlockSpec((4, HEAD_DIM), lambda b, h, i, p: (0, 0)),
                  pl.BlockSpec((1, A_V_DIM), lambda b, h, i, p: (0, 0)),
                  qspec, qspec, kspec, kspec,
                  pl.BlockSpec((1, 1, S, 2 * A_V_DIM), lambda b, h, i, p: (b, h, 0, 0))],
        out_specs=pl.BlockSpec((1, tq, A_V_DIM), lambda b, h, i, p: (b, i, h)),
        scratch_shapes=[col(), acc(), col(), acc()],
    )
    return pl.pallas_call(
        functools.partial(_diff_attn_kernel, lambda_init=lambda_init),
        grid_spec=grid_spec,
        out_shape=jax.ShapeDtypeStruct((B, S, H * A_V_DIM), BF16),
        compiler_params=_cparams(3),
    )(plain, a_lambda, sub_g.reshape(1, A_V_DIM), q1, q2, k1, k2, v1)


def _dsa_kernel(plain_ref, iq_ref, iw_ref, ik_ref, q_ref, k_ref, lat_ref, wuv_ref, tri_ref, o_ref,
                score_ref, score_t_ref, thr_ref, m_ref, acc_ref, *, topk):
    tq, tk = DSA_TQ, DSA_TK
    nh = B_HEADS
    i = pl.program_id(1)
    n_tiles = ((i + 1) * tq + tk - 1) // tk

    iq = iq_ref[0].reshape(IDX_HEADS * tq, HEAD_DIM)
    iw = iw_ref[0]

    def score_tile(j, masked):
        ks = pl.multiple_of(j * tk, tk)
        rel = jnp.maximum(_dot_nt(iq, ik_ref[0, pl.ds(ks, tk), :]), 0.0)
        rel = rel.reshape(IDX_HEADS, tq, tk)
        score = rel[0] * iw[:, 0:1]
        for h in range(1, IDX_HEADS):
            score = score + rel[h] * iw[:, h:h + 1]
        if masked:
            score = jnp.where(_chunk_mask(i * tq, ks, tq, tk), score, -jnp.inf)
        score_ref[:, pl.ds(ks, tk)] = score
        score_t_ref[pl.ds(ks, tk), :] = score.T

    def score_body(j, c):
        score_tile(j, False)
        return c

    lax.fori_loop(0, n_tiles - 1, score_body, 0)
    score_tile(n_tiles - 1, True)

    def image_to_float(t):
        return pltpu.bitcast(jnp.where(t < 0, t ^ 0x7FFFFFFF, t), F32)

    def count(pred, cand):
        parts = []
        for r0 in range(0, tq, COUNT_ROWS):
            rows = pl.ds(r0, COUNT_ROWS)
            cand_r = cand[r0:r0 + COUNT_ROWS]

            def body(j, acc, rows=rows, cand_r=cand_r):
                ks = pl.multiple_of(j * tk, tk)
                hit = jnp.where(pred(score_ref[rows, pl.ds(ks, tk)], cand_r), 1.0, 0.0)
                for c in range(tk // LANES):
                    acc = acc + hit[:, c * LANES:(c + 1) * LANES]
                return acc
            parts.append(lax.fori_loop(0, n_tiles, body, jnp.zeros((COUNT_ROWS, LANES), F32)))
        acc = jnp.concatenate(parts, axis=0)
        return jnp.sum(acc, axis=-1, keepdims=True)

    ge = lambda s, c: s >= c
    gt = lambda s, c: s > c

    def all_of(flags):
        return (jnp.min(flags) > 0.0).astype(I32)

    def count_t(pred, cand):
        def body(j, acc):
            ks = pl.multiple_of(j * tk, tk)
            hit = jnp.where(pred(score_t_ref[pl.ds(ks, tk), :], cand), 1.0, 0.0)
            return acc + jnp.sum(hit.reshape(-1, COUNT_CHAINS, 8, tq), axis=0)
        acc = lax.fori_loop(0, n_tiles, body, jnp.zeros((COUNT_CHAINS, 8, tq), F32))
        return jnp.sum(jnp.sum(acc, axis=0), axis=0, keepdims=True)

    zero = jnp.zeros((1, tq), F32)
    n_ge0 = count_t(ge, zero)
    tie0 = jnp.where(jnp.logical_and(count_t(gt, zero) < topk, n_ge0 > topk), 1.0, 0.0)
    t0 = jnp.where(n_ge0 >= topk, 0, INT_MIN)
    settled0 = jnp.where(n_ge0 == topk, 1.0, tie0)

    def bit_body(state):
        step, t, settled, _ = state
        for _ in range(WALK_UNROLL):
            bit = jnp.where(step < 32, jnp.left_shift(jnp.int32(1), jnp.maximum(31 - step, 0)), 0)
            cand = t + bit
            n_ge = count_t(ge, image_to_float(cand))
            t = jnp.where(settled > 0.0, t, jnp.where(n_ge >= topk, cand, t))
            settled = jnp.where(n_ge == topk, 1.0, settled)
            step = step + 1
        return step, t, settled, all_of(settled)

    def bit_cond(state):
        step, _, _, all_settled = state
        return jnp.logical_and(step < 32, all_settled == 0)

    _, t, _, all_settled = lax.while_loop(
        bit_cond, bit_body, (jnp.int32(1), t0, settled0, all_of(settled0)))
    thr_row = jnp.where(t == INT_MIN, jnp.finfo(F32).min, image_to_float(t))
    thr_ref[...] = jnp.broadcast_to(thr_row, (LANES, tq)).T[:, 0:1]

    def min_where(pred, cand):
        def body(j, acc):
            ks = pl.multiple_of(j * tk, tk)
            s = score_ref[:, pl.ds(ks, tk)]
            s = jnp.where(pred(s, cand), s, jnp.inf)
            for c in range(tk // LANES):
                acc = jnp.minimum(acc, s[:, c * LANES:(c + 1) * LANES])
            return acc
        acc = lax.fori_loop(0, n_tiles, body, jnp.full((tq, LANES), jnp.inf, F32))
        return jnp.min(acc, axis=-1, keepdims=True)

    @pl.when(all_settled == 0)
    def _():
        thr0 = thr_ref[...]
        low = min_where(ge, thr0)
        nxt = min_where(gt, low)
        thr_ref[...] = jnp.where(count(ge, nxt) >= topk, nxt,
                                 jnp.where(low < jnp.inf, low, thr0))

    @pl.when(jnp.logical_or(all_settled == 0, jnp.max(tie0) > 0.0))
    def _():
        thr = thr_ref[...]
        quota = topk - count(gt, thr)

        def tie_body(j, carry):
            sl = pl.ds(pl.multiple_of(j * tk, tk), tk)
            score = score_ref[:, sl]
            eq = score == thr
            prefix = carry + jnp.dot(jnp.where(eq, 1.0, 0.0).astype(BF16), tri_ref[...],
                                     preferred_element_type=F32)
            score_ref[:, sl] = jnp.where(eq & (prefix > quota), -jnp.inf, score)
            return prefix[:, tk - 1:tk]

        lax.fori_loop(0, n_tiles, tie_body, jnp.zeros((tq, 1), F32))

    thr = thr_ref[...]

    def attend(plain):
        q = q_ref[0].reshape(nh * tq, LANES)
        _softmax_init(plain, m_ref, acc_ref)

        def attn_body(j, c):
            ks = pl.multiple_of(j * tk, tk)
            s = _dot_nt(q, k_ref[0, pl.ds(ks, tk), :]).reshape(nh, tq, tk)
            sel = score_ref[:, pl.ds(ks, tk)] >= thr
            s = jnp.where(sel[None], s, NEG_BIG).reshape(nh * tq, tk)
            _softmax_step(plain, s, lat_ref[0, pl.ds(ks, tk), :], m_ref, acc_ref)
            return c

        lax.fori_loop(0, n_tiles, attn_body, 0)

    pl.when(plain_ref[0] == 1)(lambda: attend(True))
    pl.when(plain_ref[0] != 1)(lambda: attend(False))

    o = _softmax_result(acc_ref, B_LATENT).astype(BF16)
    for h in range(nh):
        oh = jnp.dot(o[h * tq:(h + 1) * tq], wuv_ref[h], preferred_element_type=F32)
        o_ref[0, :, h * HEAD_DIM:(h + 1) * HEAD_DIM] = oh.astype(BF16)


def _dsa_attention(plain, iq, iw, ik, q, k, lat1, w_uv):
    B, nh, S, dq = q.shape
    dh = HEAD_DIM
    tq = DSA_TQ
    topk = min(IDX_TOPK_MAX, S // 4)
    col = np.arange(DSA_TK)
    tri = jnp.asarray(col[:, None] <= col[None, :], BF16)
    hspec = lambda w: pl.BlockSpec((1, nh, tq, w), lambda b, i, p: (b, 0, i, 0))
    kspec = lambda w: pl.BlockSpec((1, S, w), lambda b, i, p: (b, 0, 0))
    grid_spec = pltpu.PrefetchScalarGridSpec(
        num_scalar_prefetch=1,
        grid=(B, S // tq),
        in_specs=[hspec(dh),
                  pl.BlockSpec((1, tq, IDX_HEADS), lambda b, i, p: (b, i, 0)),
                  kspec(dh), hspec(dq), kspec(dq), kspec(2 * B_LATENT),
                  pl.BlockSpec((nh, B_LATENT, dh), lambda b, i, p: (0, 0, 0)),
                  pl.BlockSpec((DSA_TK, DSA_TK), lambda b, i, p: (0, 0))],
        out_specs=pl.BlockSpec((1, tq, nh * dh), lambda b, i, p: (b, i, 0)),
        scratch_shapes=[pltpu.VMEM((tq, S), F32),
                        pltpu.VMEM((S, tq), F32),
                        pltpu.VMEM((tq, 1), F32),
                        pltpu.VMEM((nh * tq, 1), F32),
                        pltpu.VMEM((nh * tq, 2 * B_LATENT), F32)],
    )
    return pl.pallas_call(
        functools.partial(_dsa_kernel, topk=topk),
        grid_spec=grid_spec,
        out_shape=jax.ShapeDtypeStruct((B, S, nh * dh), BF16),
        compiler_params=_cparams(2),
    )(plain, iq, iw, ik, q, k, lat1, w_uv.astype(BF16), tri)


def _out_kernel(a_ref, b_ref, x_ref, g1_ref, sh_ref, sc_ref, ng_ref, woa_ref, wob_ref, rw_ref,
                rb_ref, tri_ref, x1_ref, h2_ref, idx_ref, gate_ref, rank_ref, cnt_ref, carry_ref):
    first = jnp.logical_and(pl.program_id(0) == 0, pl.program_id(1) == 0)

    @pl.when(first)
    def _():
        carry_ref[...] = jnp.zeros(carry_ref.shape, F32)

    mix = (jnp.dot(a_ref[0], woa_ref[...], preferred_element_type=F32)
           + jnp.dot(b_ref[0], wob_ref[...], preferred_element_type=F32))
    x1 = x_ref[0] + g1_ref[0] * mix
    x1_ref[0] = x1
    ms = jnp.mean(x1 * x1, axis=-1, keepdims=True)
    h2 = x1 * lax.rsqrt(ms + RMS_EPS) * ng_ref[...]
    h2 = h2 * (1.0 + sc_ref[0]) + sh_ref[0]
    h2_ref[0] = h2

    logits = jnp.dot(h2, rw_ref[...], preferred_element_type=F32,
                     precision=lax.Precision.HIGHEST) + rb_ref[...]
    ts = logits.shape[0]
    lane_i = lax.broadcasted_iota(I32, (ts, LANES), 1)
    lane = lane_i.astype(F32)
    neg_inf = jnp.float32(-jnp.inf)
    l = jnp.where(lane_i < N_EXPERTS, logits, neg_inf)
    vals, idxs = [], []
    for _ in range(TOP_K):
        m = jnp.max(l, axis=-1, keepdims=True)
        idx = jnp.min(jnp.where(l == m, lane, float(LANES)), axis=-1, keepdims=True)
        vals.append(m)
        idxs.append(idx)
        l = jnp.where(lane == idx, neg_inf, l)
    es = [jnp.exp(v - vals[0]) for v in vals]
    denom = es[0] + es[1] + es[2] + es[3]

    onehot = jnp.zeros((ts, LANES), F32)
    for idx in idxs:
        onehot = onehot + jnp.where(lane == idx, 1.0, 0.0)
    prefix = jnp.dot(tri_ref[...], onehot.astype(BF16), preferred_element_type=F32) + carry_ref[...]
    idx_out = jnp.zeros((ts, LANES), I32)
    gate_out = jnp.zeros((ts, LANES), F32)
    rank_out = jnp.zeros((ts, LANES), I32)
    for k in range(TOP_K):
        rank = jnp.sum(jnp.where(lane == idxs[k], prefix, 0.0), axis=-1, keepdims=True)
        idx_out = jnp.where(lane_i == k, idxs[k].astype(I32), idx_out)
        gate_out = jnp.where(lane_i == k, es[k] / denom, gate_out)
        rank_out = jnp.where(lane_i == k, rank.astype(I32), rank_out)
    idx_ref[0] = idx_out
    gate_ref[0] = gate_out
    rank_ref[0] = rank_out
    carry = carry_ref[...] + jnp.sum(onehot, axis=0, keepdims=True)
    carry_ref[...] = carry
    cnt_ref[...] = carry


def _out_and_route(a_out, b_out, x, g1, sh2, sc2, norm_g, w_out, router_w, router_b):
    B, S, D = x.shape
    ts = OUT_ROWS
    aw = a_out.shape[-1]
    w_bf = w_out.astype(BF16)
    rw = jnp.concatenate([router_w, jnp.zeros((D, LANES - N_EXPERTS), F32)], axis=1)
    rb = jnp.concatenate([router_b, jnp.zeros((LANES - N_EXPERTS,), F32)]).reshape(1, LANES)
    r = np.arange(ts)
    tri = jnp.asarray(r[:, None] > r[None, :], BF16)
    row = lambda b, i: (b, 0, 0)
    full = lambda b, i: (0, 0)
    tok = lambda w: pl.BlockSpec((1, ts, w), lambda b, i: (b, i, 0))
    return pl.pallas_call(
        _out_kernel,
        grid=(B, S // ts),
        in_specs=[tok(aw), tok(D - aw), tok(D),
                  pl.BlockSpec((1, 1, D), row), pl.BlockSpec((1, 1, D), row),
                  pl.BlockSpec((1, 1, D), row), pl.BlockSpec((1, D), full),
                  pl.BlockSpec((aw, D), full), pl.BlockSpec((D - aw, D), full),
                  pl.BlockSpec((D, LANES), full), pl.BlockSpec((1, LANES), full),
                  pl.BlockSpec((ts, ts), full)],
        out_specs=[tok(D), tok(D), tok(LANES), tok(LANES), tok(LANES),
                   pl.BlockSpec((1, LANES), full)],
        out_shape=[jax.ShapeDtypeStruct((B, S, D), F32), jax.ShapeDtypeStruct((B, S, D), F32),
                   jax.ShapeDtypeStruct((B, S, LANES), I32),
                   jax.ShapeDtypeStruct((B, S, LANES), F32),
                   jax.ShapeDtypeStruct((B, S, LANES), I32),
                   jax.ShapeDtypeStruct((1, LANES), F32)],
        scratch_shapes=[pltpu.VMEM((1, LANES), F32)],
        compiler_params=_cparams(2),
    )(a_out, b_out, x, g1, sh2, sc2, norm_g.reshape(1, D), w_bf[:aw], w_bf[aw:], rw, rb, tri)


def _row_copy_wait(src_ref, dst_ref, sem, rows):
    pltpu.make_async_copy(src_ref.at[pl.ds(0, rows), :], dst_ref.at[pl.ds(0, rows), :], sem).wait()


def _dispatch_kernel(dest_ref, h_ref, xin_ref, xbuf_ref, sem):
    del xin_ref
    rows = h_ref.shape[0]

    def body(r, c):
        for k in range(TOP_K):
            d = dest_ref[0, 0, r * TOP_K + k]
            pltpu.make_async_copy(h_ref.at[pl.ds(r, 1), :], xbuf_ref.at[pl.ds(d, 1), :],
                                  sem).start(priority=k % DMA_QUEUES)
        return c

    lax.fori_loop(0, rows, body, 0)
    for _ in range(TOP_K):
        _row_copy_wait(h_ref, xbuf_ref, sem, rows)


def _dispatch(h2, dest, n_rows):
    T, D = h2.shape
    ts = min(DISPATCH_ROWS, T)
    xbuf0 = jnp.zeros((n_rows, D), F32)
    return pl.pallas_call(
        _dispatch_kernel,
        grid=(T // ts,),
        in_specs=[pl.BlockSpec((1, 1, ts * TOP_K), lambda i: (i, 0, 0), memory_space=pltpu.SMEM),
                  pl.BlockSpec((ts, D), lambda i: (i, 0)),
                  pl.BlockSpec(memory_space=pl.ANY)],
        out_specs=pl.BlockSpec(memory_space=pl.ANY),
        out_shape=jax.ShapeDtypeStruct((n_rows, D), F32),
        scratch_shapes=[pltpu.SemaphoreType.DMA(())],
        input_output_aliases={2: 0},
        compiler_params=_cparams(1),
    )(dest.reshape(T // ts, 1, ts * TOP_K), h2, xbuf0)


def _moe_kernel(be_ref, nused_ref, x_ref, wg_ref, bg_ref, wu_ref, bu_ref, wd_ref, bd_ref, y_ref,
                wg_s, wu_s, wd_s):
    i = pl.program_id(0)
    used = i < nused_ref[0]
    new_expert = jnp.logical_or(i == 0, be_ref[i] != be_ref[jnp.maximum(i - 1, 0)])

    @pl.when(jnp.logical_and(used, new_expert))
    def _():
        wg_s[...] = wg_ref[0].astype(BF16)
        wu_s[...] = wu_ref[0].astype(BF16)
        wd_s[...] = wd_ref[0].astype(BF16)

    @pl.when(used)
    def _():
        xb = x_ref[...].astype(BF16)
        g = jnp.dot(xb, wg_s[...], preferred_element_type=F32) + bg_ref[0]
        u = jnp.dot(xb, wu_s[...], preferred_element_type=F32) + bu_ref[0]
        g = jnp.minimum(g, SWIGLU_LIMIT)
        u = jnp.clip(u, -SWIGLU_LIMIT, SWIGLU_LIMIT)
        a = g * (1.0 / (1.0 + jnp.exp(-SWIGLU_ALPHA * g))) * (u + 1.0)
        y_ref[...] = jnp.dot(a.astype(BF16), wd_s[...], preferred_element_type=F32) + bd_ref[0]

    @pl.when(jnp.logical_not(used))
    def _():
        y_ref[...] = jnp.zeros(y_ref.shape, F32)


def _moe_experts(xbuf, block_expert, n_used, w_gate, b_gate, w_up, b_up, w_down, b_down):
    R, D = xbuf.shape
    E, _, F = w_gate.shape
    G = MOE_ROWS
    nb = R // G
    last = lambda i, nu: jnp.maximum(jnp.minimum(i, nu[0] - 1), 0)
    blk = lambda i, be, nu: (last(i, nu), 0)
    wsel = lambda i, be, nu: (be[last(i, nu)], 0, 0)
    grid_spec = pltpu.PrefetchScalarGridSpec(
        num_scalar_prefetch=2,
        grid=(nb,),
        in_specs=[pl.BlockSpec((G, D), blk),
                  pl.BlockSpec((1, D, F), wsel), pl.BlockSpec((1, 1, F), wsel),
                  pl.BlockSpec((1, D, F), wsel), pl.BlockSpec((1, 1, F), wsel),
                  pl.BlockSpec((1, F, D), wsel), pl.BlockSpec((1, 1, D), wsel)],
        out_specs=pl.BlockSpec((G, D), lambda i, be, nu: (i, 0)),
        scratch_shapes=[pltpu.VMEM((D, F), BF16), pltpu.VMEM((D, F), BF16),
                        pltpu.VMEM((F, D), BF16)],
    )
    return pl.pallas_call(
        _moe_kernel,
        grid_spec=grid_spec,
        out_shape=jax.ShapeDtypeStruct((R, D), F32),
        compiler_params=_cparams(1),
    )(block_expert, n_used, xbuf, w_gate, b_gate.reshape(E, 1, F),
      w_up, b_up.reshape(E, 1, F), w_down, b_down.reshape(E, 1, D))


def _combine_kernel(dest_ref, gate_ref, x1_ref, g2_ref, ybuf_ref, o_ref, buf_ref, sem):
    rows = x1_ref.shape[0]

    def body(r, c):
        for k in range(TOP_K):
            d = dest_ref[0, 0, r * TOP_K + k]
            pltpu.make_async_copy(ybuf_ref.at[pl.ds(d, 1), :], buf_ref.at[k, pl.ds(r, 1), :],
                                  sem).start(priority=k % DMA_QUEUES)
        return c

    lax.fori_loop(0, rows, body, 0)
    for k in range(TOP_K):
        _row_copy_wait(ybuf_ref, buf_ref.at[k], sem, rows)
    gate = gate_ref[...]
    y = buf_ref[0] * gate[:, 0:1]
    for k in range(1, TOP_K):
        y = y + buf_ref[k] * gate[:, k:k + 1]
    o_ref[...] = x1_ref[...] + g2_ref[0] * y


def _combine(ybuf, dest, gates, x1, g2, seq):
    T, D = x1.shape
    ts = COMBINE_ROWS
    per_seq = seq // ts
    return pl.pallas_call(
        _combine_kernel,
        grid=(T // ts,),
        in_specs=[pl.BlockSpec((1, 1, ts * TOP_K), lambda i: (i, 0, 0), memory_space=pltpu.SMEM),
                  pl.BlockSpec((ts, LANES), lambda i: (i, 0)),
                  pl.BlockSpec((ts, D), lambda i: (i, 0)),
                  pl.BlockSpec((1, 1, D), lambda i: (i // per_seq, 0, 0)),
                  pl.BlockSpec(memory_space=pl.ANY)],
        out_specs=pl.BlockSpec((ts, D), lambda i: (i, 0)),
        out_shape=jax.ShapeDtypeStruct((T, D), F32),
        scratch_shapes=[pltpu.VMEM((TOP_K, ts, D), F32), pltpu.SemaphoreType.DMA(())],
        compiler_params=_cparams(1),
    )(dest.reshape(T // ts, 1, ts * TOP_K), gates, x1, g2, ybuf)


def _layer(x, mod, lambda_init, norm1_g, norm2_g, w_in, w_out, a_q_norm_g, a_k_norm_g, a_lambda,
           a_sub_g, b_q_norm_g, b_k_norm_g, b_kv_norm_g, b_w_uv, router_w, router_b, w_gate,
           b_gate, w_up, b_up, w_down, b_down):
    B, S, D = x.shape
    T = B * S
    sh1, sc1, g1, sh2, sc2, g2 = [m.reshape(B, 1, D) for m in jnp.split(mod, 6, axis=-1)]

    (aq1, aq2, ak1, ak2, bq, bk, ik, iq, blat1, av1, iw), plain_a, plain_b = _project(
        x, sh1, sc1, norm1_g, w_in, a_q_norm_g, a_k_norm_g, b_q_norm_g, b_k_norm_g, b_kv_norm_g)
    a_out = _diff_attention(plain_a, aq1, aq2, ak1, ak2, av1, a_lambda, a_sub_g, lambda_init)
    b_out = _dsa_attention(plain_b, iq, iw, ik, bq, bk, blat1, b_w_uv)

    x1, h2, top_idx, gates, rank, counts = _out_and_route(
        a_out, b_out, x, g1, sh2, sc2, norm2_g, w_out, router_w, router_b)

    G = MOE_ROWS
    counts = counts[0, :N_EXPERTS].astype(I32)
    padded = ((counts + G - 1) // G) * G
    pcum = jnp.cumsum(padded)
    poff = pcum - padded
    nb = (T * TOP_K) // G + N_EXPERTS
    starts = jnp.arange(nb, dtype=I32) * G
    block_expert = jnp.minimum(
        jnp.sum((pcum[None, :] <= starts[:, None]).astype(I32), axis=1), N_EXPERTS - 1)
    n_used = (pcum[-1:] // G).astype(I32)
    top_idx = top_idx.reshape(T, LANES)[:, :TOP_K]
    dest = (poff[top_idx] + rank.reshape(T, LANES)[:, :TOP_K]).astype(I32).reshape(T * TOP_K)

    xbuf = _dispatch(h2.reshape(T, D), dest, nb * G)
    ybuf = _moe_experts(xbuf, block_expert.astype(I32), n_used, w_gate, b_gate, w_up, b_up,
                        w_down, b_down)
    out = _combine(ybuf, dest, gates.reshape(T, LANES), x1.reshape(T, D), g2, S)
    return out.reshape(B, S, D)


def kernel(x, c, norm1_g, norm2_g, w_ada, b_ada, w_in, w_out, a_q_norm_g, a_k_norm_g, a_lambda,
           a_sub_g, b_q_norm_g, b_k_norm_g, b_kv_norm_g, b_w_uv, router_w, router_b, w_gate,
           b_gate, w_up, b_up, w_down, b_down):
    depth = w_in.shape[0]
    for l in range(depth):
        lambda_init = 0.8 - 0.6 * math.exp(-0.3 * l)
        mod = _ada(c, w_ada[l], b_ada[l])
        x = _layer(x, mod, lambda_init, norm1_g[l], norm2_g[l], w_in[l], w_out[l], a_q_norm_g[l],
                   a_k_norm_g[l], a_lambda[l], a_sub_g[l], b_q_norm_g[l], b_k_norm_g[l],
                   b_kv_norm_g[l], b_w_uv[l], router_w[l], router_b[l], w_gate[l], b_gate[l],
                   w_up[l], b_up[l], w_down[l], b_down[l])
    return x
```

```python
import functools
import math

import numpy as np
import jax
import jax.numpy as jnp
from jax import lax
from jax.experimental import pallas as pl
from jax.experimental.pallas import tpu as pltpu

F32 = jnp.float32
BF16 = jnp.bfloat16
I32 = jnp.int32

CHUNK = 64
HEAD_DIM = 64
ROT_DIM = HEAD_DIM // 4
ROPE_THETA = 500000.0
RMS_EPS = 1e-6
A_HEADS = 4
A_V_DIM = 2 * HEAD_DIM
B_HEADS = 8
B_LATENT = 128
IDX_HEADS = 8
IDX_TOPK_MAX = 256
N_EXPERTS = 32
TOP_K = 4
SWIGLU_LIMIT = 7.0
SWIGLU_ALPHA = 1.702

LANES = 128
INT_MIN = -(2 ** 31)
NEG_BIG = -1e30
VMEM_LIMIT = 56 * 1024 * 1024

NORM_SLACK = 1.01
PLAIN_EXP_MAX_BOUND = 30.0

PROJ_ROWS = 512
ATT_TQ = 512
ATT_TK = 512
DSA_TQ = 256
DSA_TK = 512
COUNT_ROWS = 128
COUNT_CHAINS = 8
WALK_UNROLL = 4
OUT_ROWS = 512
MOE_ROWS = 512
DISPATCH_ROWS = 1024
COMBINE_ROWS = 512


def _cparams(n_axes):
    return pltpu.CompilerParams(
        dimension_semantics=("arbitrary",) * n_axes, vmem_limit_bytes=VMEM_LIMIT)


def _dot_nt(a, b):
    return lax.dot_general(a, b, (((1,), (1,)), ((), ())), preferred_element_type=F32)


def _ada_kernel(c_ref, w_ref, b_ref, o_ref):
    c = c_ref[...]
    sc = c / (1.0 + jnp.exp(-c))
    o_ref[...] = jnp.dot(sc, w_ref[...], preferred_element_type=F32,
                         precision=lax.Precision.HIGHEST) + b_ref[...]


def _ada(c, w, b):
    B, D = c.shape
    N = w.shape[1]
    return pl.pallas_call(
        _ada_kernel,
        grid=(N // D,),
        in_specs=[pl.BlockSpec((B, D), lambda j: (0, 0)),
                  pl.BlockSpec((D, D), lambda j: (0, j)),
                  pl.BlockSpec((1, D), lambda j: (0, j))],
        out_specs=pl.BlockSpec((B, D), lambda j: (0, j)),
        out_shape=jax.ShapeDtypeStruct((B, N), F32),
        compiler_params=_cparams(1),
    )(c, w, b.reshape(1, N))


C_QK = 0
C_BQ = 1024
C_BKIK = 1536
C_IQ = 1664
C_LAT = 2176
C_AV = 2304
C_IW = 2816
C_END = 2944
N_GAIN = C_IQ


def _group_sumsq(p, bd):
    sq = p * p
    hi = sq.astype(BF16)
    lo = (sq - hi.astype(F32)).astype(BF16)
    return (jnp.dot(hi, bd, preferred_element_type=F32)
            + jnp.dot(lo, bd, preferred_element_type=F32))


def _rope(y, c, s1, s2):
    w = y.shape[1]
    return y * c + pltpu.roll(y, w - ROT_DIM // 2, 1) * s1 + pltpu.roll(y, ROT_DIM // 2, 1) * s2


def _proj_kernel(x_ref, sh_ref, sc_ref, g_ref, w_ref, gain_ref, kb_ref, latg_ref, rc_ref, rs1_ref,
                 rs2_ref, bd_ref,
                 aq1_ref, aq2_ref, ak1_ref, ak2_ref, bq_ref, bk_ref, ik_ref, iq_ref, lat_ref,
                 av_ref, iw_ref):
    x = x_ref[0]
    ms = jnp.mean(x * x, axis=-1, keepdims=True)
    h = x * lax.rsqrt(ms + RMS_EPS) * g_ref[...]
    h = (h * (1.0 + sc_ref[0]) + sh_ref[0]).astype(BF16)

    rc, rs1, rs2 = rc_ref[...], rs1_ref[...], rs2_ref[...]
    bd = bd_ref[...]
    ts = x.shape[0]
    lane = lax.broadcasted_iota(I32, (ts, LANES), 1)
    ones = jnp.ones((ts, LANES), F32)

    def proj(c0, width):
        return jnp.dot(h, w_ref[:, c0:c0 + width], preferred_element_type=F32)

    def normed(p, c0):
        width = p.shape[1]
        ss = _group_sumsq(p, bd[:width, :width])
        return p * lax.rsqrt(ss * (1.0 / HEAD_DIM) + RMS_EPS) * gain_ref[:, c0:c0 + width]

    def store_slots(ref, y, extra, first_head=0):
        for pair in range(y.shape[1] // LANES):
            z = y[:, pair * LANES:(pair + 1) * LANES]
            e = extra[:, pair * LANES:(pair + 1) * LANES]
            even = jnp.where(lane < HEAD_DIM, z,
                             jnp.where(lane == HEAD_DIM, pltpu.roll(e, HEAD_DIM, 1), 0.0))
            odd = jnp.where(lane < HEAD_DIM, pltpu.roll(z, HEAD_DIM, 1),
                            jnp.where(lane == HEAD_DIM, e, 0.0))
            ref[0, first_head + 2 * pair] = even.astype(BF16)
            ref[0, first_head + 2 * pair + 1] = odd.astype(BF16)

    def query(c0):
        y = _rope(normed(proj(c0, 256), c0), rc, rs1, rs2)
        norm = jnp.sqrt(jnp.dot((y * y).astype(BF16), bd, preferred_element_type=F32))
        return y, -norm * kb_ref[:, c0:c0 + 256]

    def key(c0):
        return _rope(normed(proj(c0, 256), c0), rc, rs1, rs2), jnp.ones((ts, 256), F32)

    store_slots(aq1_ref, *query(0))
    store_slots(aq2_ref, *query(256))
    store_slots(ak1_ref, *key(512))
    store_slots(ak2_ref, *key(768))
    for half in range(2):
        store_slots(bq_ref, *query(C_BQ + 256 * half), first_head=4 * half)
        y = _rope(proj(C_IQ + 256 * half, 256), rc, rs1, rs2)
        for j in range(4):
            iq_ref[0, 4 * half + j] = y[:, j * HEAD_DIM:(j + 1) * HEAD_DIM].astype(BF16)

    p = proj(C_BKIK, LANES)
    y = _rope(jnp.where(lane < HEAD_DIM, normed(p, C_BKIK), p),
              rc[:, :LANES], rs1[:, :LANES], rs2[:, :LANES])
    bk_ref[0] = jnp.where(lane < HEAD_DIM, y, jnp.where(lane == HEAD_DIM, 1.0, 0.0)).astype(BF16)
    ik_ref[0] = y[:, HEAD_DIM:].astype(BF16)

    p = proj(C_LAT, LANES)
    ms = jnp.mean(p * p, axis=-1, keepdims=True)
    lat = p * lax.rsqrt(ms + RMS_EPS) * latg_ref[...]
    lat_ref[0] = jnp.concatenate([lat, ones], axis=1).astype(BF16)

    for j in range(A_HEADS):
        av_ref[0, j] = jnp.concatenate([proj(C_AV + j * A_V_DIM, A_V_DIM), ones],
                                       axis=1).astype(BF16)

    p = proj(C_IW, LANES)
    iw_ref[0] = p[:, :IDX_HEADS] * (IDX_HEADS ** -0.5 * HEAD_DIM ** -0.5)


def _rope_tables(S, width):
    half = ROT_DIM // 2
    pos = jnp.arange(S, dtype=F32)
    inv = ROPE_THETA ** (-jnp.arange(0, ROT_DIM, 2, dtype=F32) / ROT_DIM)
    ang = pos[:, None] * inv[None, :]
    cos, sin = jnp.cos(ang), jnp.sin(ang)
    zeros = jnp.zeros((S, HEAD_DIM - ROT_DIM), F32)
    c = jnp.concatenate([cos, cos, zeros + 1.0], axis=1)
    s1 = jnp.concatenate([-sin, jnp.zeros((S, half), F32), zeros], axis=1)
    s2 = jnp.concatenate([jnp.zeros((S, half), F32), sin, zeros], axis=1)
    reps = width // HEAD_DIM
    return tuple(jnp.tile(t, (1, reps)) for t in (c, s1, s2))


def _project(x, sh1, sc1, norm_g, w_in, a_q_g, a_k_g, b_q_g, b_k_g, b_kv_g):
    B, S, D = x.shape
    ts = PROJ_ROWS
    sizes = (256, 256, 256, 256, 512, 512, 64, 128, 512, 64, 8)
    offs = np.concatenate([[0], np.cumsum(sizes)])
    seg = lambda i: w_in[:, offs[i]:offs[i + 1]]
    w_p = jnp.concatenate(
        [seg(0), seg(1), seg(2), seg(3), seg(5), seg(6), seg(9), seg(8), seg(7), seg(4), seg(10),
         jnp.zeros((D, C_END - C_IW - IDX_HEADS), F32)], axis=1).astype(BF16)
    scale = HEAD_DIM ** -0.5
    gain = jnp.concatenate(
        [jnp.tile(a_q_g * scale, 2 * A_HEADS), jnp.tile(a_k_g, 2 * A_HEADS),
         jnp.tile(b_q_g * scale, B_HEADS), b_k_g, jnp.ones((HEAD_DIM,), F32)]).reshape(1, N_GAIN)
    kb_a = 8.0 * NORM_SLACK * jnp.max(jnp.abs(a_k_g))
    kb_b = 8.0 * NORM_SLACK * jnp.max(jnp.abs(b_k_g))
    zeros = lambda n: jnp.zeros((n,), F32)
    kb = jnp.concatenate([zeros(512) + kb_a, zeros(512), zeros(512) + kb_b,
                          zeros(N_GAIN - C_BKIK)]).reshape(1, N_GAIN)
    bound_a = jnp.max(jnp.abs(a_q_g)) * kb_a
    bound_b = jnp.max(jnp.abs(b_q_g)) * kb_b
    rc, rs1, rs2 = _rope_tables(S, 256)
    gid = np.arange(256) // HEAD_DIM
    bd = jnp.asarray(gid[:, None] == gid[None, :], BF16)

    row = lambda b, i: (b, 0, 0)
    full = lambda b, i: (0, 0)
    heads = lambda n, w: pl.BlockSpec((1, n, ts, w), lambda b, i: (b, 0, i, 0))
    flat = lambda w: pl.BlockSpec((1, ts, w), lambda b, i: (b, i, 0))
    hshape = lambda n, w: jax.ShapeDtypeStruct((B, n, S, w), BF16)
    outs = pl.pallas_call(
        _proj_kernel,
        grid=(B, S // ts),
        in_specs=[pl.BlockSpec((1, ts, D), lambda b, i: (b, i, 0)),
                  pl.BlockSpec((1, 1, D), row), pl.BlockSpec((1, 1, D), row),
                  pl.BlockSpec((1, D), full),
                  pl.BlockSpec((D, C_END), full),
                  pl.BlockSpec((1, N_GAIN), full),
                  pl.BlockSpec((1, N_GAIN), full),
                  pl.BlockSpec((1, LANES), full),
                  pl.BlockSpec((ts, 256), lambda b, i: (i, 0)),
                  pl.BlockSpec((ts, 256), lambda b, i: (i, 0)),
                  pl.BlockSpec((ts, 256), lambda b, i: (i, 0)),
                  pl.BlockSpec((256, 256), full)],
        out_specs=[heads(A_HEADS, LANES)] * 4
        + [heads(B_HEADS, LANES), flat(LANES), flat(HEAD_DIM), heads(IDX_HEADS, HEAD_DIM),
           flat(2 * B_LATENT), heads(A_HEADS, 2 * A_V_DIM), flat(IDX_HEADS)],
        out_shape=[hshape(A_HEADS, LANES)] * 4
        + [hshape(B_HEADS, LANES), jax.ShapeDtypeStruct((B, S, LANES), BF16),
           jax.ShapeDtypeStruct((B, S, HEAD_DIM), BF16), hshape(IDX_HEADS, HEAD_DIM),
           jax.ShapeDtypeStruct((B, S, 2 * B_LATENT), BF16), hshape(A_HEADS, 2 * A_V_DIM),
           jax.ShapeDtypeStruct((B, S, IDX_HEADS), F32)],
        compiler_params=_cparams(2),
    )(x, sh1, sc1, norm_g.reshape(1, D), w_p, gain, kb, b_kv_g.reshape(1, B_LATENT), rc, rs1, rs2,
      bd)
    plain_a = (bound_a <= PLAIN_EXP_MAX_BOUND).astype(I32).reshape(1)
    plain_b = (bound_b <= PLAIN_EXP_MAX_BOUND).astype(I32).reshape(1)
    return outs, plain_a, plain_b


def _softmax_init(plain, m_ref, acc_ref):
    acc_ref[...] = jnp.zeros(acc_ref.shape, F32)
    if not plain:
        m_ref[...] = jnp.full(m_ref.shape, NEG_BIG, F32)


def _softmax_step(plain, s, v1, m_ref, acc_ref):
    if plain:
        acc_ref[...] += jnp.dot(jnp.exp(s).astype(BF16), v1, preferred_element_type=F32)
        return
    m_old = m_ref[...]
    m_new = jnp.maximum(m_old, jnp.max(s, axis=-1, keepdims=True))
    p = jnp.exp(s - m_new).astype(BF16)
    acc_ref[...] = (jnp.exp(m_old - m_new) * acc_ref[...]
                    + jnp.dot(p, v1, preferred_element_type=F32))
    m_ref[...] = m_new


def _softmax_result(acc_ref, width):
    acc = acc_ref[...]
    return acc[:, :width] / acc[:, width:width + 1]


def _chunk_mask(q0, k0, tq, tk):
    qc = (q0 + lax.broadcasted_iota(I32, (tq, tk), 0)) // CHUNK
    kc = (k0 + lax.broadcasted_iota(I32, (tq, tk), 1)) // CHUNK
    return kc <= qc


def _diff_attn_kernel(plain_ref, lam_ref, subg_ref, q1_ref, q2_ref, k1_ref, k2_ref, v_ref, o_ref,
                      m1_ref, acc1_ref, m2_ref, acc2_ref, *, lambda_init):
    tq, tk = ATT_TQ, ATT_TK
    i = pl.program_id(2)
    n_tiles = ((i + 1) * tq + tk - 1) // tk

    def attend(plain):
        q1 = q1_ref[0, 0]
        q2 = q2_ref[0, 0]
        _softmax_init(plain, m1_ref, acc1_ref)
        _softmax_init(plain, m2_ref, acc2_ref)

        def tile(j, masked):
            ks = pl.multiple_of(j * tk, tk)
            v1 = v_ref[0, 0, pl.ds(ks, tk), :]
            s1 = _dot_nt(q1, k1_ref[0, 0, pl.ds(ks, tk), :])
            s2 = _dot_nt(q2, k2_ref[0, 0, pl.ds(ks, tk), :])
            if masked:
                ok = _chunk_mask(i * tq, ks, tq, tk)
                s1 = jnp.where(ok, s1, NEG_BIG)
                s2 = jnp.where(ok, s2, NEG_BIG)
            _softmax_step(plain, s1, v1, m1_ref, acc1_ref)
            _softmax_step(plain, s2, v1, m2_ref, acc2_ref)

        def body(j, c):
            tile(j, False)
            return c

        lax.fori_loop(0, n_tiles - 1, body, 0)
        tile(n_tiles - 1, True)

    pl.when(plain_ref[0] == 1)(lambda: attend(True))
    pl.when(plain_ref[0] != 1)(lambda: attend(False))

    lv = lam_ref[...]
    lam = (jnp.exp(jnp.sum(lv[0:1] * lv[1:2], axis=-1, keepdims=True))
           - jnp.exp(jnp.sum(lv[2:3] * lv[3:4], axis=-1, keepdims=True)) + lambda_init)
    o = _softmax_result(acc1_ref, A_V_DIM) - lam * _softmax_result(acc2_ref, A_V_DIM)
    ms = jnp.mean(o * o, axis=-1, keepdims=True)
    o = o * lax.rsqrt(ms + RMS_EPS) * subg_ref[...] * (1.0 - lambda_init)
    o_ref[0] = o.astype(BF16)


def _diff_attention(plain, q1, q2, k1, k2, v1, a_lambda, sub_g, lambda_init):
    B, H, S, dq = q1.shape
    tq = ATT_TQ
    qspec = pl.BlockSpec((1, 1, tq, dq), lambda b, h, i, p: (b, h, i, 0))
    kspec = pl.BlockSpec((1, 1, S, dq), lambda b, h, i, p: (b, h, 0, 0))
    col = lambda: pltpu.VMEM((tq, 1), F32)
    acc = lambda: pltpu.VMEM((tq, 2 * A_V_DIM), F32)
    grid_spec = pltpu.PrefetchScalarGridSpec(
        num_scalar_prefetch=1,
        grid=(B, H, S // tq),
        in_specs=[pl.BlockSpec((4, HEAD_DIM), lambda b, h, i, p: (0, 0)),
                  pl.BlockSpec((1, A_V_DIM), lambda b, h, i, p: (0, 0)),
                  qspec, qspec, kspec, kspec,
                  pl.BlockSpec((1, 1, S, 2 * A_V_DIM), lambda b, h, i, p: (b, h, 0, 0))],
        out_specs=pl.BlockSpec((1, tq, A_V_DIM), lambda b, h, i, p: (b, i, h)),
        scratch_shapes=[col(), acc(), col(), acc()],
    )
    return pl.pallas_call(
        functools.partial(_diff_attn_kernel, lambda_init=lambda_init),
        grid_spec=grid_spec,
        out_shape=jax.ShapeDtypeStruct((B, S, H * A_V_DIM), BF16),
        compiler_params=_cparams(3),
    )(plain, a_lambda, sub_g.reshape(1, A_V_DIM), q1, q2, k1, k2, v1)


def _dsa_kernel(plain_ref, iq_ref, iw_ref, ik_ref, q_ref, k_ref, lat_ref, wuv_ref, tri_ref, o_ref,
                score_ref, score_t_ref, thr_ref, m_ref, acc_ref, *, topk):
    tq, tk = DSA_TQ, DSA_TK
    nh = B_HEADS
    i = pl.program_id(1)
    n_tiles = ((i + 1) * tq + tk - 1) // tk

    iq = iq_ref[0].reshape(IDX_HEADS * tq, HEAD_DIM)
    iw = iw_ref[0]

    def score_tile(j, masked):
        ks = pl.multiple_of(j * tk, tk)
        rel = jnp.maximum(_dot_nt(iq, ik_ref[0, pl.ds(ks, tk), :]), 0.0)
        rel = rel.reshape(IDX_HEADS, tq, tk)
        score = rel[0] * iw[:, 0:1]
        for h in range(1, IDX_HEADS):
            score = score + rel[h] * iw[:, h:h + 1]
        if masked:
            score = jnp.where(_chunk_mask(i * tq, ks, tq, tk), score, -jnp.inf)
        score_ref[:, pl.ds(ks, tk)] = score
        score_t_ref[pl.ds(ks, tk), :] = score.T

    def score_body(j, c):
        score_tile(j, False)
        return c

    lax.fori_loop(0, n_tiles - 1, score_body, 0)
    score_tile(n_tiles - 1, True)

    def image_to_float(t):
        return pltpu.bitcast(jnp.where(t < 0, t ^ 0x7FFFFFFF, t), F32)

    def count(pred, cand):
        parts = []
        for r0 in range(0, tq, COUNT_ROWS):
            rows = pl.ds(r0, COUNT_ROWS)
            cand_r = cand[r0:r0 + COUNT_ROWS]

            def body(j, acc, rows=rows, cand_r=cand_r):
                ks = pl.multiple_of(j * tk, tk)
                hit = jnp.where(pred(score_ref[rows, pl.ds(ks, tk)], cand_r), 1.0, 0.0)
                for c in range(tk // LANES):
                    acc = acc + hit[:, c * LANES:(c + 1) * LANES]
                return acc
            parts.append(lax.fori_loop(0, n_tiles, body, jnp.zeros((COUNT_ROWS, LANES), F32)))
        acc = jnp.concatenate(parts, axis=0)
        return jnp.sum(acc, axis=-1, keepdims=True)

    ge = lambda s, c: s >= c
    gt = lambda s, c: s > c

    def all_of(flags):
        return (jnp.min(flags) > 0.0).astype(I32)

    def count_t(pred, cand):
        def body(j, acc):
            ks = pl.multiple_of(j * tq, tq)
            hit = jnp.where(pred(score_t_ref[pl.ds(ks, tq), :], cand), 1.0, 0.0)
            return acc + jnp.sum(hit.reshape(-1, COUNT_CHAINS, 8, tq), axis=0)
        acc = lax.fori_loop(0, i + 1, body, jnp.zeros((COUNT_CHAINS, 8, tq), F32))
        return jnp.sum(jnp.sum(acc, axis=0), axis=0, keepdims=True)

    zero = jnp.zeros((1, tq), F32)
    n_ge0 = count_t(ge, zero)
    tie0 = jnp.where(jnp.logical_and(count_t(gt, zero) < topk, n_ge0 > topk), 1.0, 0.0)
    t0 = jnp.where(n_ge0 >= topk, 0, INT_MIN)
    settled0 = jnp.where(n_ge0 == topk, 1.0, tie0)

    def bit_body(state):
        step, t, settled, _ = state
        for _ in range(WALK_UNROLL):
            bit = jnp.where(step < 32, jnp.left_shift(jnp.int32(1), jnp.maximum(31 - step, 0)), 0)
            cand = t + bit
            n_ge = count_t(ge, image_to_float(cand))
            t = jnp.where(settled > 0.0, t, jnp.where(n_ge >= topk, cand, t))
            settled = jnp.where(n_ge == topk, 1.0, settled)
            step = step + 1
        return step, t, settled, all_of(settled)

    def bit_cond(state):
        step, _, _, all_settled = state
        return jnp.logical_and(step < 32, all_settled == 0)

    _, t, _, all_settled = lax.while_loop(
        bit_cond, bit_body, (jnp.int32(1), t0, settled0, all_of(settled0)))
    thr_row = jnp.where(t == INT_MIN, jnp.finfo(F32).min, image_to_float(t))
    thr_ref[...] = jnp.broadcast_to(thr_row, (LANES, tq)).T[:, 0:1]

    def min_where(pred, cand):
        def body(j, acc):
            ks = pl.multiple_of(j * tk, tk)
            s = score_ref[:, pl.ds(ks, tk)]
            s = jnp.where(pred(s, cand), s, jnp.inf)
            for c in range(tk // LANES):
                acc = jnp.minimum(acc, s[:, c * LANES:(c + 1) * LANES])
            return acc
        acc = lax.fori_loop(0, n_tiles, body, jnp.full((tq, LANES), jnp.inf, F32))
        return jnp.min(acc, axis=-1, keepdims=True)

    @pl.when(all_settled == 0)
    def _():
        thr0 = thr_ref[...]
        low = min_where(ge, thr0)
        nxt = min_where(gt, low)
        thr_ref[...] = jnp.where(count(ge, nxt) >= topk, nxt,
                                 jnp.where(low < jnp.inf, low, thr0))

    @pl.when(jnp.logical_or(all_settled == 0, jnp.max(tie0) > 0.0))
    def _():
        thr = thr_ref[...]
        quota = topk - count(gt, thr)

        def tie_body(j, carry):
            sl = pl.ds(pl.multiple_of(j * tk, tk), tk)
            score = score_ref[:, sl]
            eq = score == thr
            prefix = carry + jnp.dot(jnp.where(eq, 1.0, 0.0).astype(BF16), tri_ref[...],
                                     preferred_element_type=F32)
            score_ref[:, sl] = jnp.where(eq & (prefix > quota), -jnp.inf, score)
            return prefix[:, tk - 1:tk]

        lax.fori_loop(0, n_tiles, tie_body, jnp.zeros((tq, 1), F32))

    thr = thr_ref[...]

    def attend(plain):
        q = q_ref[0].reshape(nh * tq, LANES)
        _softmax_init(plain, m_ref, acc_ref)

        def attn_body(j, c):
            ks = pl.multiple_of(j * tk, tk)
            s = _dot_nt(q, k_ref[0, pl.ds(ks, tk), :]).reshape(nh, tq, tk)
            sel = score_ref[:, pl.ds(ks, tk)] >= thr
            s = jnp.where(sel[None], s, NEG_BIG).reshape(nh * tq, tk)
            _softmax_step(plain, s, lat_ref[0, pl.ds(ks, tk), :], m_ref, acc_ref)
            return c

        lax.fori_loop(0, n_tiles, attn_body, 0)

    pl.when(plain_ref[0] == 1)(lambda: attend(True))
    pl.when(plain_ref[0] != 1)(lambda: attend(False))

    o = _softmax_result(acc_ref, B_LATENT).astype(BF16)
    for h in range(nh):
        oh = jnp.dot(o[h * tq:(h + 1) * tq], wuv_ref[h], preferred_element_type=F32)
        o_ref[0, :, h * HEAD_DIM:(h + 1) * HEAD_DIM] = oh.astype(BF16)


def _dsa_attention(plain, iq, iw, ik, q, k, lat1, w_uv):
    B, nh, S, dq = q.shape
    dh = HEAD_DIM
    tq = DSA_TQ
    topk = min(IDX_TOPK_MAX, S // 4)
    col = np.arange(DSA_TK)
    tri = jnp.asarray(col[:, None] <= col[None, :], BF16)
    hspec = lambda w: pl.BlockSpec((1, nh, tq, w), lambda b, i, p: (b, 0, i, 0))
    kspec = lambda w: pl.BlockSpec((1, S, w), lambda b, i, p: (b, 0, 0))
    grid_spec = pltpu.PrefetchScalarGridSpec(
        num_scalar_prefetch=1,
        grid=(B, S // tq),
        in_specs=[hspec(dh),
                  pl.BlockSpec((1, tq, IDX_HEADS), lambda b, i, p: (b, i, 0)),
                  kspec(dh), hspec(dq), kspec(dq), kspec(2 * B_LATENT),
                  pl.BlockSpec((nh, B_LATENT, dh), lambda b, i, p: (0, 0, 0)),
                  pl.BlockSpec((DSA_TK, DSA_TK), lambda b, i, p: (0, 0))],
        out_specs=pl.BlockSpec((1, tq, nh * dh), lambda b, i, p: (b, i, 0)),
        scratch_shapes=[pltpu.VMEM((tq, S), F32),
                        pltpu.VMEM((S, tq), F32),
                        pltpu.VMEM((tq, 1), F32),
                        pltpu.VMEM((nh * tq, 1), F32),
                        pltpu.VMEM((nh * tq, 2 * B_LATENT), F32)],
    )
    return pl.pallas_call(
        functools.partial(_dsa_kernel, topk=topk),
        grid_spec=grid_spec,
        out_shape=jax.ShapeDtypeStruct((B, S, nh * dh), BF16),
        compiler_params=_cparams(2),
    )(plain, iq, iw, ik, q, k, lat1, w_uv.astype(BF16), tri)


def _out_kernel(a_ref, b_ref, x_ref, g1_ref, sh_ref, sc_ref, ng_ref, woa_ref, wob_ref, rw_ref,
                rb_ref, tri_ref, x1_ref, h2_ref, idx_ref, gate_ref, rank_ref, cnt_ref, carry_ref):
    first = jnp.logical_and(pl.program_id(0) == 0, pl.program_id(1) == 0)

    @pl.when(first)
    def _():
        carry_ref[...] = jnp.zeros(carry_ref.shape, F32)

    mix = (jnp.dot(a_ref[0], woa_ref[...], preferred_element_type=F32)
           + jnp.dot(b_ref[0], wob_ref[...], preferred_element_type=F32))
    x1 = x_ref[0] + g1_ref[0] * mix
    x1_ref[0] = x1
    ms = jnp.mean(x1 * x1, axis=-1, keepdims=True)
    h2 = x1 * lax.rsqrt(ms + RMS_EPS) * ng_ref[...]
    h2 = h2 * (1.0 + sc_ref[0]) + sh_ref[0]
    h2_ref[0] = h2

    logits = jnp.dot(h2, rw_ref[...], preferred_element_type=F32,
                     precision=lax.Precision.HIGHEST) + rb_ref[...]
    ts = logits.shape[0]
    lane_i = lax.broadcasted_iota(I32, (ts, LANES), 1)
    lane = lane_i.astype(F32)
    neg_inf = jnp.float32(-jnp.inf)
    l = jnp.where(lane_i < N_EXPERTS, logits, neg_inf)
    vals, idxs = [], []
    for _ in range(TOP_K):
        m = jnp.max(l, axis=-1, keepdims=True)
        idx = jnp.min(jnp.where(l == m, lane, float(LANES)), axis=-1, keepdims=True)
        vals.append(m)
        idxs.append(idx)
        l = jnp.where(lane == idx, neg_inf, l)
    es = [jnp.exp(v - vals[0]) for v in vals]
    denom = es[0] + es[1] + es[2] + es[3]

    onehot = jnp.zeros((ts, LANES), F32)
    for idx in idxs:
        onehot = onehot + jnp.where(lane == idx, 1.0, 0.0)
    prefix = jnp.dot(tri_ref[...], onehot.astype(BF16), preferred_element_type=F32) + carry_ref[...]
    idx_out = jnp.zeros((ts, LANES), I32)
    gate_out = jnp.zeros((ts, LANES), F32)
    rank_out = jnp.zeros((ts, LANES), I32)
    for k in range(TOP_K):
        rank = jnp.sum(jnp.where(lane == idxs[k], prefix, 0.0), axis=-1, keepdims=True)
        idx_out = jnp.where(lane_i == k, idxs[k].astype(I32), idx_out)
        gate_out = jnp.where(lane_i == k, es[k] / denom, gate_out)
        rank_out = jnp.where(lane_i == k, rank.astype(I32), rank_out)
    idx_ref[0] = idx_out
    gate_ref[0] = gate_out
    rank_ref[0] = rank_out
    carry = carry_ref[...] + jnp.sum(onehot, axis=0, keepdims=True)
    carry_ref[...] = carry
    cnt_ref[...] = carry


def _out_and_route(a_out, b_out, x, g1, sh2, sc2, norm_g, w_out, router_w, router_b):
    B, S, D = x.shape
    ts = OUT_ROWS
    aw = a_out.shape[-1]
    w_bf = w_out.astype(BF16)
    rw = jnp.concatenate([router_w, jnp.zeros((D, LANES - N_EXPERTS), F32)], axis=1)
    rb = jnp.concatenate([router_b, jnp.zeros((LANES - N_EXPERTS,), F32)]).reshape(1, LANES)
    r = np.arange(ts)
    tri = jnp.asarray(r[:, None] > r[None, :], BF16)
    row = lambda b, i: (b, 0, 0)
    full = lambda b, i: (0, 0)
    tok = lambda w: pl.BlockSpec((1, ts, w), lambda b, i: (b, i, 0))
    return pl.pallas_call(
        _out_kernel,
        grid=(B, S // ts),
        in_specs=[tok(aw), tok(D - aw), tok(D),
                  pl.BlockSpec((1, 1, D), row), pl.BlockSpec((1, 1, D), row),
                  pl.BlockSpec((1, 1, D), row), pl.BlockSpec((1, D), full),
                  pl.BlockSpec((aw, D), full), pl.BlockSpec((D - aw, D), full),
                  pl.BlockSpec((D, LANES), full), pl.BlockSpec((1, LANES), full),
                  pl.BlockSpec((ts, ts), full)],
        out_specs=[tok(D), tok(D), tok(LANES), tok(LANES), tok(LANES),
                   pl.BlockSpec((1, LANES), full)],
        out_shape=[jax.ShapeDtypeStruct((B, S, D), F32), jax.ShapeDtypeStruct((B, S, D), F32),
                   jax.ShapeDtypeStruct((B, S, LANES), I32),
                   jax.ShapeDtypeStruct((B, S, LANES), F32),
                   jax.ShapeDtypeStruct((B, S, LANES), I32),
                   jax.ShapeDtypeStruct((1, LANES), F32)],
        scratch_shapes=[pltpu.VMEM((1, LANES), F32)],
        compiler_params=_cparams(2),
    )(a_out, b_out, x, g1, sh2, sc2, norm_g.reshape(1, D), w_bf[:aw], w_bf[aw:], rw, rb, tri)


def _row_copy_wait(src_ref, dst_ref, sem, rows):
    pltpu.make_async_copy(src_ref.at[pl.ds(0, rows), :], dst_ref.at[pl.ds(0, rows), :], sem).wait()


def _dispatch_kernel(dest_ref, h_ref, hall_ref, xin_ref, xbuf_ref, sem):
    del xin_ref
    rows = h_ref.shape[0]
    row0 = pl.program_id(0) * rows

    def body(r, c):
        for k in range(TOP_K):
            d = dest_ref[0, 0, r * TOP_K + k]
            src = h_ref.at[pl.ds(r, 1), :] if k % 2 == 0 else hall_ref.at[pl.ds(row0 + r, 1), :]
            pltpu.make_async_copy(src, xbuf_ref.at[pl.ds(d, 1), :], sem).start()
        return c

    lax.fori_loop(0, rows, body, 0)
    for _ in range(TOP_K):
        _row_copy_wait(h_ref, xbuf_ref, sem, rows)


def _dispatch(h2, dest, n_rows):
    T, D = h2.shape
    ts = min(DISPATCH_ROWS, T)
    xbuf0 = jnp.zeros((n_rows, D), F32)
    return pl.pallas_call(
        _dispatch_kernel,
        grid=(T // ts,),
        in_specs=[pl.BlockSpec((1, 1, ts * TOP_K), lambda i: (i, 0, 0), memory_space=pltpu.SMEM),
                  pl.BlockSpec((ts, D), lambda i: (i, 0)),
                  pl.BlockSpec(memory_space=pl.ANY),
                  pl.BlockSpec(memory_space=pl.ANY)],
        out_specs=pl.BlockSpec(memory_space=pl.ANY),
        out_shape=jax.ShapeDtypeStruct((n_rows, D), F32),
        scratch_shapes=[pltpu.SemaphoreType.DMA(())],
        input_output_aliases={3: 0},
        compiler_params=_cparams(1),
    )(dest.reshape(T // ts, 1, ts * TOP_K), h2, h2, xbuf0)


def _moe_kernel(be_ref, nused_ref, x_ref, wg_ref, bg_ref, wu_ref, bu_ref, wd_ref, bd_ref, y_ref,
                wg_s, wu_s, wd_s):
    i = pl.program_id(0)
    used = i < nused_ref[0]
    new_expert = jnp.logical_or(i == 0, be_ref[i] != be_ref[jnp.maximum(i - 1, 0)])

    @pl.when(jnp.logical_and(used, new_expert))
    def _():
        wg_s[...] = wg_ref[0].astype(BF16)
        wu_s[...] = wu_ref[0].astype(BF16)
        wd_s[...] = wd_ref[0].astype(BF16)

    @pl.when(used)
    def _():
        xb = x_ref[...].astype(BF16)
        g = jnp.dot(xb, wg_s[...], preferred_element_type=F32) + bg_ref[0]
        u = jnp.dot(xb, wu_s[...], preferred_element_type=F32) + bu_ref[0]
        g = jnp.minimum(g, SWIGLU_LIMIT)
        u = jnp.clip(u, -SWIGLU_LIMIT, SWIGLU_LIMIT)
        a = g * (1.0 / (1.0 + jnp.exp(-SWIGLU_ALPHA * g))) * (u + 1.0)
        y_ref[...] = jnp.dot(a.astype(BF16), wd_s[...], preferred_element_type=F32) + bd_ref[0]

    @pl.when(jnp.logical_not(used))
    def _():
        y_ref[...] = jnp.zeros(y_ref.shape, F32)


def _moe_experts(xbuf, block_expert, n_used, w_gate, b_gate, w_up, b_up, w_down, b_down):
    R, D = xbuf.shape
    E, _, F = w_gate.shape
    G = MOE_ROWS
    nb = R // G
    last = lambda i, nu: jnp.maximum(jnp.minimum(i, nu[0] - 1), 0)
    blk = lambda i, be, nu: (last(i, nu), 0)
    wsel = lambda i, be, nu: (be[last(i, nu)], 0, 0)
    grid_spec = pltpu.PrefetchScalarGridSpec(
        num_scalar_prefetch=2,
        grid=(nb,),
        in_specs=[pl.BlockSpec((G, D), blk),
                  pl.BlockSpec((1, D, F), wsel), pl.BlockSpec((1, 1, F), wsel),
                  pl.BlockSpec((1, D, F), wsel), pl.BlockSpec((1, 1, F), wsel),
                  pl.BlockSpec((1, F, D), wsel), pl.BlockSpec((1, 1, D), wsel)],
        out_specs=pl.BlockSpec((G, D), lambda i, be, nu: (i, 0)),
        scratch_shapes=[pltpu.VMEM((D, F), BF16), pltpu.VMEM((D, F), BF16),
                        pltpu.VMEM((F, D), BF16)],
    )
    return pl.pallas_call(
        _moe_kernel,
        grid_spec=grid_spec,
        out_shape=jax.ShapeDtypeStruct((R, D), F32),
        compiler_params=_cparams(1),
    )(block_expert, n_used, xbuf, w_gate, b_gate.reshape(E, 1, F),
      w_up, b_up.reshape(E, 1, F), w_down, b_down.reshape(E, 1, D))


def _combine_kernel(dest_ref, gate_ref, x1_ref, g2_ref, ybuf_ref, o_ref, buf_ref, sem):
    rows = x1_ref.shape[0]

    def body(r, c):
        for k in range(TOP_K):
            d = dest_ref[0, 0, r * TOP_K + k]
            pltpu.make_async_copy(ybuf_ref.at[pl.ds(d, 1), :], buf_ref.at[k, pl.ds(r, 1), :],
                                  sem).start()
        return c

    lax.fori_loop(0, rows, body, 0)
    for k in range(TOP_K):
        _row_copy_wait(ybuf_ref, buf_ref.at[k], sem, rows)
    gate = gate_ref[...]
    y = buf_ref[0] * gate[:, 0:1]
    for k in range(1, TOP_K):
        y = y + buf_ref[k] * gate[:, k:k + 1]
    o_ref[...] = x1_ref[...] + g2_ref[0] * y


def _combine(ybuf, dest, gates, x1, g2, seq):
    T, D = x1.shape
    ts = COMBINE_ROWS
    per_seq = seq // ts
    return pl.pallas_call(
        _combine_kernel,
        grid=(T // ts,),
        in_specs=[pl.BlockSpec((1, 1, ts * TOP_K), lambda i: (i, 0, 0), memory_space=pltpu.SMEM),
                  pl.BlockSpec((ts, LANES), lambda i: (i, 0)),
                  pl.BlockSpec((ts, D), lambda i: (i, 0)),
                  pl.BlockSpec((1, 1, D), lambda i: (i // per_seq, 0, 0)),
                  pl.BlockSpec(memory_space=pl.ANY)],
        out_specs=pl.BlockSpec((ts, D), lambda i: (i, 0)),
        out_shape=jax.ShapeDtypeStruct((T, D), F32),
        scratch_shapes=[pltpu.VMEM((TOP_K, ts, D), F32), pltpu.SemaphoreType.DMA(())],
        compiler_params=_cparams(1),
    )(dest.reshape(T // ts, 1, ts * TOP_K), gates, x1, g2, ybuf)


def _layer(x, mod, lambda_init, norm1_g, norm2_g, w_in, w_out, a_q_norm_g, a_k_norm_g, a_lambda,
           a_sub_g, b_q_norm_g, b_k_norm_g, b_kv_norm_g, b_w_uv, router_w, router_b, w_gate,
           b_gate, w_up, b_up, w_down, b_down):
    B, S, D = x.shape
    T = B * S
    sh1, sc1, g1, sh2, sc2, g2 = [m.reshape(B, 1, D) for m in jnp.split(mod, 6, axis=-1)]

    (aq1, aq2, ak1, ak2, bq, bk, ik, iq, blat1, av1, iw), plain_a, plain_b = _project(
        x, sh1, sc1, norm1_g, w_in, a_q_norm_g, a_k_norm_g, b_q_norm_g, b_k_norm_g, b_kv_norm_g)
    a_out = _diff_attention(plain_a, aq1, aq2, ak1, ak2, av1, a_lambda, a_sub_g, lambda_init)
    b_out = _dsa_attention(plain_b, iq, iw, ik, bq, bk, blat1, b_w_uv)

    x1, h2, top_idx, gates, rank, counts = _out_and_route(
        a_out, b_out, x, g1, sh2, sc2, norm2_g, w_out, router_w, router_b)

    G = MOE_ROWS
    counts = counts[0, :N_EXPERTS].astype(I32)
    padded = ((counts + G - 1) // G) * G
    pcum = jnp.cumsum(padded)
    poff = pcum - padded
    nb = (T * TOP_K) // G + N_EXPERTS
    starts = jnp.arange(nb, dtype=I32) * G
    block_expert = jnp.minimum(
        jnp.sum((pcum[None, :] <= starts[:, None]).astype(I32), axis=1), N_EXPERTS - 1)
    n_used = (pcum[-1:] // G).astype(I32)
    top_idx = top_idx.reshape(T, LANES)[:, :TOP_K]
    dest = (poff[top_idx] + rank.reshape(T, LANES)[:, :TOP_K]).astype(I32).reshape(T * TOP_K)

    xbuf = _dispatch(h2.reshape(T, D), dest, nb * G)
    ybuf = _moe_experts(xbuf, block_expert.astype(I32), n_used, w_gate, b_gate, w_up, b_up,
                        w_down, b_down)
    out = _combine(ybuf, dest, gates.reshape(T, LANES), x1.reshape(T, D), g2, S)
    return out.reshape(B, S, D)


def kernel(x, c, norm1_g, norm2_g, w_ada, b_ada, w_in, w_out, a_q_norm_g, a_k_norm_g, a_lambda,
           a_sub_g, b_q_norm_g, b_k_norm_g, b_kv_norm_g, b_w_uv, router_w, router_b, w_gate,
           b_gate, w_up, b_up, w_down, b_down):
    depth = w_in.shape[0]
    for l in range(depth):
        lambda_init = 0.8 - 0.6 * math.exp(-0.3 * l)
        mod = _ada(c, w_ada[l], b_ada[l])
        x = _layer(x, mod, lambda_init, norm1_g[l], norm2_g[l], w_in[l], w_out[l], a_q_norm_g[l],
                   a_k_norm_g[l], a_lambda[l], a_sub_g[l], b_q_norm_g[l], b_k_norm_g[l],
                   b_kv_norm_g[l], b_w_uv[l], router_w[l], router_b[l], w_gate[l], b_gate[l],
                   w_up[l], b_up[l], w_down[l], b_down[l])
    return x
```

```python
import functools
import math

import numpy as np
import jax
import jax.numpy as jnp
from jax import lax
from jax.experimental import pallas as pl
from jax.experimental.pallas import tpu as pltpu

F32 = jnp.float32
BF16 = jnp.bfloat16
I32 = jnp.int32

CHUNK = 64
HEAD_DIM = 64
ROT_DIM = HEAD_DIM // 4
ROPE_THETA = 500000.0
RMS_EPS = 1e-6
A_HEADS = 4
A_V_DIM = 2 * HEAD_DIM
B_HEADS = 8
B_LATENT = 128
IDX_HEADS = 8
IDX_TOPK_MAX = 256
N_EXPERTS = 32
TOP_K = 4
SWIGLU_LIMIT = 7.0
SWIGLU_ALPHA = 1.702

LANES = 128
INT_MIN = -(2 ** 31)
NEG_BIG = -1e30
VMEM_LIMIT = 56 * 1024 * 1024

NORM_SLACK = 1.01
PLAIN_EXP_MAX_BOUND = 30.0

PROJ_ROWS = 512
ATT_TQ = 512
DSA_TQ = 256
DSA_TK = 512
COUNT_ROWS = 128
COUNT_CHAINS = 8
WALK_UNROLL = 4
OUT_ROWS = 512
MOE_ROWS = 512
DISPATCH_ROWS = 1024
COMBINE_ROWS = 512


def _cparams(n_axes):
    return pltpu.CompilerParams(
        dimension_semantics=("arbitrary",) * n_axes, vmem_limit_bytes=VMEM_LIMIT)


def _dot_nt(a, b):
    return lax.dot_general(a, b, (((1,), (1,)), ((), ())), preferred_element_type=F32)


def _ada_kernel(c_ref, w_ref, b_ref, o_ref):
    c = c_ref[...]
    sc = c / (1.0 + jnp.exp(-c))
    o_ref[...] = jnp.dot(sc, w_ref[...], preferred_element_type=F32,
                         precision=lax.Precision.HIGHEST) + b_ref[...]


def _ada(c, w, b):
    B, D = c.shape
    N = w.shape[1]
    return pl.pallas_call(
        _ada_kernel,
        grid=(N // D,),
        in_specs=[pl.BlockSpec((B, D), lambda j: (0, 0)),
                  pl.BlockSpec((D, D), lambda j: (0, j)),
                  pl.BlockSpec((1, D), lambda j: (0, j))],
        out_specs=pl.BlockSpec((B, D), lambda j: (0, j)),
        out_shape=jax.ShapeDtypeStruct((B, N), F32),
        compiler_params=_cparams(1),
    )(c, w, b.reshape(1, N))


C_QK = 0
C_BQ = 1024
C_BKIK = 1536
C_IQ = 1664
C_LAT = 2176
C_AV = 2304
C_IW = 2816
C_END = 2944
N_GAIN = C_IQ


def _group_sumsq(p, bd):
    sq = p * p
    hi = sq.astype(BF16)
    lo = (sq - hi.astype(F32)).astype(BF16)
    return (jnp.dot(hi, bd, preferred_element_type=F32)
            + jnp.dot(lo, bd, preferred_element_type=F32))


def _rope(y, c, s1, s2):
    w = y.shape[1]
    return y * c + pltpu.roll(y, w - ROT_DIM // 2, 1) * s1 + pltpu.roll(y, ROT_DIM // 2, 1) * s2


def _proj_kernel(x_ref, sh_ref, sc_ref, g_ref, w_ref, gain_ref, kb_ref, latg_ref, rc_ref, rs1_ref,
                 rs2_ref, bd_ref,
                 aq1_ref, aq2_ref, ak1_ref, ak2_ref, bq_ref, bk_ref, ik_ref, iq_ref, lat_ref,
                 av_ref, iw_ref):
    x = x_ref[0]
    ms = jnp.mean(x * x, axis=-1, keepdims=True)
    h = x * lax.rsqrt(ms + RMS_EPS) * g_ref[...]
    h = (h * (1.0 + sc_ref[0]) + sh_ref[0]).astype(BF16)

    rc, rs1, rs2 = rc_ref[...], rs1_ref[...], rs2_ref[...]
    bd = bd_ref[...]
    ts = x.shape[0]
    lane = lax.broadcasted_iota(I32, (ts, LANES), 1)
    ones = jnp.ones((ts, LANES), F32)

    def proj(c0, width):
        return jnp.dot(h, w_ref[:, c0:c0 + width], preferred_element_type=F32)

    def normed(p, c0):
        width = p.shape[1]
        ss = _group_sumsq(p, bd[:width, :width])
        return p * lax.rsqrt(ss * (1.0 / HEAD_DIM) + RMS_EPS) * gain_ref[:, c0:c0 + width]

    def store_slots(ref, y, extra, first_head=0):
        for pair in range(y.shape[1] // LANES):
            z = y[:, pair * LANES:(pair + 1) * LANES]
            e = extra[:, pair * LANES:(pair + 1) * LANES]
            even = jnp.where(lane < HEAD_DIM, z,
                             jnp.where(lane == HEAD_DIM, pltpu.roll(e, HEAD_DIM, 1), 0.0))
            odd = jnp.where(lane < HEAD_DIM, pltpu.roll(z, HEAD_DIM, 1),
                            jnp.where(lane == HEAD_DIM, e, 0.0))
            ref[0, first_head + 2 * pair] = even.astype(BF16)
            ref[0, first_head + 2 * pair + 1] = odd.astype(BF16)

    def query(c0):
        y = _rope(normed(proj(c0, 256), c0), rc, rs1, rs2)
        norm = jnp.sqrt(jnp.dot((y * y).astype(BF16), bd, preferred_element_type=F32))
        return y, -norm * kb_ref[:, c0:c0 + 256]

    def key(c0):
        return _rope(normed(proj(c0, 256), c0), rc, rs1, rs2), jnp.ones((ts, 256), F32)

    store_slots(aq1_ref, *query(0))
    store_slots(aq2_ref, *query(256))
    store_slots(ak1_ref, *key(512))
    store_slots(ak2_ref, *key(768))
    for half in range(2):
        store_slots(bq_ref, *query(C_BQ + 256 * half), first_head=4 * half)
        y = _rope(proj(C_IQ + 256 * half, 256), rc, rs1, rs2)
        for j in range(4):
            iq_ref[0, 4 * half + j] = y[:, j * HEAD_DIM:(j + 1) * HEAD_DIM].astype(BF16)

    p = proj(C_BKIK, LANES)
    y = _rope(jnp.where(lane < HEAD_DIM, normed(p, C_BKIK), p),
              rc[:, :LANES], rs1[:, :LANES], rs2[:, :LANES])
    bk_ref[0] = jnp.where(lane < HEAD_DIM, y, jnp.where(lane == HEAD_DIM, 1.0, 0.0)).astype(BF16)
    ik_ref[0] = y[:, HEAD_DIM:].astype(BF16)

    p = proj(C_LAT, LANES)
    ms = jnp.mean(p * p, axis=-1, keepdims=True)
    lat = p * lax.rsqrt(ms + RMS_EPS) * latg_ref[...]
    lat_ref[0] = jnp.concatenate([lat, ones], axis=1).astype(BF16)

    for j in range(A_HEADS):
        av_ref[0, j] = jnp.concatenate([proj(C_AV + j * A_V_DIM, A_V_DIM), ones],
                                       axis=1).astype(BF16)

    p = proj(C_IW, LANES)
    iw_ref[0] = p[:, :IDX_HEADS] * (IDX_HEADS ** -0.5 * HEAD_DIM ** -0.5)


def _rope_tables(S, width):
    half = ROT_DIM // 2
    pos = jnp.arange(S, dtype=F32)
    inv = ROPE_THETA ** (-jnp.arange(0, ROT_DIM, 2, dtype=F32) / ROT_DIM)
    ang = pos[:, None] * inv[None, :]
    cos, sin = jnp.cos(ang), jnp.sin(ang)
    zeros = jnp.zeros((S, HEAD_DIM - ROT_DIM), F32)
    c = jnp.concatenate([cos, cos, zeros + 1.0], axis=1)
    s1 = jnp.concatenate([-sin, jnp.zeros((S, half), F32), zeros], axis=1)
    s2 = jnp.concatenate([jnp.zeros((S, half), F32), sin, zeros], axis=1)
    reps = width // HEAD_DIM
    return tuple(jnp.tile(t, (1, reps)) for t in (c, s1, s2))


def _project(x, sh1, sc1, norm_g, w_in, a_q_g, a_k_g, b_q_g, b_k_g, b_kv_g):
    B, S, D = x.shape
    ts = PROJ_ROWS
    sizes = (256, 256, 256, 256, 512, 512, 64, 128, 512, 64, 8)
    offs = np.concatenate([[0], np.cumsum(sizes)])
    seg = lambda i: w_in[:, offs[i]:offs[i + 1]]
    w_p = jnp.concatenate(
        [seg(0), seg(1), seg(2), seg(3), seg(5), seg(6), seg(9), seg(8), seg(7), seg(4), seg(10),
         jnp.zeros((D, C_END - C_IW - IDX_HEADS), F32)], axis=1).astype(BF16)
    scale = HEAD_DIM ** -0.5
    gain = jnp.concatenate(
        [jnp.tile(a_q_g * scale, 2 * A_HEADS), jnp.tile(a_k_g, 2 * A_HEADS),
         jnp.tile(b_q_g * scale, B_HEADS), b_k_g, jnp.ones((HEAD_DIM,), F32)]).reshape(1, N_GAIN)
    kb_a = 8.0 * NORM_SLACK * jnp.max(jnp.abs(a_k_g))
    kb_b = 8.0 * NORM_SLACK * jnp.max(jnp.abs(b_k_g))
    zeros = lambda n: jnp.zeros((n,), F32)
    kb = jnp.concatenate([zeros(512) + kb_a, zeros(512), zeros(512) + kb_b,
                          zeros(N_GAIN - C_BKIK)]).reshape(1, N_GAIN)
    bound_a = jnp.max(jnp.abs(a_q_g)) * kb_a
    bound_b = jnp.max(jnp.abs(b_q_g)) * kb_b
    rc, rs1, rs2 = _rope_tables(S, 256)
    gid = np.arange(256) // HEAD_DIM
    bd = jnp.asarray(gid[:, None] == gid[None, :], BF16)

    row = lambda b, i: (b, 0, 0)
    full = lambda b, i: (0, 0)
    heads = lambda n, w: pl.BlockSpec((1, n, ts, w), lambda b, i: (b, 0, i, 0))
    flat = lambda w: pl.BlockSpec((1, ts, w), lambda b, i: (b, i, 0))
    hshape = lambda n, w: jax.ShapeDtypeStruct((B, n, S, w), BF16)
    outs = pl.pallas_call(
        _proj_kernel,
        grid=(B, S // ts),
        in_specs=[pl.BlockSpec((1, ts, D), lambda b, i: (b, i, 0)),
                  pl.BlockSpec((1, 1, D), row), pl.BlockSpec((1, 1, D), row),
                  pl.BlockSpec((1, D), full),
                  pl.BlockSpec((D, C_END), full),
                  pl.BlockSpec((1, N_GAIN), full),
                  pl.BlockSpec((1, N_GAIN), full),
                  pl.BlockSpec((1, LANES), full),
                  pl.BlockSpec((ts, 256), lambda b, i: (i, 0)),
                  pl.BlockSpec((ts, 256), lambda b, i: (i, 0)),
                  pl.BlockSpec((ts, 256), lambda b, i: (i, 0)),
                  pl.BlockSpec((256, 256), full)],
        out_specs=[heads(A_HEADS, LANES)] * 4
        + [heads(B_HEADS, LANES), flat(LANES), flat(HEAD_DIM), heads(IDX_HEADS, HEAD_DIM),
           flat(2 * B_LATENT), heads(A_HEADS, 2 * A_V_DIM), flat(IDX_HEADS)],
        out_shape=[hshape(A_HEADS, LANES)] * 4
        + [hshape(B_HEADS, LANES), jax.ShapeDtypeStruct((B, S, LANES), BF16),
           jax.ShapeDtypeStruct((B, S, HEAD_DIM), BF16), hshape(IDX_HEADS, HEAD_DIM),
           jax.ShapeDtypeStruct((B, S, 2 * B_LATENT), BF16), hshape(A_HEADS, 2 * A_V_DIM),
           jax.ShapeDtypeStruct((B, S, IDX_HEADS), F32)],
        compiler_params=_cparams(2),
    )(x, sh1, sc1, norm_g.reshape(1, D), w_p, gain, kb, b_kv_g.reshape(1, B_LATENT), rc, rs1, rs2,
      bd)
    plain_a = (bound_a <= PLAIN_EXP_MAX_BOUND).astype(I32).reshape(1)
    plain_b = (bound_b <= PLAIN_EXP_MAX_BOUND).astype(I32).reshape(1)
    return outs, plain_a, plain_b


def _softmax_init(plain, m_ref, acc_ref):
    acc_ref[...] = jnp.zeros(acc_ref.shape, F32)
    if not plain:
        m_ref[...] = jnp.full(m_ref.shape, NEG_BIG, F32)


def _softmax_step(plain, s, v1, m_ref, acc_ref):
    if plain:
        acc_ref[...] += jnp.dot(jnp.exp(s).astype(BF16), v1, preferred_element_type=F32)
        return
    m_old = m_ref[...]
    m_new = jnp.maximum(m_old, jnp.max(s, axis=-1, keepdims=True))
    p = jnp.exp(s - m_new).astype(BF16)
    acc_ref[...] = (jnp.exp(m_old - m_new) * acc_ref[...]
                    + jnp.dot(p, v1, preferred_element_type=F32))
    m_ref[...] = m_new


def _softmax_result(acc_ref, width):
    acc = acc_ref[...]
    return acc[:, :width] / acc[:, width:width + 1]


def _chunk_mask(q0, k0, tq, tk):
    qc = (q0 + lax.broadcasted_iota(I32, (tq, tk), 0)) // CHUNK
    kc = (k0 + lax.broadcasted_iota(I32, (tq, tk), 1)) // CHUNK
    return kc <= qc


def _diff_attn_kernel(plain_ref, lam_ref, subg_ref, diag_ref, q1_ref, q2_ref, k1_ref, k2_ref, v_ref,
                      o_ref, m1_ref, acc1_ref, m2_ref, acc2_ref, *, lambda_init):
    tq = tk = ATT_TQ
    i = pl.program_id(2)
    n_tiles = i + 1

    def attend(plain):
        q1 = q1_ref[0, 0]
        q2 = q2_ref[0, 0]
        _softmax_init(plain, m1_ref, acc1_ref)
        _softmax_init(plain, m2_ref, acc2_ref)

        def tile(j, masked):
            ks = pl.multiple_of(j * tk, tk)
            v1 = v_ref[0, 0, pl.ds(ks, tk), :]
            s1 = _dot_nt(q1, k1_ref[0, 0, pl.ds(ks, tk), :])
            s2 = _dot_nt(q2, k2_ref[0, 0, pl.ds(ks, tk), :])
            if masked:
                s1 = s1 + diag_ref[...]
                s2 = s2 + diag_ref[...]
            _softmax_step(plain, s1, v1, m1_ref, acc1_ref)
            _softmax_step(plain, s2, v1, m2_ref, acc2_ref)

        def body(j, c):
            tile(j, False)
            return c

        lax.fori_loop(0, n_tiles - 1, body, 0)
        tile(n_tiles - 1, True)

    pl.when(plain_ref[0] == 1)(lambda: attend(True))
    pl.when(plain_ref[0] != 1)(lambda: attend(False))

    lv = lam_ref[...]
    lam = (jnp.exp(jnp.sum(lv[0:1] * lv[1:2], axis=-1, keepdims=True))
           - jnp.exp(jnp.sum(lv[2:3] * lv[3:4], axis=-1, keepdims=True)) + lambda_init)
    o = _softmax_result(acc1_ref, A_V_DIM) - lam * _softmax_result(acc2_ref, A_V_DIM)
    ms = jnp.mean(o * o, axis=-1, keepdims=True)
    o = o * lax.rsqrt(ms + RMS_EPS) * subg_ref[...] * (1.0 - lambda_init)
    o_ref[0] = o.astype(BF16)


def _diff_attention(plain, q1, q2, k1, k2, v1, a_lambda, sub_g, lambda_init):
    B, H, S, dq = q1.shape
    tq = ATT_TQ
    chunk = np.arange(tq) // CHUNK
    diag = jnp.asarray(np.where(chunk[None, :] <= chunk[:, None], 0.0, NEG_BIG), F32)
    qspec = pl.BlockSpec((1, 1, tq, dq), lambda b, h, i, p: (b, h, i, 0))
    kspec = pl.BlockSpec((1, 1, S, dq), lambda b, h, i, p: (b, h, 0, 0))
    col = lambda: pltpu.VMEM((tq, 1), F32)
    acc = lambda: pltpu.VMEM((tq, 2 * A_V_DIM), F32)
    grid_spec = pltpu.PrefetchScalarGridSpec(
        num_scalar_prefetch=1,
        grid=(B, H, S // tq),
        in_specs=[pl.BlockSpec((4, HEAD_DIM), lambda b, h, i, p: (0, 0)),
                  pl.BlockSpec((1, A_V_DIM), lambda b, h, i, p: (0, 0)),
                  pl.BlockSpec((tq, tq), lambda b, h, i, p: (0, 0)),
                  qspec, qspec, kspec, kspec,
                  pl.BlockSpec((1, 1, S, 2 * A_V_DIM), lambda b, h, i, p: (b, h, 0, 0))],
        out_specs=pl.BlockSpec((1, tq, A_V_DIM), lambda b, h, i, p: (b, i, h)),
        scratch_shapes=[col(), acc(), col(), acc()],
    )
    return pl.pallas_call(
        functools.partial(_diff_attn_kernel, lambda_init=lambda_init),
        grid_spec=grid_spec,
        out_shape=jax.ShapeDtypeStruct((B, S, H * A_V_DIM), BF16),
        compiler_params=_cparams(3),
    )(plain, a_lambda, sub_g.reshape(1, A_V_DIM), diag, q1, q2, k1, k2, v1)


def _dsa_kernel(plain_ref, iq_ref, iw_ref, ik_ref, q_ref, k_ref, lat_ref, wuv_ref, tri_ref, o_ref,
                score_ref, score_t_ref, thr_ref, m_ref, acc_ref, *, topk):
    tq, tk = DSA_TQ, DSA_TK
    nh = B_HEADS
    i = pl.program_id(1)
    n_tiles = ((i + 1) * tq + tk - 1) // tk

    iq = iq_ref[0].reshape(IDX_HEADS * tq, HEAD_DIM)
    iw = iw_ref[0]

    def score_tile(j, masked):
        ks = pl.multiple_of(j * tk, tk)
        rel = jnp.maximum(_dot_nt(iq, ik_ref[0, pl.ds(ks, tk), :]), 0.0)
        rel = rel.reshape(IDX_HEADS, tq, tk)
        score = rel[0] * iw[:, 0:1]
        for h in range(1, IDX_HEADS):
            score = score + rel[h] * iw[:, h:h + 1]
        if masked:
            score = jnp.where(_chunk_mask(i * tq, ks, tq, tk), score, -jnp.inf)
        score_ref[:, pl.ds(ks, tk)] = score
        score_t_ref[pl.ds(ks, tk), :] = score.T

    def score_body(j, c):
        score_tile(j, False)
        return c

    lax.fori_loop(0, n_tiles - 1, score_body, 0)
    score_tile(n_tiles - 1, True)

    def image_to_float(t):
        return pltpu.bitcast(jnp.where(t < 0, t ^ 0x7FFFFFFF, t), F32)

    def count(pred, cand):
        parts = []
        for r0 in range(0, tq, COUNT_ROWS):
            rows = pl.ds(r0, COUNT_ROWS)
            cand_r = cand[r0:r0 + COUNT_ROWS]

            def body(j, acc, rows=rows, cand_r=cand_r):
                ks = pl.multiple_of(j * tk, tk)
                hit = jnp.where(pred(score_ref[rows, pl.ds(ks, tk)], cand_r), 1.0, 0.0)
                for c in range(tk // LANES):
                    acc = acc + hit[:, c * LANES:(c + 1) * LANES]
                return acc
            parts.append(lax.fori_loop(0, n_tiles, body, jnp.zeros((COUNT_ROWS, LANES), F32)))
        acc = jnp.concatenate(parts, axis=0)
        return jnp.sum(acc, axis=-1, keepdims=True)

    ge = lambda s, c: s >= c
    gt = lambda s, c: s > c

    def all_of(flags):
        return (jnp.min(flags) > 0.0).astype(I32)

    def count_t(pred, cand):
        def body(j, acc):
            ks = pl.multiple_of(j * tq, tq)
            hit = jnp.where(pred(score_t_ref[pl.ds(ks, tq), :], cand), 1.0, 0.0)
            return acc + jnp.sum(hit.reshape(-1, COUNT_CHAINS, 8, tq), axis=0)
        acc = lax.fori_loop(0, i + 1, body, jnp.zeros((COUNT_CHAINS, 8, tq), F32))
        return jnp.sum(jnp.sum(acc, axis=0), axis=0, keepdims=True)

    zero = jnp.zeros((1, tq), F32)
    n_ge0 = count_t(ge, zero)
    tie0 = jnp.where(jnp.logical_and(count_t(gt, zero) < topk, n_ge0 > topk), 1.0, 0.0)
    t0 = jnp.where(n_ge0 >= topk, 0, INT_MIN)
    settled0 = jnp.where(n_ge0 == topk, 1.0, tie0)

    def bit_body(state):
        step, t, settled, _ = state
        for _ in range(WALK_UNROLL):
            bit = jnp.where(step < 32, jnp.left_shift(jnp.int32(1), jnp.maximum(31 - step, 0)), 0)
            cand = t + bit
            n_ge = count_t(ge, image_to_float(cand))
            t = jnp.where(settled > 0.0, t, jnp.where(n_ge >= topk, cand, t))
            settled = jnp.where(n_ge == topk, 1.0, settled)
            step = step + 1
        return step, t, settled, all_of(settled)

    def bit_cond(state):
        step, _, _, all_settled = state
        return jnp.logical_and(step < 32, all_settled == 0)

    _, t, _, all_settled = lax.while_loop(
        bit_cond, bit_body, (jnp.int32(1), t0, settled0, all_of(settled0)))
    thr_row = jnp.where(t == INT_MIN, jnp.finfo(F32).min, image_to_float(t))
    thr_ref[...] = jnp.broadcast_to(thr_row, (LANES, tq)).T[:, 0:1]

    def min_where(pred, cand):
        def body(j, acc):
            ks = pl.multiple_of(j * tk, tk)
            s = score_ref[:, pl.ds(ks, tk)]
            s = jnp.where(pred(s, cand), s, jnp.inf)
            for c in range(tk // LANES):
                acc = jnp.minimum(acc, s[:, c * LANES:(c + 1) * LANES])
            return acc
        acc = lax.fori_loop(0, n_tiles, body, jnp.full((tq, LANES), jnp.inf, F32))
        return jnp.min(acc, axis=-1, keepdims=True)

    @pl.when(all_settled == 0)
    def _():
        thr0 = thr_ref[...]
        low = min_where(ge, thr0)
        nxt = min_where(gt, low)
        thr_ref[...] = jnp.where(count(ge, nxt) >= topk, nxt,
                                 jnp.where(low < jnp.inf, low, thr0))

    @pl.when(jnp.logical_or(all_settled == 0, jnp.max(tie0) > 0.0))
    def _():
        thr = thr_ref[...]
        quota = topk - count(gt, thr)

        def tie_body(j, carry):
            sl = pl.ds(pl.multiple_of(j * tk, tk), tk)
            score = score_ref[:, sl]
            eq = score == thr
            prefix = carry + jnp.dot(jnp.where(eq, 1.0, 0.0).astype(BF16), tri_ref[...],
                                     preferred_element_type=F32)
            score_ref[:, sl] = jnp.where(eq & (prefix > quota), -jnp.inf, score)
            return prefix[:, tk - 1:tk]

        lax.fori_loop(0, n_tiles, tie_body, jnp.zeros((tq, 1), F32))

    thr = thr_ref[...]

    def attend(plain):
        q = q_ref[0].reshape(nh * tq, LANES)
        _softmax_init(plain, m_ref, acc_ref)

        def attn_body(j, c):
            ks = pl.multiple_of(j * tk, tk)
            s = _dot_nt(q, k_ref[0, pl.ds(ks, tk), :]).reshape(nh, tq, tk)
            sel = score_ref[:, pl.ds(ks, tk)] >= thr
            s = jnp.where(sel[None], s, NEG_BIG).reshape(nh * tq, tk)
            _softmax_step(plain, s, lat_ref[0, pl.ds(ks, tk), :], m_ref, acc_ref)
            return c

        lax.fori_loop(0, n_tiles, attn_body, 0)

    pl.when(plain_ref[0] == 1)(lambda: attend(True))
    pl.when(plain_ref[0] != 1)(lambda: attend(False))

    o = _softmax_result(acc_ref, B_LATENT).astype(BF16)
    for h in range(nh):
        oh = jnp.dot(o[h * tq:(h + 1) * tq], wuv_ref[h], preferred_element_type=F32)
        o_ref[0, :, h * HEAD_DIM:(h + 1) * HEAD_DIM] = oh.astype(BF16)


def _dsa_attention(plain, iq, iw, ik, q, k, lat1, w_uv):
    B, nh, S, dq = q.shape
    dh = HEAD_DIM
    tq = DSA_TQ
    topk = min(IDX_TOPK_MAX, S // 4)
    col = np.arange(DSA_TK)
    tri = jnp.asarray(col[:, None] <= col[None, :], BF16)
    hspec = lambda w: pl.BlockSpec((1, nh, tq, w), lambda b, i, p: (b, 0, i, 0))
    kspec = lambda w: pl.BlockSpec((1, S, w), lambda b, i, p: (b, 0, 0))
    grid_spec = pltpu.PrefetchScalarGridSpec(
        num_scalar_prefetch=1,
        grid=(B, S // tq),
        in_specs=[hspec(dh),
                  pl.BlockSpec((1, tq, IDX_HEADS), lambda b, i, p: (b, i, 0)),
                  kspec(dh), hspec(dq), kspec(dq), kspec(2 * B_LATENT),
                  pl.BlockSpec((nh, B_LATENT, dh), lambda b, i, p: (0, 0, 0)),
                  pl.BlockSpec((DSA_TK, DSA_TK), lambda b, i, p: (0, 0))],
        out_specs=pl.BlockSpec((1, tq, nh * dh), lambda b, i, p: (b, i, 0)),
        scratch_shapes=[pltpu.VMEM((tq, S), F32),
                        pltpu.VMEM((S, tq), F32),
                        pltpu.VMEM((tq, 1), F32),
                        pltpu.VMEM((nh * tq, 1), F32),
                        pltpu.VMEM((nh * tq, 2 * B_LATENT), F32)],
    )
    return pl.pallas_call(
        functools.partial(_dsa_kernel, topk=topk),
        grid_spec=grid_spec,
        out_shape=jax.ShapeDtypeStruct((B, S, nh * dh), BF16),
        compiler_params=_cparams(2),
    )(plain, iq, iw, ik, q, k, lat1, w_uv.astype(BF16), tri)


def _out_kernel(a_ref, b_ref, x_ref, g1_ref, sh_ref, sc_ref, ng_ref, woa_ref, wob_ref, rwh_ref,
                rwl_ref, rb_ref, tri_ref, x1_ref, h2_ref, idx_ref, gate_ref, rank_ref, cnt_ref,
                carry_ref):
    first = jnp.logical_and(pl.program_id(0) == 0, pl.program_id(1) == 0)

    @pl.when(first)
    def _():
        carry_ref[...] = jnp.zeros(carry_ref.shape, F32)

    mix = (jnp.dot(a_ref[0], woa_ref[...], preferred_element_type=F32)
           + jnp.dot(b_ref[0], wob_ref[...], preferred_element_type=F32))
    x1 = x_ref[0] + g1_ref[0] * mix
    x1_ref[0] = x1
    ms = jnp.mean(x1 * x1, axis=-1, keepdims=True)
    h2 = x1 * lax.rsqrt(ms + RMS_EPS) * ng_ref[...]
    h2 = h2 * (1.0 + sc_ref[0]) + sh_ref[0]
    h2_ref[0] = h2

    h_top = pltpu.bitcast(pltpu.bitcast(h2, I32) & -65536, F32)
    h_hi = h_top.astype(BF16)
    h_lo = (h2 - h_top).astype(BF16)
    logits = (jnp.dot(h_hi, rwh_ref[...], preferred_element_type=F32)
              + jnp.dot(h_hi, rwl_ref[...], preferred_element_type=F32)
              + jnp.dot(h_lo, rwh_ref[...], preferred_element_type=F32)
              + jnp.dot(h_lo, rwl_ref[...], preferred_element_type=F32)) + rb_ref[...]
    ts = logits.shape[0]
    lane_i = lax.broadcasted_iota(I32, (ts, LANES), 1)
    lane = lane_i.astype(F32)
    neg_inf = jnp.float32(-jnp.inf)
    l = jnp.where(lane_i < N_EXPERTS, logits, neg_inf)
    vals, idxs = [], []
    for _ in range(TOP_K):
        m = jnp.max(l, axis=-1, keepdims=True)
        idx = jnp.min(jnp.where(l == m, lane, float(LANES)), axis=-1, keepdims=True)
        vals.append(m)
        idxs.append(idx)
        l = jnp.where(lane == idx, neg_inf, l)
    es = [jnp.exp(v - vals[0]) for v in vals]
    denom = es[0] + es[1] + es[2] + es[3]

    onehot = jnp.zeros((ts, LANES), F32)
    for idx in idxs:
        onehot = onehot + jnp.where(lane == idx, 1.0, 0.0)
    prefix = jnp.dot(tri_ref[...], onehot.astype(BF16), preferred_element_type=F32) + carry_ref[...]
    idx_out = jnp.zeros((ts, LANES), I32)
    gate_out = jnp.zeros((ts, LANES), F32)
    rank_out = jnp.zeros((ts, LANES), I32)
    for k in range(TOP_K):
        rank = jnp.sum(jnp.where(lane == idxs[k], prefix, 0.0), axis=-1, keepdims=True)
        idx_out = jnp.where(lane_i == k, idxs[k].astype(I32), idx_out)
        gate_out = jnp.where(lane_i == k, es[k] / denom, gate_out)
        rank_out = jnp.where(lane_i == k, rank.astype(I32), rank_out)
    idx_ref[0] = idx_out
    gate_ref[0] = gate_out
    rank_ref[0] = rank_out
    carry = carry_ref[...] + jnp.sum(onehot, axis=0, keepdims=True)
    carry_ref[...] = carry
    cnt_ref[...] = carry


def _out_and_route(a_out, b_out, x, g1, sh2, sc2, norm_g, w_out, router_w, router_b):
    B, S, D = x.shape
    ts = OUT_ROWS
    aw = a_out.shape[-1]
    w_bf = w_out.astype(BF16)
    rw = jnp.concatenate([router_w, jnp.zeros((D, LANES - N_EXPERTS), F32)], axis=1)
    rw_top = lax.bitcast_convert_type(lax.bitcast_convert_type(rw, I32) & -65536, F32)
    rw_hi = rw_top.astype(BF16)
    rw_lo = (rw - rw_top).astype(BF16)
    rb =jnp.concatenate([router_b, jnp.zeros((LANES - N_EXPERTS,), F32)]).reshape(1, LANES)
    r = np.arange(ts)
    tri = jnp.asarray(r[:, None] > r[None, :], BF16)
    row = lambda b, i: (b, 0, 0)
    full = lambda b, i: (0, 0)
    tok = lambda w: pl.BlockSpec((1, ts, w), lambda b, i: (b, i, 0))
    return pl.pallas_call(
        _out_kernel,
        grid=(B, S // ts),
        in_specs=[tok(aw), tok(D - aw), tok(D),
                  pl.BlockSpec((1, 1, D), row), pl.BlockSpec((1, 1, D), row),
                  pl.BlockSpec((1, 1, D), row), pl.BlockSpec((1, D), full),
                  pl.BlockSpec((aw, D), full), pl.BlockSpec((D - aw, D), full),
                  pl.BlockSpec((D, LANES), full), pl.BlockSpec((D, LANES), full),
                  pl.BlockSpec((1, LANES), full), pl.BlockSpec((ts, ts), full)],
        out_specs=[tok(D), tok(D), tok(LANES), tok(LANES), tok(LANES),
                   pl.BlockSpec((1, LANES), full)],
        out_shape=[jax.ShapeDtypeStruct((B, S, D), F32), jax.ShapeDtypeStruct((B, S, D), F32),
                   jax.ShapeDtypeStruct((B, S, LANES), I32),
                   jax.ShapeDtypeStruct((B, S, LANES), F32),
                   jax.ShapeDtypeStruct((B, S, LANES), I32),
                   jax.ShapeDtypeStruct((1, LANES), F32)],
        scratch_shapes=[pltpu.VMEM((1, LANES), F32)],
        compiler_params=_cparams(2),
    )(a_out, b_out, x, g1, sh2, sc2, norm_g.reshape(1, D), w_bf[:aw], w_bf[aw:], rw_hi, rw_lo, rb,
      tri)


def _row_copy_wait(src_ref, dst_ref, sem, rows):
    pltpu.make_async_copy(src_ref.at[pl.ds(0, rows), :], dst_ref.at[pl.ds(0, rows), :], sem).wait()


def _dispatch_kernel(dest_ref, h_ref, xin_ref, xbuf_ref, sem):
    del xin_ref
    rows = h_ref.shape[0]

    def body(r, c):
        for k in range(TOP_K):
            d = dest_ref[0, 0, r * TOP_K + k]
            pltpu.make_async_copy(h_ref.at[pl.ds(r, 1), :], xbuf_ref.at[pl.ds(d, 1), :], sem).start()
        return c

    lax.fori_loop(0, rows, body, 0)
    for _ in range(TOP_K):
        _row_copy_wait(h_ref, xbuf_ref, sem, rows)


def _dispatch(h2, dest, n_rows):
    T, D = h2.shape
    ts = min(DISPATCH_ROWS, T)
    xbuf0 = jnp.zeros((n_rows, D), F32)
    return pl.pallas_call(
        _dispatch_kernel,
        grid=(T // ts,),
        in_specs=[pl.BlockSpec((1, 1, ts * TOP_K), lambda i: (i, 0, 0), memory_space=pltpu.SMEM),
                  pl.BlockSpec((ts, D), lambda i: (i, 0)),
                  pl.BlockSpec(memory_space=pl.ANY)],
        out_specs=pl.BlockSpec(memory_space=pl.ANY),
        out_shape=jax.ShapeDtypeStruct((n_rows, D), F32),
        scratch_shapes=[pltpu.SemaphoreType.DMA(())],
        input_output_aliases={2: 0},
        compiler_params=_cparams(1),
    )(dest.reshape(T // ts, 1, ts * TOP_K), h2, xbuf0)


def _moe_kernel(be_ref, nused_ref, x_ref, wg_ref, bg_ref, wu_ref, bu_ref, wd_ref, bd_ref, y_ref,
                wg_s, wu_s, wd_s):
    i = pl.program_id(0)
    used = i < nused_ref[0]
    new_expert = jnp.logical_or(i == 0, be_ref[i] != be_ref[jnp.maximum(i - 1, 0)])

    @pl.when(jnp.logical_and(used, new_expert))
    def _():
        wg_s[...] = wg_ref[0].astype(BF16)
        wu_s[...] = wu_ref[0].astype(BF16)
        wd_s[...] = wd_ref[0].astype(BF16)

    @pl.when(used)
    def _():
        xb = x_ref[...].astype(BF16)
        g = jnp.dot(xb, wg_s[...], preferred_element_type=F32) + bg_ref[0]
        u = jnp.dot(xb, wu_s[...], preferred_element_type=F32) + bu_ref[0]
        g = jnp.minimum(g, SWIGLU_LIMIT)
        u = jnp.clip(u, -SWIGLU_LIMIT, SWIGLU_LIMIT)
        a = g * (1.0 / (1.0 + jnp.exp(-SWIGLU_ALPHA * g))) * (u + 1.0)
        y_ref[...] = jnp.dot(a.astype(BF16), wd_s[...], preferred_element_type=F32) + bd_ref[0]

    @pl.when(jnp.logical_not(used))
    def _():
        y_ref[...] = jnp.zeros(y_ref.shape, F32)


def _moe_experts(xbuf, block_expert, n_used, w_gate, b_gate, w_up, b_up, w_down, b_down):
    R, D = xbuf.shape
    E, _, F = w_gate.shape
    G = MOE_ROWS
    nb = R // G
    last = lambda i, nu: jnp.maximum(jnp.minimum(i, nu[0] - 1), 0)
    blk = lambda i, be, nu: (last(i, nu), 0)
    wsel = lambda i, be, nu: (be[last(i, nu)], 0, 0)
    grid_spec = pltpu.PrefetchScalarGridSpec(
        num_scalar_prefetch=2,
        grid=(nb,),
        in_specs=[pl.BlockSpec((G, D), blk),
                  pl.BlockSpec((1, D, F), wsel), pl.BlockSpec((1, 1, F), wsel),
                  pl.BlockSpec((1, D, F), wsel), pl.BlockSpec((1, 1, F), wsel),
                  pl.BlockSpec((1, F, D), wsel), pl.BlockSpec((1, 1, D), wsel)],
        out_specs=pl.BlockSpec((G, D), lambda i, be, nu: (i, 0)),
        scratch_shapes=[pltpu.VMEM((D, F), BF16), pltpu.VMEM((D, F), BF16),
                        pltpu.VMEM((F, D), BF16)],
    )
    return pl.pallas_call(
        _moe_kernel,
        grid_spec=grid_spec,
        out_shape=jax.ShapeDtypeStruct((R, D), F32),
        compiler_params=_cparams(1),
    )(block_expert, n_used, xbuf, w_gate, b_gate.reshape(E, 1, F),
      w_up, b_up.reshape(E, 1, F), w_down, b_down.reshape(E, 1, D))


def _combine_kernel(dest_ref, gate_ref, x1_ref, g2_ref, ybuf_ref, o_ref, buf_ref, sem):
    rows = x1_ref.shape[0]

    def body(r, c):
        for k in range(TOP_K):
            d = dest_ref[0, 0, r * TOP_K + k]
            pltpu.make_async_copy(ybuf_ref.at[pl.ds(d, 1), :], buf_ref.at[k, pl.ds(r, 1), :],
                                  sem).start()
        return c

    lax.fori_loop(0, rows, body, 0)
    for k in range(TOP_K):
        _row_copy_wait(ybuf_ref, buf_ref.at[k], sem, rows)
    gate = gate_ref[...]
    y = buf_ref[0] * gate[:, 0:1]
    for k in range(1, TOP_K):
        y = y + buf_ref[k] * gate[:, k:k + 1]
    o_ref[...] = x1_ref[...] + g2_ref[0] * y


def _combine(ybuf, dest, gates, x1, g2, seq):
    T, D = x1.shape
    ts = COMBINE_ROWS
    per_seq = seq // ts
    return pl.pallas_call(
        _combine_kernel,
        grid=(T // ts,),
        in_specs=[pl.BlockSpec((1, 1, ts * TOP_K), lambda i: (i, 0, 0), memory_space=pltpu.SMEM),
                  pl.BlockSpec((ts, LANES), lambda i: (i, 0)),
                  pl.BlockSpec((ts, D), lambda i: (i, 0)),
                  pl.BlockSpec((1, 1, D), lambda i: (i // per_seq, 0, 0)),
                  pl.BlockSpec(memory_space=pl.ANY)],
        out_specs=pl.BlockSpec((ts, D), lambda i: (i, 0)),
        out_shape=jax.ShapeDtypeStruct((T, D), F32),
        scratch_shapes=[pltpu.VMEM((TOP_K, ts, D), F32), pltpu.SemaphoreType.DMA(())],
        compiler_params=_cparams(1),
    )(dest.reshape(T // ts, 1, ts * TOP_K), gates, x1, g2, ybuf)


def _layer(x, mod, lambda_init, norm1_g, norm2_g, w_in, w_out, a_q_norm_g, a_k_norm_g, a_lambda,
           a_sub_g, b_q_norm_g, b_k_norm_g, b_kv_norm_g, b_w_uv, router_w, router_b, w_gate,
           b_gate, w_up, b_up, w_down, b_down):
    B, S, D = x.shape
    T = B * S
    sh1, sc1, g1, sh2, sc2, g2 = [m.reshape(B, 1, D) for m in jnp.split(mod, 6, axis=-1)]

    (aq1, aq2, ak1, ak2, bq, bk, ik, iq, blat1, av1, iw), plain_a, plain_b = _project(
        x, sh1, sc1, norm1_g, w_in, a_q_norm_g, a_k_norm_g, b_q_norm_g, b_k_norm_g, b_kv_norm_g)
    a_out = _diff_attention(plain_a, aq1, aq2, ak1, ak2, av1, a_lambda, a_sub_g, lambda_init)
    b_out = _dsa_attention(plain_b, iq, iw, ik, bq, bk, blat1, b_w_uv)

    x1, h2, top_idx, gates, rank, counts = _out_and_route(
        a_out, b_out, x, g1, sh2, sc2, norm2_g, w_out, router_w, router_b)

    G = MOE_ROWS
    counts = counts[0, :N_EXPERTS].astype(I32)
    padded = ((counts + G - 1) // G) * G
    pcum = jnp.cumsum(padded)
    poff = pcum - padded
    nb = (T * TOP_K) // G + N_EXPERTS
    starts = jnp.arange(nb, dtype=I32) * G
    block_expert = jnp.minimum(
        jnp.sum((pcum[None, :] <= starts[:, None]).astype(I32), axis=1), N_EXPERTS - 1)
    n_used = (pcum[-1:] // G).astype(I32)
    top_idx = top_idx.reshape(T, LANES)[:, :TOP_K]
    dest = (poff[top_idx] + rank.reshape(T, LANES)[:, :TOP_K]).astype(I32).reshape(T * TOP_K)

    xbuf = _dispatch(h2.reshape(T, D), dest, nb * G)
    ybuf = _moe_experts(xbuf, block_expert.astype(I32), n_used, w_gate, b_gate, w_up, b_up,
                        w_down, b_down)
    out = _combine(ybuf, dest, gates.reshape(T, LANES), x1.reshape(T, D), g2, S)
    return out.reshape(B, S, D)


def kernel(x, c, norm1_g, norm2_g, w_ada, b_ada, w_in, w_out, a_q_norm_g, a_k_norm_g, a_lambda,
           a_sub_g, b_q_norm_g, b_k_norm_g, b_kv_norm_g, b_w_uv, router_w, router_b, w_gate,
           b_gate, w_up, b_up, w_down, b_down):
    depth = w_in.shape[0]
    for l in range(depth):
        lambda_init = 0.8 - 0.6 * math.exp(-0.3 * l)
        mod = _ada(c, w_ada[l], b_ada[l])
        x = _layer(x, mod, lambda_init, norm1_g[l], norm2_g[l], w_in[l], w_out[l], a_q_norm_g[l],
                   a_k_norm_g[l], a_lambda[l], a_sub_g[l], b_q_norm_g[l], b_k_norm_g[l],
                   b_kv_norm_g[l], b_w_uv[l], router_w[l], router_b[l], w_gate[l], b_gate[l],
                   w_up[l], b_up[l], w_down[l], b_down[l])
    return x
```

```python
import functools
import math

import numpy as np
import jax
import jax.numpy as jnp
from jax import lax
from jax.experimental import pallas as pl
from jax.experimental.pallas import tpu as pltpu

F32 = jnp.float32
BF16 = jnp.bfloat16
I32 = jnp.int32

CHUNK = 64
HEAD_DIM = 64
ROT_DIM = HEAD_DIM // 4
ROPE_THETA = 500000.0
RMS_EPS = 1e-6
A_HEADS = 4
A_V_DIM = 2 * HEAD_DIM
B_HEADS = 8
B_LATENT = 128
IDX_HEADS = 8
IDX_TOPK_MAX = 256
N_EXPERTS = 32
TOP_K = 4
SWIGLU_LIMIT = 7.0
SWIGLU_ALPHA = 1.702

LANES = 128
INT_MIN = -(2 ** 31)
NEG_BIG = -1e30
VMEM_LIMIT = 56 * 1024 * 1024

NORM_SLACK = 1.01
PLAIN_EXP_MAX_BOUND = 30.0

PROJ_ROWS = 512
ATT_TQ = 512
DSA_TQ = 256
DSA_TK = 512
DSA_HEAD_GROUP = 4
COUNT_ROWS = 128
COUNT_CHAINS = 8
WALK_UNROLL = 4
OUT_ROWS = 512
MOE_ROWS = 512
DISPATCH_ROWS = 1024
COMBINE_ROWS = 512


def _cparams(n_axes):
    return pltpu.CompilerParams(
        dimension_semantics=("arbitrary",) * n_axes, vmem_limit_bytes=VMEM_LIMIT)


def _dot_nt(a, b):
    return lax.dot_general(a, b, (((1,), (1,)), ((), ())), preferred_element_type=F32)


def _ada_kernel(c_ref, w_ref, b_ref, o_ref):
    c = c_ref[...]
    sc = c / (1.0 + jnp.exp(-c))
    o_ref[...] = jnp.dot(sc, w_ref[...], preferred_element_type=F32,
                         precision=lax.Precision.HIGHEST) + b_ref[...]


def _ada(c, w, b):
    B, D = c.shape
    N = w.shape[1]
    return pl.pallas_call(
        _ada_kernel,
        grid=(N // D,),
        in_specs=[pl.BlockSpec((B, D), lambda j: (0, 0)),
                  pl.BlockSpec((D, D), lambda j: (0, j)),
                  pl.BlockSpec((1, D), lambda j: (0, j))],
        out_specs=pl.BlockSpec((B, D), lambda j: (0, j)),
        out_shape=jax.ShapeDtypeStruct((B, N), F32),
        compiler_params=_cparams(1),
    )(c, w, b.reshape(1, N))


C_QK = 0
C_BQ = 1024
C_BKIK = 1536
C_IQ = 1664
C_LAT = 2176
C_AV = 2304
C_IW = 2816
C_END = 2944
N_GAIN = C_IQ


def _group_sumsq(p, bd):
    return jnp.dot((p * p).astype(BF16), bd, preferred_element_type=F32)


def _rope(y, c, s1, s2):
    w = y.shape[1]
    return y * c + pltpu.roll(y, w - ROT_DIM // 2, 1) * s1 + pltpu.roll(y, ROT_DIM // 2, 1) * s2


def _proj_kernel(x_ref, sh_ref, sc_ref, g_ref, w_ref, gain_ref, kb_ref, latg_ref, rc_ref, rs1_ref,
                 rs2_ref, bd_ref,
                 aq1_ref, aq2_ref, ak1_ref, ak2_ref, bq_ref, bk_ref, ik_ref, iq_ref, lat_ref,
                 av_ref, iw_ref):
    x = x_ref[0]
    ms = jnp.mean(x * x, axis=-1, keepdims=True)
    h = x * lax.rsqrt(ms + RMS_EPS) * g_ref[...]
    h = (h * (1.0 + sc_ref[0]) + sh_ref[0]).astype(BF16)

    rc, rs1, rs2 = rc_ref[...], rs1_ref[...], rs2_ref[...]
    bd = bd_ref[...]
    ts = x.shape[0]
    lane = lax.broadcasted_iota(I32, (ts, LANES), 1)
    ones = jnp.ones((ts, LANES), F32)

    def proj(c0, width):
        return jnp.dot(h, w_ref[:, c0:c0 + width], preferred_element_type=F32)

    def normed(p, c0):
        width = p.shape[1]
        ss = _group_sumsq(p, bd[:width, :width])
        return p * lax.rsqrt(ss * (1.0 / HEAD_DIM) + RMS_EPS) * gain_ref[:, c0:c0 + width]

    def store_slots(ref, y, extra, first_head=0):
        for pair in range(y.shape[1] // LANES):
            z = y[:, pair * LANES:(pair + 1) * LANES]
            e = extra[:, pair * LANES:(pair + 1) * LANES]
            even = jnp.where(lane < HEAD_DIM, z,
                             jnp.where(lane == HEAD_DIM, pltpu.roll(e, HEAD_DIM, 1), 0.0))
            odd = jnp.where(lane < HEAD_DIM, pltpu.roll(z, HEAD_DIM, 1),
                            jnp.where(lane == HEAD_DIM, e, 0.0))
            ref[0, first_head + 2 * pair] = even.astype(BF16)
            ref[0, first_head + 2 * pair + 1] = odd.astype(BF16)

    def query(c0):
        y = _rope(normed(proj(c0, 256), c0), rc, rs1, rs2)
        norm = jnp.sqrt(jnp.dot((y * y).astype(BF16), bd, preferred_element_type=F32))
        return y, -norm * kb_ref[:, c0:c0 + 256]

    def key(c0):
        return _rope(normed(proj(c0, 256), c0), rc, rs1, rs2), jnp.ones((ts, 256), F32)

    store_slots(aq1_ref, *query(0))
    store_slots(aq2_ref, *query(256))
    store_slots(ak1_ref, *key(512))
    store_slots(ak2_ref, *key(768))
    for half in range(2):
        store_slots(bq_ref, *query(C_BQ + 256 * half), first_head=4 * half)
        y = _rope(proj(C_IQ + 256 * half, 256), rc, rs1, rs2)
        for j in range(4):
            iq_ref[0, 4 * half + j] = y[:, j * HEAD_DIM:(j + 1) * HEAD_DIM].astype(BF16)

    p = proj(C_BKIK, LANES)
    y = _rope(jnp.where(lane < HEAD_DIM, normed(p, C_BKIK), p),
              rc[:, :LANES], rs1[:, :LANES], rs2[:, :LANES])
    bk_ref[0] = jnp.where(lane < HEAD_DIM, y, jnp.where(lane == HEAD_DIM, 1.0, 0.0)).astype(BF16)
    ik_ref[0] = y[:, HEAD_DIM:].astype(BF16)

    p = proj(C_LAT, LANES)
    ms = jnp.mean(p * p, axis=-1, keepdims=True)
    lat = p * lax.rsqrt(ms + RMS_EPS) * latg_ref[...]
    lat_ref[0] = jnp.concatenate([lat, ones], axis=1).astype(BF16)

    for j in range(A_HEADS):
        av_ref[0, j] = jnp.concatenate([proj(C_AV + j * A_V_DIM, A_V_DIM), ones],
                                       axis=1).astype(BF16)

    p = proj(C_IW, LANES)
    iw_ref[0] = p[:, :IDX_HEADS] * (IDX_HEADS ** -0.5 * HEAD_DIM ** -0.5)


def _rope_tables(S, width):
    half = ROT_DIM // 2
    pos = jnp.arange(S, dtype=F32)
    inv = ROPE_THETA ** (-jnp.arange(0, ROT_DIM, 2, dtype=F32) / ROT_DIM)
    ang = pos[:, None] * inv[None, :]
    cos, sin = jnp.cos(ang), jnp.sin(ang)
    zeros = jnp.zeros((S, HEAD_DIM - ROT_DIM), F32)
    c = jnp.concatenate([cos, cos, zeros + 1.0], axis=1)
    s1 = jnp.concatenate([-sin, jnp.zeros((S, half), F32), zeros], axis=1)
    s2 = jnp.concatenate([jnp.zeros((S, half), F32), sin, zeros], axis=1)
    reps = width // HEAD_DIM
    return tuple(jnp.tile(t, (1, reps)) for t in (c, s1, s2))


def _project(x, sh1, sc1, norm_g, w_in, a_q_g, a_k_g, b_q_g, b_k_g, b_kv_g):
    B, S, D = x.shape
    ts = PROJ_ROWS
    sizes = (256, 256, 256, 256, 512, 512, 64, 128, 512, 64, 8)
    offs = np.concatenate([[0], np.cumsum(sizes)])
    seg = lambda i: w_in[:, offs[i]:offs[i + 1]]
    w_p = jnp.concatenate(
        [seg(0), seg(1), seg(2), seg(3), seg(5), seg(6), seg(9), seg(8), seg(7), seg(4), seg(10),
         jnp.zeros((D, C_END - C_IW - IDX_HEADS), F32)], axis=1).astype(BF16)
    scale = HEAD_DIM ** -0.5
    gain = jnp.concatenate(
        [jnp.tile(a_q_g * scale, 2 * A_HEADS), jnp.tile(a_k_g, 2 * A_HEADS),
         jnp.tile(b_q_g * scale, B_HEADS), b_k_g, jnp.ones((HEAD_DIM,), F32)]).reshape(1, N_GAIN)
    kb_a = 8.0 * NORM_SLACK * jnp.max(jnp.abs(a_k_g))
    kb_b = 8.0 * NORM_SLACK * jnp.max(jnp.abs(b_k_g))
    zeros = lambda n: jnp.zeros((n,), F32)
    kb = jnp.concatenate([zeros(512) + kb_a, zeros(512), zeros(512) + kb_b,
                          zeros(N_GAIN - C_BKIK)]).reshape(1, N_GAIN)
    bound_a = jnp.max(jnp.abs(a_q_g)) * kb_a
    bound_b = jnp.max(jnp.abs(b_q_g)) * kb_b
    rc, rs1, rs2 = _rope_tables(S, 256)
    gid = np.arange(256) // HEAD_DIM
    bd = jnp.asarray(gid[:, None] == gid[None, :], BF16)

    row = lambda b, i: (b, 0, 0)
    full = lambda b, i: (0, 0)
    heads = lambda n, w: pl.BlockSpec((1, n, ts, w), lambda b, i: (b, 0, i, 0))
    flat = lambda w: pl.BlockSpec((1, ts, w), lambda b, i: (b, i, 0))
    hshape = lambda n, w: jax.ShapeDtypeStruct((B, n, S, w), BF16)
    outs = pl.pallas_call(
        _proj_kernel,
        grid=(B, S // ts),
        in_specs=[pl.BlockSpec((1, ts, D), lambda b, i: (b, i, 0)),
                  pl.BlockSpec((1, 1, D), row), pl.BlockSpec((1, 1, D), row),
                  pl.BlockSpec((1, D), full),
                  pl.BlockSpec((D, C_END), full),
                  pl.BlockSpec((1, N_GAIN), full),
                  pl.BlockSpec((1, N_GAIN), full),
                  pl.BlockSpec((1, LANES), full),
                  pl.BlockSpec((ts, 256), lambda b, i: (i, 0)),
                  pl.BlockSpec((ts, 256), lambda b, i: (i, 0)),
                  pl.BlockSpec((ts, 256), lambda b, i: (i, 0)),
                  pl.BlockSpec((256, 256), full)],
        out_specs=[heads(A_HEADS, LANES)] * 4
        + [heads(B_HEADS, LANES), flat(LANES), flat(HEAD_DIM), heads(IDX_HEADS, HEAD_DIM),
           flat(2 * B_LATENT), heads(A_HEADS, 2 * A_V_DIM), flat(IDX_HEADS)],
        out_shape=[hshape(A_HEADS, LANES)] * 4
        + [hshape(B_HEADS, LANES), jax.ShapeDtypeStruct((B, S, LANES), BF16),
           jax.ShapeDtypeStruct((B, S, HEAD_DIM), BF16), hshape(IDX_HEADS, HEAD_DIM),
           jax.ShapeDtypeStruct((B, S, 2 * B_LATENT), BF16), hshape(A_HEADS, 2 * A_V_DIM),
           jax.ShapeDtypeStruct((B, S, IDX_HEADS), F32)],
        compiler_params=_cparams(2),
    )(x, sh1, sc1, norm_g.reshape(1, D), w_p, gain, kb, b_kv_g.reshape(1, B_LATENT), rc, rs1, rs2,
      bd)
    plain_a = (bound_a <= PLAIN_EXP_MAX_BOUND).astype(I32).reshape(1)
    plain_b = (bound_b <= PLAIN_EXP_MAX_BOUND).astype(I32).reshape(1)
    return outs, plain_a, plain_b


def _softmax_init(plain, m_ref, acc_ref):
    acc_ref[...] = jnp.zeros(acc_ref.shape, F32)
    if not plain:
        m_ref[...] = jnp.full(m_ref.shape, NEG_BIG, F32)


def _softmax_step(plain, s, v1, m_ref, acc_ref):
    if plain:
        acc_ref[...] += jnp.dot(jnp.exp(s).astype(BF16), v1, preferred_element_type=F32)
        return
    m_old = m_ref[...]
    m_new = jnp.maximum(m_old, jnp.max(s, axis=-1, keepdims=True))
    p = jnp.exp(s - m_new).astype(BF16)
    acc_ref[...] = (jnp.exp(m_old - m_new) * acc_ref[...]
                    + jnp.dot(p, v1, preferred_element_type=F32))
    m_ref[...] = m_new


def _softmax_result(acc_ref, width):
    acc = acc_ref[...]
    return acc[:, :width] / acc[:, width:width + 1]


def _chunk_mask(q0, k0, tq, tk):
    qc = (q0 + lax.broadcasted_iota(I32, (tq, tk), 0)) // CHUNK
    kc = (k0 + lax.broadcasted_iota(I32, (tq, tk), 1)) // CHUNK
    return kc <= qc


def _diff_attn_kernel(plain_ref, lam_ref, subg_ref, diag_ref, q1_ref, q2_ref, k1_ref, k2_ref, v_ref,
                      o_ref, m1_ref, acc1_ref, m2_ref, acc2_ref, *, lambda_init):
    tq = tk = ATT_TQ
    i = pl.program_id(2)
    n_tiles = i + 1

    def attend(plain):
        q1 = q1_ref[0, 0]
        q2 = q2_ref[0, 0]
        _softmax_init(plain, m1_ref, acc1_ref)
        _softmax_init(plain, m2_ref, acc2_ref)

        def tile(j, masked):
            ks = pl.multiple_of(j * tk, tk)
            v1 = v_ref[0, 0, pl.ds(ks, tk), :]
            s1 = _dot_nt(q1, k1_ref[0, 0, pl.ds(ks, tk), :])
            s2 = _dot_nt(q2, k2_ref[0, 0, pl.ds(ks, tk), :])
            if masked:
                s1 = s1 + diag_ref[...]
                s2 = s2 + diag_ref[...]
            _softmax_step(plain, s1, v1, m1_ref, acc1_ref)
            _softmax_step(plain, s2, v1, m2_ref, acc2_ref)

        def body(j, c):
            tile(j, False)
            return c

        lax.fori_loop(0, n_tiles - 1, body, 0)
        tile(n_tiles - 1, True)

    pl.when(plain_ref[0] == 1)(lambda: attend(True))
    pl.when(plain_ref[0] != 1)(lambda: attend(False))

    lv = lam_ref[...]
    lam = (jnp.exp(jnp.sum(lv[0:1] * lv[1:2], axis=-1, keepdims=True))
           - jnp.exp(jnp.sum(lv[2:3] * lv[3:4], axis=-1, keepdims=True)) + lambda_init)
    o = _softmax_result(acc1_ref, A_V_DIM) - lam * _softmax_result(acc2_ref, A_V_DIM)
    ms = jnp.mean(o * o, axis=-1, keepdims=True)
    o = o * lax.rsqrt(ms + RMS_EPS) * subg_ref[...] * (1.0 - lambda_init)
    o_ref[0] = o.astype(BF16)


def _diff_attention(plain, q1, q2, k1, k2, v1, a_lambda, sub_g, lambda_init):
    B, H, S, dq = q1.shape
    tq = ATT_TQ
    chunk = np.arange(tq) // CHUNK
    diag = jnp.asarray(np.where(chunk[None, :] <= chunk[:, None], 0.0, NEG_BIG), F32)
    qspec = pl.BlockSpec((1, 1, tq, dq), lambda b, h, i, p: (b, h, i, 0))
    kspec = pl.BlockSpec((1, 1, S, dq), lambda b, h, i, p: (b, h, 0, 0))
    col = lambda: pltpu.VMEM((tq, 1), F32)
    acc = lambda: pltpu.VMEM((tq, 2 * A_V_DIM), F32)
    grid_spec = pltpu.PrefetchScalarGridSpec(
        num_scalar_prefetch=1,
        grid=(B, H, S // tq),
        in_specs=[pl.BlockSpec((4, HEAD_DIM), lambda b, h, i, p: (0, 0)),
                  pl.BlockSpec((1, A_V_DIM), lambda b, h, i, p: (0, 0)),
                  pl.BlockSpec((tq, tq), lambda b, h, i, p: (0, 0)),
                  qspec, qspec, kspec, kspec,
                  pl.BlockSpec((1, 1, S, 2 * A_V_DIM), lambda b, h, i, p: (b, h, 0, 0))],
        out_specs=pl.BlockSpec((1, tq, A_V_DIM), lambda b, h, i, p: (b, i, h)),
        scratch_shapes=[col(), acc(), col(), acc()],
    )
    return pl.pallas_call(
        functools.partial(_diff_attn_kernel, lambda_init=lambda_init),
        grid_spec=grid_spec,
        out_shape=jax.ShapeDtypeStruct((B, S, H * A_V_DIM), BF16),
        compiler_params=_cparams(3),
    )(plain, a_lambda, sub_g.reshape(1, A_V_DIM), diag, q1, q2, k1, k2, v1)


def _dsa_kernel(plain_ref, iq_ref, iw_ref, ik_ref, q_ref, k_ref, lat_ref, wuv_ref, tri_ref, o_ref,
                score_ref, score_t_ref, thr_ref, m_ref, acc_ref, *, topk):
    tq, tk = DSA_TQ, DSA_TK
    nh = B_HEADS
    i = pl.program_id(1)
    n_tiles = ((i + 1) * tq + tk - 1) // tk

    hg = DSA_HEAD_GROUP
    iw = iw_ref[0]

    def score_tile(j, masked):
        ks = pl.multiple_of(j * tk, tk)
        ik = ik_ref[0, pl.ds(ks, tk), :]
        score = None
        for g in range(IDX_HEADS // hg):
            iq = iq_ref[0, g * hg:(g + 1) * hg].reshape(hg * tq, HEAD_DIM)
            rel = jnp.maximum(_dot_nt(iq, ik), 0.0).reshape(hg, tq, tk)
            for h in range(hg):
                term = rel[h] * iw[:, g * hg + h:g * hg + h + 1]
                score = term if score is None else score + term
        if masked:
            score = jnp.where(_chunk_mask(i * tq, ks, tq, tk), score, -jnp.inf)
        score_ref[:, pl.ds(ks, tk)] = score
        score_t_ref[pl.ds(ks, tk), :] = score.T

    def score_body(j, c):
        score_tile(j, False)
        return c

    lax.fori_loop(0, n_tiles - 1, score_body, 0)
    score_tile(n_tiles - 1, True)

    def image_to_float(t):
        return pltpu.bitcast(jnp.where(t < 0, t ^ 0x7FFFFFFF, t), F32)

    def count(pred, cand):
        parts = []
        for r0 in range(0, tq, COUNT_ROWS):
            rows = pl.ds(r0, COUNT_ROWS)
            cand_r = cand[r0:r0 + COUNT_ROWS]

            def body(j, acc, rows=rows, cand_r=cand_r):
                ks = pl.multiple_of(j * tk, tk)
                hit = jnp.where(pred(score_ref[rows, pl.ds(ks, tk)], cand_r), 1.0, 0.0)
                for c in range(tk // LANES):
                    acc = acc + hit[:, c * LANES:(c + 1) * LANES]
                return acc
            parts.append(lax.fori_loop(0, n_tiles, body, jnp.zeros((COUNT_ROWS, LANES), F32)))
        acc = jnp.concatenate(parts, axis=0)
        return jnp.sum(acc, axis=-1, keepdims=True)

    ge = lambda s, c: s >= c
    gt = lambda s, c: s > c

    def all_of(flags):
        return (jnp.min(flags) > 0.0).astype(I32)

    def count_t(pred, cand):
        def body(j, acc):
            ks = pl.multiple_of(j * tq, tq)
            hit = jnp.where(pred(score_t_ref[pl.ds(ks, tq), :], cand), 1.0, 0.0)
            return acc + jnp.sum(hit.reshape(-1, COUNT_CHAINS, 8, tq), axis=0)
        acc = lax.fori_loop(0, i + 1, body, jnp.zeros((COUNT_CHAINS, 8, tq), F32))
        return jnp.sum(jnp.sum(acc, axis=0), axis=0, keepdims=True)

    zero = jnp.zeros((1, tq), F32)
    n_ge0 = count_t(ge, zero)
    tie0 = jnp.where(jnp.logical_and(count_t(gt, zero) < topk, n_ge0 > topk), 1.0, 0.0)
    t0 = jnp.where(n_ge0 >= topk, 0, INT_MIN)
    settled0 = jnp.where(n_ge0 == topk, 1.0, tie0)

    def bit_body(state):
        step, t, settled, _ = state
        for _ in range(WALK_UNROLL):
            bit = jnp.where(step < 32, jnp.left_shift(jnp.int32(1), jnp.maximum(31 - step, 0)), 0)
            cand = t + bit
            n_ge = count_t(ge, image_to_float(cand))
            t = jnp.where(settled > 0.0, t, jnp.where(n_ge >= topk, cand, t))
            settled = jnp.where(n_ge == topk, 1.0, settled)
            step = step + 1
        return step, t, settled, all_of(settled)

    def bit_cond(state):
        step, _, _, all_settled = state
        return jnp.logical_and(step < 32, all_settled == 0)

    _, t, _, all_settled = lax.while_loop(
        bit_cond, bit_body, (jnp.int32(1), t0, settled0, all_of(settled0)))
    thr_row = jnp.where(t == INT_MIN, jnp.finfo(F32).min, image_to_float(t))
    thr_ref[...] = jnp.broadcast_to(thr_row, (LANES, tq)).T[:, 0:1]

    def min_where(pred, cand):
        def body(j, acc):
            ks = pl.multiple_of(j * tk, tk)
            s = score_ref[:, pl.ds(ks, tk)]
            s = jnp.where(pred(s, cand), s, jnp.inf)
            for c in range(tk // LANES):
                acc = jnp.minimum(acc, s[:, c * LANES:(c + 1) * LANES])
            return acc
        acc = lax.fori_loop(0, n_tiles, body, jnp.full((tq, LANES), jnp.inf, F32))
        return jnp.min(acc, axis=-1, keepdims=True)

    @pl.when(all_settled == 0)
    def _():
        thr0 = thr_ref[...]
        low = min_where(ge, thr0)
        nxt = min_where(gt, low)
        thr_ref[...] = jnp.where(count(ge, nxt) >= topk, nxt,
                                 jnp.where(low < jnp.inf, low, thr0))

    @pl.when(jnp.logical_or(all_settled == 0, jnp.max(tie0) > 0.0))
    def _():
        thr = thr_ref[...]
        quota = topk - count(gt, thr)

        def tie_body(j, carry):
            sl = pl.ds(pl.multiple_of(j * tk, tk), tk)
            score = score_ref[:, sl]
            eq = score == thr
            prefix = carry + jnp.dot(jnp.where(eq, 1.0, 0.0).astype(BF16), tri_ref[...],
                                     preferred_element_type=F32)
            score_ref[:, sl] = jnp.where(eq & (prefix > quota), -jnp.inf, score)
            return prefix[:, tk - 1:tk]

        lax.fori_loop(0, n_tiles, tie_body, jnp.zeros((tq, 1), F32))

    thr = thr_ref[...]

    def attend(plain):
        _softmax_init(plain, m_ref, acc_ref)

        def attn_body(j, c):
            ks = pl.multiple_of(j * tk, tk)
            k = k_ref[0, pl.ds(ks, tk), :]
            v1 = lat_ref[0, pl.ds(ks, tk), :]
            sel = score_ref[:, pl.ds(ks, tk)] >= thr
            for g in range(nh // hg):
                rows = pl.ds(g * hg * tq, hg * tq)
                q = q_ref[0, g * hg:(g + 1) * hg].reshape(hg * tq, LANES)
                s = _dot_nt(q, k).reshape(hg, tq, tk)
                s = jnp.where(sel[None], s, NEG_BIG).reshape(hg * tq, tk)
                _softmax_step(plain, s, v1, m_ref.at[rows, :], acc_ref.at[rows, :])
            return c

        lax.fori_loop(0, n_tiles, attn_body, 0)

    pl.when(plain_ref[0] == 1)(lambda: attend(True))
    pl.when(plain_ref[0] != 1)(lambda: attend(False))

    for h in range(nh):
        o = _softmax_result(acc_ref.at[pl.ds(h * tq, tq), :], B_LATENT).astype(BF16)
        oh = jnp.dot(o, wuv_ref[h], preferred_element_type=F32)
        o_ref[0, :, h * HEAD_DIM:(h + 1) * HEAD_DIM] = oh.astype(BF16)


def _dsa_attention(plain, iq, iw, ik, q, k, lat1, w_uv):
    B, nh, S, dq = q.shape
    dh = HEAD_DIM
    tq = DSA_TQ
    topk = min(IDX_TOPK_MAX, S // 4)
    col = np.arange(DSA_TK)
    tri = jnp.asarray(col[:, None] <= col[None, :], BF16)
    hspec = lambda w: pl.BlockSpec((1, nh, tq, w), lambda b, i, p: (b, 0, i, 0))
    kspec = lambda w: pl.BlockSpec((1, S, w), lambda b, i, p: (b, 0, 0))
    grid_spec = pltpu.PrefetchScalarGridSpec(
        num_scalar_prefetch=1,
        grid=(B, S // tq),
        in_specs=[hspec(dh),
                  pl.BlockSpec((1, tq, IDX_HEADS), lambda b, i, p: (b, i, 0)),
                  kspec(dh), hspec(dq), kspec(dq), kspec(2 * B_LATENT),
                  pl.BlockSpec((nh, B_LATENT, dh), lambda b, i, p: (0, 0, 0)),
                  pl.BlockSpec((DSA_TK, DSA_TK), lambda b, i, p: (0, 0))],
        out_specs=pl.BlockSpec((1, tq, nh * dh), lambda b, i, p: (b, i, 0)),
        scratch_shapes=[pltpu.VMEM((tq, S), F32),
                        pltpu.VMEM((S, tq), F32),
                        pltpu.VMEM((tq, 1), F32),
                        pltpu.VMEM((nh * tq, 1), F32),
                        pltpu.VMEM((nh * tq, 2 * B_LATENT), F32)],
    )
    return pl.pallas_call(
        functools.partial(_dsa_kernel, topk=topk),
        grid_spec=grid_spec,
        out_shape=jax.ShapeDtypeStruct((B, S, nh * dh), BF16),
        compiler_params=_cparams(2),
    )(plain, iq, iw, ik, q, k, lat1, w_uv.astype(BF16), tri)


def _out_kernel(a_ref, b_ref, x_ref, g1_ref, sh_ref, sc_ref, ng_ref, woa_ref, wob_ref, rwh_ref,
                rwl_ref, rb_ref, tri_ref, x1_ref, h2_ref, idx_ref, gate_ref, rank_ref, cnt_ref,
                carry_ref):
    first = jnp.logical_and(pl.program_id(0) == 0, pl.program_id(1) == 0)

    @pl.when(first)
    def _():
        carry_ref[...] = jnp.zeros(carry_ref.shape, F32)

    mix = (jnp.dot(a_ref[0], woa_ref[...], preferred_element_type=F32)
           + jnp.dot(b_ref[0], wob_ref[...], preferred_element_type=F32))
    x1 = x_ref[0] + g1_ref[0] * mix
    x1_ref[0] = x1
    ms = jnp.mean(x1 * x1, axis=-1, keepdims=True)
    h2 = x1 * lax.rsqrt(ms + RMS_EPS) * ng_ref[...]
    h2 = h2 * (1.0 + sc_ref[0]) + sh_ref[0]
    h2_ref[0] = h2

    h_top = pltpu.bitcast(pltpu.bitcast(h2, I32) & -65536, F32)
    h_hi = h_top.astype(BF16)
    h_lo = (h2 - h_top).astype(BF16)
    logits = (jnp.dot(h_hi, rwh_ref[...], preferred_element_type=F32)
              + jnp.dot(h_hi, rwl_ref[...], preferred_element_type=F32)
              + jnp.dot(h_lo, rwh_ref[...], preferred_element_type=F32)
              + jnp.dot(h_lo, rwl_ref[...], preferred_element_type=F32)) + rb_ref[...]
    ts = logits.shape[0]
    lane_i = lax.broadcasted_iota(I32, (ts, LANES), 1)
    lane = lane_i.astype(F32)
    neg_inf = jnp.float32(-jnp.inf)
    l = jnp.where(lane_i < N_EXPERTS, logits, neg_inf)
    vals, idxs = [], []
    for _ in range(TOP_K):
        m = jnp.max(l, axis=-1, keepdims=True)
        idx = jnp.min(jnp.where(l == m, lane, float(LANES)), axis=-1, keepdims=True)
        vals.append(m)
        idxs.append(idx)
        l = jnp.where(lane == idx, neg_inf, l)
    es = [jnp.exp(v - vals[0]) for v in vals]
    denom = es[0] + es[1] + es[2] + es[3]

    onehot = jnp.zeros((ts, LANES), F32)
    for idx in idxs:
        onehot = onehot + jnp.where(lane == idx, 1.0, 0.0)
    prefix = jnp.dot(tri_ref[...], onehot.astype(BF16), preferred_element_type=F32) + carry_ref[...]
    idx_out = jnp.zeros((ts, LANES), I32)
    gate_out = jnp.zeros((ts, LANES), F32)
    rank_out = jnp.zeros((ts, LANES), I32)
    for k in range(TOP_K):
        rank = jnp.sum(jnp.where(lane == idxs[k], prefix, 0.0), axis=-1, keepdims=True)
        idx_out = jnp.where(lane_i == k, idxs[k].astype(I32), idx_out)
        gate_out = jnp.where(lane_i == k, es[k] / denom, gate_out)
        rank_out = jnp.where(lane_i == k, rank.astype(I32), rank_out)
    idx_ref[0] = idx_out
    gate_ref[0] = gate_out
    rank_ref[0] = rank_out
    carry = carry_ref[...] + jnp.sum(onehot, axis=0, keepdims=True)
    carry_ref[...] = carry
    cnt_ref[...] = carry


def _out_and_route(a_out, b_out, x, g1, sh2, sc2, norm_g, w_out, router_w, router_b):
    B, S, D = x.shape
    ts = OUT_ROWS
    aw = a_out.shape[-1]
    w_bf = w_out.astype(BF16)
    rw = jnp.concatenate([router_w, jnp.zeros((D, LANES - N_EXPERTS), F32)], axis=1)
    rw_top = lax.bitcast_convert_type(lax.bitcast_convert_type(rw, I32) & -65536, F32)
    rw_hi = rw_top.astype(BF16)
    rw_lo = (rw - rw_top).astype(BF16)
    rb =jnp.concatenate([router_b, jnp.zeros((LANES - N_EXPERTS,), F32)]).reshape(1, LANES)
    r = np.arange(ts)
    tri = jnp.asarray(r[:, None] > r[None, :], BF16)
    row = lambda b, i: (b, 0, 0)
    full = lambda b, i: (0, 0)
    tok = lambda w: pl.BlockSpec((1, ts, w), lambda b, i: (b, i, 0))
    return pl.pallas_call(
        _out_kernel,
        grid=(B, S // ts),
        in_specs=[tok(aw), tok(D - aw), tok(D),
                  pl.BlockSpec((1, 1, D), row), pl.BlockSpec((1, 1, D), row),
                  pl.BlockSpec((1, 1, D), row), pl.BlockSpec((1, D), full),
                  pl.BlockSpec((aw, D), full), pl.BlockSpec((D - aw, D), full),
                  pl.BlockSpec((D, LANES), full), pl.BlockSpec((D, LANES), full),
                  pl.BlockSpec((1, LANES), full), pl.BlockSpec((ts, ts), full)],
        out_specs=[tok(D), tok(D), tok(LANES), tok(LANES), tok(LANES),
                   pl.BlockSpec((1, LANES), full)],
        out_shape=[jax.ShapeDtypeStruct((B, S, D), F32), jax.ShapeDtypeStruct((B, S, D), F32),
                   jax.ShapeDtypeStruct((B, S, LANES), I32),
                   jax.ShapeDtypeStruct((B, S, LANES), F32),
                   jax.ShapeDtypeStruct((B, S, LANES), I32),
                   jax.ShapeDtypeStruct((1, LANES), F32)],
        scratch_shapes=[pltpu.VMEM((1, LANES), F32)],
        compiler_params=_cparams(2),
    )(a_out, b_out, x, g1, sh2, sc2, norm_g.reshape(1, D), w_bf[:aw], w_bf[aw:], rw_hi, rw_lo, rb,
      tri)


def _row_copy_wait(src_ref, dst_ref, sem, rows):
    pltpu.make_async_copy(src_ref.at[pl.ds(0, rows), :], dst_ref.at[pl.ds(0, rows), :], sem).wait()


def _dispatch_kernel(fill_ref, dest_ref, h_ref, xbuf_ref, zero_ref, sem, zsem):
    rows = h_ref.shape[0]
    blk = zero_ref.shape[0]
    n_pad, n_used = fill_ref[2 * N_EXPERTS], fill_ref[2 * N_EXPERTS + 1]
    n_blocks = xbuf_ref.shape[0] // blk

    @pl.when(pl.program_id(0) == 0)
    def _():
        zero_ref[...] = jnp.zeros(zero_ref.shape, F32)
        zero_row = zero_ref.at[pl.ds(0, 1), :]

        def per_expert(e, c):
            start = fill_ref[e]

            def per_row(r, c2):
                pltpu.make_async_copy(zero_row, xbuf_ref.at[pl.ds(start + r, 1), :], zsem).start()
                return c2
            return lax.fori_loop(0, fill_ref[N_EXPERTS + e], per_row, c)

        lax.fori_loop(0, N_EXPERTS, per_expert, 0)

        def per_block(b, c):
            pltpu.make_async_copy(zero_ref, xbuf_ref.at[pl.ds(b * blk, blk), :], zsem).start()
            return c

        lax.fori_loop(n_used, n_blocks, per_block, 0)

        def wait_row(r, c):
            pltpu.make_async_copy(zero_row, xbuf_ref.at[pl.ds(0, 1), :], zsem).wait()
            return c

        lax.fori_loop(0, n_pad, wait_row, 0)

        def wait_block(b, c):
            pltpu.make_async_copy(zero_ref, xbuf_ref.at[pl.ds(0, blk), :], zsem).wait()
            return c

        lax.fori_loop(n_used, n_blocks, wait_block, 0)

    def body(r, c):
        for k in range(TOP_K):
            d = dest_ref[0, 0, r * TOP_K + k]
            pltpu.make_async_copy(h_ref.at[pl.ds(r, 1), :], xbuf_ref.at[pl.ds(d, 1), :], sem).start()
        return c

    lax.fori_loop(0, rows, body, 0)
    for _ in range(TOP_K):
        _row_copy_wait(h_ref, xbuf_ref, sem, rows)


def _dispatch(h2, dest, fill, n_rows):
    T, D = h2.shape
    ts = min(DISPATCH_ROWS, T)
    grid_spec = pltpu.PrefetchScalarGridSpec(
        num_scalar_prefetch=1,
        grid=(T // ts,),
        in_specs=[pl.BlockSpec((1, 1, ts * TOP_K), lambda i, f: (i, 0, 0),
                               memory_space=pltpu.SMEM),
                  pl.BlockSpec((ts, D), lambda i, f: (i, 0))],
        out_specs=pl.BlockSpec(memory_space=pl.ANY),
        scratch_shapes=[pltpu.VMEM((MOE_ROWS, D), F32), pltpu.SemaphoreType.DMA(()),
                        pltpu.SemaphoreType.DMA(())],
    )
    return pl.pallas_call(
        _dispatch_kernel,
        grid_spec=grid_spec,
        out_shape=jax.ShapeDtypeStruct((n_rows, D), F32),
        compiler_params=_cparams(1),
    )(fill, dest.reshape(T // ts, 1, ts * TOP_K), h2)


def _moe_kernel(be_ref, nused_ref, x_ref, wg_ref, bg_ref, wu_ref, bu_ref, wd_ref, bd_ref, y_ref,
                wg_s, wu_s, wd_s):
    i = pl.program_id(0)
    used = i < nused_ref[0]
    new_expert = jnp.logical_or(i == 0, be_ref[i] != be_ref[jnp.maximum(i - 1, 0)])

    @pl.when(jnp.logical_and(used, new_expert))
    def _():
        wg_s[...] = wg_ref[0].astype(BF16)
        wu_s[...] = wu_ref[0].astype(BF16)
        wd_s[...] = wd_ref[0].astype(BF16)

    @pl.when(used)
    def _():
        xb = x_ref[...].astype(BF16)
        g = jnp.dot(xb, wg_s[...], preferred_element_type=F32) + bg_ref[0]
        u = jnp.dot(xb, wu_s[...], preferred_element_type=F32) + bu_ref[0]
        g = jnp.minimum(g, SWIGLU_LIMIT)
        u = jnp.clip(u, -SWIGLU_LIMIT, SWIGLU_LIMIT)
        a = g * (1.0 / (1.0 + jnp.exp(-SWIGLU_ALPHA * g))) * (u + 1.0)
        y_ref[...] = jnp.dot(a.astype(BF16), wd_s[...], preferred_element_type=F32) + bd_ref[0]

    @pl.when(jnp.logical_not(used))
    def _():
        y_ref[...] = jnp.zeros(y_ref.shape, F32)


def _moe_experts(xbuf, block_expert, n_used, w_gate, b_gate, w_up, b_up, w_down, b_down):
    R, D = xbuf.shape
    E, _, F = w_gate.shape
    G = MOE_ROWS
    nb = R // G
    last = lambda i, nu: jnp.maximum(jnp.minimum(i, nu[0] - 1), 0)
    blk = lambda i, be, nu: (last(i, nu), 0)
    wsel = lambda i, be, nu: (be[last(i, nu)], 0, 0)
    grid_spec = pltpu.PrefetchScalarGridSpec(
        num_scalar_prefetch=2,
        grid=(nb,),
        in_specs=[pl.BlockSpec((G, D), blk),
                  pl.BlockSpec((1, D, F), wsel), pl.BlockSpec((1, 1, F), wsel),
                  pl.BlockSpec((1, D, F), wsel), pl.BlockSpec((1, 1, F), wsel),
                  pl.BlockSpec((1, F, D), wsel), pl.BlockSpec((1, 1, D), wsel)],
        out_specs=pl.BlockSpec((G, D), lambda i, be, nu: (i, 0)),
        scratch_shapes=[pltpu.VMEM((D, F), BF16), pltpu.VMEM((D, F), BF16),
                        pltpu.VMEM((F, D), BF16)],
    )
    return pl.pallas_call(
        _moe_kernel,
        grid_spec=grid_spec,
        out_shape=jax.ShapeDtypeStruct((R, D), F32),
        compiler_params=_cparams(1),
    )(block_expert, n_used, xbuf, w_gate, b_gate.reshape(E, 1, F),
      w_up, b_up.reshape(E, 1, F), w_down, b_down.reshape(E, 1, D))


def _combine_kernel(dest_ref, gate_ref, x1_ref, g2_ref, ybuf_ref, o_ref, buf_ref, sem):
    rows = x1_ref.shape[0]

    def body(r, c):
        for k in range(TOP_K):
            d = dest_ref[0, 0, r * TOP_K + k]
            pltpu.make_async_copy(ybuf_ref.at[pl.ds(d, 1), :], buf_ref.at[k, pl.ds(r, 1), :],
                                  sem).start()
        return c

    lax.fori_loop(0, rows, body, 0)
    for k in range(TOP_K):
        _row_copy_wait(ybuf_ref, buf_ref.at[k], sem, rows)
    gate = gate_ref[...]
    y = buf_ref[0] * gate[:, 0:1]
    for k in range(1, TOP_K):
        y = y + buf_ref[k] * gate[:, k:k + 1]
    o_ref[...] = x1_ref[...] + g2_ref[0] * y


def _combine(ybuf, dest, gates, x1, g2, seq):
    T, D = x1.shape
    ts = COMBINE_ROWS
    per_seq = seq // ts
    return pl.pallas_call(
        _combine_kernel,
        grid=(T // ts,),
        in_specs=[pl.BlockSpec((1, 1, ts * TOP_K), lambda i: (i, 0, 0), memory_space=pltpu.SMEM),
                  pl.BlockSpec((ts, LANES), lambda i: (i, 0)),
                  pl.BlockSpec((ts, D), lambda i: (i, 0)),
                  pl.BlockSpec((1, 1, D), lambda i: (i // per_seq, 0, 0)),
                  pl.BlockSpec(memory_space=pl.ANY)],
        out_specs=pl.BlockSpec((ts, D), lambda i: (i, 0)),
        out_shape=jax.ShapeDtypeStruct((T, D), F32),
        scratch_shapes=[pltpu.VMEM((TOP_K, ts, D), F32), pltpu.SemaphoreType.DMA(())],
        compiler_params=_cparams(1),
    )(dest.reshape(T // ts, 1, ts * TOP_K), gates, x1, g2, ybuf)


def _layer(x, mod, lambda_init, norm1_g, norm2_g, w_in, w_out, a_q_norm_g, a_k_norm_g, a_lambda,
           a_sub_g, b_q_norm_g, b_k_norm_g, b_kv_norm_g, b_w_uv, router_w, router_b, w_gate,
           b_gate, w_up, b_up, w_down, b_down):
    B, S, D = x.shape
    T = B * S
    sh1, sc1, g1, sh2, sc2, g2 = [m.reshape(B, 1, D) for m in jnp.split(mod, 6, axis=-1)]

    (aq1, aq2, ak1, ak2, bq, bk, ik, iq, blat1, av1, iw), plain_a, plain_b = _project(
        x, sh1, sc1, norm1_g, w_in, a_q_norm_g, a_k_norm_g, b_q_norm_g, b_k_norm_g, b_kv_norm_g)
    a_out = _diff_attention(plain_a, aq1, aq2, ak1, ak2, av1, a_lambda, a_sub_g, lambda_init)
    b_out = _dsa_attention(plain_b, iq, iw, ik, bq, bk, blat1, b_w_uv)

    x1, h2, top_idx, gates, rank, counts = _out_and_route(
        a_out, b_out, x, g1, sh2, sc2, norm2_g, w_out, router_w, router_b)

    G = MOE_ROWS
    counts = counts[0, :N_EXPERTS].astype(I32)
    padded = ((counts + G - 1) // G) * G
    pcum = jnp.cumsum(padded)
    poff = pcum - padded
    nb = (T * TOP_K) // G + N_EXPERTS
    starts = jnp.arange(nb, dtype=I32) * G
    block_expert = jnp.minimum(
        jnp.sum((pcum[None, :] <= starts[:, None]).astype(I32), axis=1), N_EXPERTS - 1)
    n_used = (pcum[-1:] // G).astype(I32)
    top_idx = top_idx.reshape(T, LANES)[:, :TOP_K]
    dest = (poff[top_idx] + rank.reshape(T, LANES)[:, :TOP_K]).astype(I32).reshape(T * TOP_K)

    pad = padded - counts
    fill = jnp.concatenate([poff + counts, pad, jnp.sum(pad, keepdims=True), n_used]).astype(I32)
    xbuf = _dispatch(h2.reshape(T, D), dest, fill, nb * G)
    ybuf = _moe_experts(xbuf, block_expert.astype(I32), n_used, w_gate, b_gate, w_up, b_up,
                        w_down, b_down)
    out = _combine(ybuf, dest, gates.reshape(T, LANES), x1.reshape(T, D), g2, S)
    return out.reshape(B, S, D)


def kernel(x, c, norm1_g, norm2_g, w_ada, b_ada, w_in, w_out, a_q_norm_g, a_k_norm_g, a_lambda,
           a_sub_g, b_q_norm_g, b_k_norm_g, b_kv_norm_g, b_w_uv, router_w, router_b, w_gate,
           b_gate, w_up, b_up, w_down, b_down):
    depth = w_in.shape[0]
    for l in range(depth):
        lambda_init = 0.8 - 0.6 * math.exp(-0.3 * l)
        mod = _ada(c, w_ada[l], b_ada[l])
        x = _layer(x, mod, lambda_init, norm1_g[l], norm2_g[l], w_in[l], w_out[l], a_q_norm_g[l],
                   a_k_norm_g[l], a_lambda[l], a_sub_g[l], b_q_norm_g[l], b_k_norm_g[l],
                   b_kv_norm_g[l], b_w_uv[l], router_w[l], router_b[l], w_gate[l], b_gate[l],
                   w_up[l], b_up[l], w_down[l], b_down[l])
    return x
```

```python
import functools
import math

import numpy as np
import jax
import jax.numpy as jnp
from jax import lax
from jax.experimental import pallas as pl
from jax.experimental.pallas import tpu as pltpu

F32 = jnp.float32
BF16 = jnp.bfloat16
I32 = jnp.int32

CHUNK = 64
HEAD_DIM = 64
ROT_DIM = HEAD_DIM // 4
ROPE_THETA = 500000.0
RMS_EPS = 1e-6
A_HEADS = 4
A_V_DIM = 2 * HEAD_DIM
B_HEADS = 8
B_LATENT = 128
IDX_HEADS = 8
IDX_TOPK_MAX = 256
N_EXPERTS = 32
TOP_K = 4
SWIGLU_LIMIT = 7.0
SWIGLU_ALPHA = 1.702

LANES = 128
INT_MIN = -(2 ** 31)
NEG_BIG = -1e30
VMEM_LIMIT = 56 * 1024 * 1024

NORM_SLACK = 1.01
PLAIN_EXP_MAX_BOUND = 30.0

PROJ_ROWS = 512
ATT_TQ = 512
DSA_TQ = 512
WALK_KEYS = 256
DSA_TK = 512
DSA_HEAD_GROUP = 4
COUNT_ROWS = 128
COUNT_CHAINS = 4
WALK_UNROLL = 4
OUT_ROWS = 512
MOE_ROWS = 512
DISPATCH_ROWS = 1024
COMBINE_ROWS = 512


def _cparams(n_axes):
    return pltpu.CompilerParams(
        dimension_semantics=("arbitrary",) * n_axes, vmem_limit_bytes=VMEM_LIMIT)


def _dot_nt(a, b):
    return lax.dot_general(a, b, (((1,), (1,)), ((), ())), preferred_element_type=F32)


def _ada_kernel(c_ref, w_ref, b_ref, o_ref):
    c = c_ref[...]
    sc = c / (1.0 + jnp.exp(-c))
    o_ref[...] = jnp.dot(sc, w_ref[...], preferred_element_type=F32,
                         precision=lax.Precision.HIGHEST) + b_ref[...]


def _ada(c, w, b):
    B, D = c.shape
    N = w.shape[1]
    return pl.pallas_call(
        _ada_kernel,
        grid=(N // D,),
        in_specs=[pl.BlockSpec((B, D), lambda j: (0, 0)),
                  pl.BlockSpec((D, D), lambda j: (0, j)),
                  pl.BlockSpec((1, D), lambda j: (0, j))],
        out_specs=pl.BlockSpec((B, D), lambda j: (0, j)),
        out_shape=jax.ShapeDtypeStruct((B, N), F32),
        compiler_params=_cparams(1),
    )(c, w, b.reshape(1, N))


C_QK = 0
C_BQ = 1024
C_BKIK = 1536
C_IQ = 1664
C_LAT = 2176
C_AV = 2304
C_IW = 2816
C_END = 2944
N_GAIN = C_IQ


def _group_sumsq(p, bd):
    return jnp.dot((p * p).astype(BF16), bd, preferred_element_type=F32)


def _rope(y, c, s1, s2):
    w = y.shape[1]
    return y * c + pltpu.roll(y, w - ROT_DIM // 2, 1) * s1 + pltpu.roll(y, ROT_DIM // 2, 1) * s2


def _proj_kernel(x_ref, sh_ref, sc_ref, g_ref, w_ref, gain_ref, kb_ref, latg_ref, rc_ref, rs1_ref,
                 rs2_ref, bd_ref,
                 aq1_ref, aq2_ref, ak1_ref, ak2_ref, bq_ref, bk_ref, ik_ref, iq_ref, lat_ref,
                 av_ref, iw_ref):
    x = x_ref[0]
    ms = jnp.mean(x * x, axis=-1, keepdims=True)
    h = x * lax.rsqrt(ms + RMS_EPS) * g_ref[...]
    h = (h * (1.0 + sc_ref[0]) + sh_ref[0]).astype(BF16)

    rc, rs1, rs2 = rc_ref[...], rs1_ref[...], rs2_ref[...]
    bd = bd_ref[...]
    ts = x.shape[0]
    lane = lax.broadcasted_iota(I32, (ts, LANES), 1)
    ones = jnp.ones((ts, LANES), F32)

    def proj(c0, width):
        return jnp.dot(h, w_ref[:, c0:c0 + width], preferred_element_type=F32)

    def normed(p, c0):
        width = p.shape[1]
        ss = _group_sumsq(p, bd[:width, :width])
        return p * lax.rsqrt(ss * (1.0 / HEAD_DIM) + RMS_EPS) * gain_ref[:, c0:c0 + width]

    def store_slots(ref, y, extra, first_head=0):
        for pair in range(y.shape[1] // LANES):
            z = y[:, pair * LANES:(pair + 1) * LANES]
            e = extra[:, pair * LANES:(pair + 1) * LANES]
            even = jnp.where(lane < HEAD_DIM, z,
                             jnp.where(lane == HEAD_DIM, pltpu.roll(e, HEAD_DIM, 1), 0.0))
            odd = jnp.where(lane < HEAD_DIM, pltpu.roll(z, HEAD_DIM, 1),
                            jnp.where(lane == HEAD_DIM, e, 0.0))
            ref[0, first_head + 2 * pair] = even.astype(BF16)
            ref[0, first_head + 2 * pair + 1] = odd.astype(BF16)

    def query(c0):
        y = _rope(normed(proj(c0, 256), c0), rc, rs1, rs2)
        norm = jnp.sqrt(jnp.dot((y * y).astype(BF16), bd, preferred_element_type=F32))
        return y, -norm * kb_ref[:, c0:c0 + 256]

    def key(c0):
        return _rope(normed(proj(c0, 256), c0), rc, rs1, rs2), jnp.ones((ts, 256), F32)

    store_slots(aq1_ref, *query(0))
    store_slots(aq2_ref, *query(256))
    store_slots(ak1_ref, *key(512))
    store_slots(ak2_ref, *key(768))
    for half in range(2):
        store_slots(bq_ref, *query(C_BQ + 256 * half), first_head=4 * half)
        y = _rope(proj(C_IQ + 256 * half, 256), rc, rs1, rs2)
        for j in range(4):
            iq_ref[0, 4 * half + j] = y[:, j * HEAD_DIM:(j + 1) * HEAD_DIM].astype(BF16)

    p = proj(C_BKIK, LANES)
    y = _rope(jnp.where(lane < HEAD_DIM, normed(p, C_BKIK), p),
              rc[:, :LANES], rs1[:, :LANES], rs2[:, :LANES])
    bk_ref[0] = jnp.where(lane < HEAD_DIM, y, jnp.where(lane == HEAD_DIM, 1.0, 0.0)).astype(BF16)
    ik_ref[0] = y[:, HEAD_DIM:].astype(BF16)

    p = proj(C_LAT, LANES)
    ms = jnp.mean(p * p, axis=-1, keepdims=True)
    lat = p * lax.rsqrt(ms + RMS_EPS) * latg_ref[...]
    lat_ref[0] = jnp.concatenate([lat, ones], axis=1).astype(BF16)

    for j in range(A_HEADS):
        av_ref[0, j] = jnp.concatenate([proj(C_AV + j * A_V_DIM, A_V_DIM), ones],
                                       axis=1).astype(BF16)

    p = proj(C_IW, LANES)
    iw_ref[0] = p[:, :IDX_HEADS] * (IDX_HEADS ** -0.5 * HEAD_DIM ** -0.5)


def _rope_tables(S, width):
    half = ROT_DIM // 2
    pos = jnp.arange(S, dtype=F32)
    inv = ROPE_THETA ** (-jnp.arange(0, ROT_DIM, 2, dtype=F32) / ROT_DIM)
    ang = pos[:, None] * inv[None, :]
    cos, sin = jnp.cos(ang), jnp.sin(ang)
    zeros = jnp.zeros((S, HEAD_DIM - ROT_DIM), F32)
    c = jnp.concatenate([cos, cos, zeros + 1.0], axis=1)
    s1 = jnp.concatenate([-sin, jnp.zeros((S, half), F32), zeros], axis=1)
    s2 = jnp.concatenate([jnp.zeros((S, half), F32), sin, zeros], axis=1)
    reps = width // HEAD_DIM
    return tuple(jnp.tile(t, (1, reps)) for t in (c, s1, s2))


def _project(x, sh1, sc1, norm_g, w_in, a_q_g, a_k_g, b_q_g, b_k_g, b_kv_g):
    B, S, D = x.shape
    ts = PROJ_ROWS
    sizes = (256, 256, 256, 256, 512, 512, 64, 128, 512, 64, 8)
    offs = np.concatenate([[0], np.cumsum(sizes)])
    seg = lambda i: w_in[:, offs[i]:offs[i + 1]]
    w_p = jnp.concatenate(
        [seg(0), seg(1), seg(2), seg(3), seg(5), seg(6), seg(9), seg(8), seg(7), seg(4), seg(10),
         jnp.zeros((D, C_END - C_IW - IDX_HEADS), F32)], axis=1).astype(BF16)
    scale = HEAD_DIM ** -0.5
    gain = jnp.concatenate(
        [jnp.tile(a_q_g * scale, 2 * A_HEADS), jnp.tile(a_k_g, 2 * A_HEADS),
         jnp.tile(b_q_g * scale, B_HEADS), b_k_g, jnp.ones((HEAD_DIM,), F32)]).reshape(1, N_GAIN)
    kb_a = 8.0 * NORM_SLACK * jnp.max(jnp.abs(a_k_g))
    kb_b = 8.0 * NORM_SLACK * jnp.max(jnp.abs(b_k_g))
    zeros = lambda n: jnp.zeros((n,), F32)
    kb = jnp.concatenate([zeros(512) + kb_a, zeros(512), zeros(512) + kb_b,
                          zeros(N_GAIN - C_BKIK)]).reshape(1, N_GAIN)
    bound_a = jnp.max(jnp.abs(a_q_g)) * kb_a
    bound_b = jnp.max(jnp.abs(b_q_g)) * kb_b
    rc, rs1, rs2 = _rope_tables(S, 256)
    gid = np.arange(256) // HEAD_DIM
    bd = jnp.asarray(gid[:, None] == gid[None, :], BF16)

    row = lambda b, i: (b, 0, 0)
    full = lambda b, i: (0, 0)
    heads = lambda n, w: pl.BlockSpec((1, n, ts, w), lambda b, i: (b, 0, i, 0))
    flat = lambda w: pl.BlockSpec((1, ts, w), lambda b, i: (b, i, 0))
    hshape = lambda n, w: jax.ShapeDtypeStruct((B, n, S, w), BF16)
    outs = pl.pallas_call(
        _proj_kernel,
        grid=(B, S // ts),
        in_specs=[pl.BlockSpec((1, ts, D), lambda b, i: (b, i, 0)),
                  pl.BlockSpec((1, 1, D), row), pl.BlockSpec((1, 1, D), row),
                  pl.BlockSpec((1, D), full),
                  pl.BlockSpec((D, C_END), full),
                  pl.BlockSpec((1, N_GAIN), full),
                  pl.BlockSpec((1, N_GAIN), full),
                  pl.BlockSpec((1, LANES), full),
                  pl.BlockSpec((ts, 256), lambda b, i: (i, 0)),
                  pl.BlockSpec((ts, 256), lambda b, i: (i, 0)),
                  pl.BlockSpec((ts, 256), lambda b, i: (i, 0)),
                  pl.BlockSpec((256, 256), full)],
        out_specs=[heads(A_HEADS, LANES)] * 4
        + [heads(B_HEADS, LANES), flat(LANES), flat(HEAD_DIM), heads(IDX_HEADS, HEAD_DIM),
           flat(2 * B_LATENT), heads(A_HEADS, 2 * A_V_DIM), flat(IDX_HEADS)],
        out_shape=[hshape(A_HEADS, LANES)] * 4
        + [hshape(B_HEADS, LANES), jax.ShapeDtypeStruct((B, S, LANES), BF16),
           jax.ShapeDtypeStruct((B, S, HEAD_DIM), BF16), hshape(IDX_HEADS, HEAD_DIM),
           jax.ShapeDtypeStruct((B, S, 2 * B_LATENT), BF16), hshape(A_HEADS, 2 * A_V_DIM),
           jax.ShapeDtypeStruct((B, S, IDX_HEADS), F32)],
        compiler_params=_cparams(2),
    )(x, sh1, sc1, norm_g.reshape(1, D), w_p, gain, kb, b_kv_g.reshape(1, B_LATENT), rc, rs1, rs2,
      bd)
    plain_a = (bound_a <= PLAIN_EXP_MAX_BOUND).astype(I32).reshape(1)
    plain_b = (bound_b <= PLAIN_EXP_MAX_BOUND).astype(I32).reshape(1)
    return outs, plain_a, plain_b


def _softmax_init(plain, m_ref, acc_ref):
    acc_ref[...] = jnp.zeros(acc_ref.shape, F32)
    if not plain:
        m_ref[...] = jnp.full(m_ref.shape, NEG_BIG, F32)


def _softmax_step(plain, s, v1, m_ref, acc_ref):
    if plain:
        acc_ref[...] += jnp.dot(jnp.exp(s).astype(BF16), v1, preferred_element_type=F32)
        return
    m_old = m_ref[...]
    m_new = jnp.maximum(m_old, jnp.max(s, axis=-1, keepdims=True))
    p = jnp.exp(s - m_new).astype(BF16)
    acc_ref[...] = (jnp.exp(m_old - m_new) * acc_ref[...]
                    + jnp.dot(p, v1, preferred_element_type=F32))
    m_ref[...] = m_new


def _softmax_result(acc_ref, width):
    acc = acc_ref[...]
    return acc[:, :width] / acc[:, width:width + 1]


def _chunk_mask(q0, k0, tq, tk):
    qc = (q0 + lax.broadcasted_iota(I32, (tq, tk), 0)) // CHUNK
    kc = (k0 + lax.broadcasted_iota(I32, (tq, tk), 1)) // CHUNK
    return kc <= qc


def _diff_attn_kernel(plain_ref, lam_ref, subg_ref, diag_ref, q1_ref, q2_ref, k1_ref, k2_ref, v_ref,
                      o_ref, m1_ref, acc1_ref, m2_ref, acc2_ref, *, lambda_init):
    tq = tk = ATT_TQ
    i = pl.program_id(2)
    n_tiles = i + 1

    def attend(plain):
        q1 = q1_ref[0, 0]
        q2 = q2_ref[0, 0]
        _softmax_init(plain, m1_ref, acc1_ref)
        _softmax_init(plain, m2_ref, acc2_ref)

        def tile(j, masked):
            ks = pl.multiple_of(j * tk, tk)
            v1 = v_ref[0, 0, pl.ds(ks, tk), :]
            s1 = _dot_nt(q1, k1_ref[0, 0, pl.ds(ks, tk), :])
            s2 = _dot_nt(q2, k2_ref[0, 0, pl.ds(ks, tk), :])
            if masked:
                s1 = s1 + diag_ref[...]
                s2 = s2 + diag_ref[...]
            _softmax_step(plain, s1, v1, m1_ref, acc1_ref)
            _softmax_step(plain, s2, v1, m2_ref, acc2_ref)

        def body(j, c):
            tile(j, False)
            return c

        lax.fori_loop(0, n_tiles - 1, body, 0)
        tile(n_tiles - 1, True)

    pl.when(plain_ref[0] == 1)(lambda: attend(True))
    pl.when(plain_ref[0] != 1)(lambda: attend(False))

    lv = lam_ref[...]
    lam = (jnp.exp(jnp.sum(lv[0:1] * lv[1:2], axis=-1, keepdims=True))
           - jnp.exp(jnp.sum(lv[2:3] * lv[3:4], axis=-1, keepdims=True)) + lambda_init)
    o = _softmax_result(acc1_ref, A_V_DIM) - lam * _softmax_result(acc2_ref, A_V_DIM)
    ms = jnp.mean(o * o, axis=-1, keepdims=True)
    o = o * lax.rsqrt(ms + RMS_EPS) * subg_ref[...] * (1.0 - lambda_init)
    o_ref[0] = o.astype(BF16)


def _diff_attention(plain, q1, q2, k1, k2, v1, a_lambda, sub_g, lambda_init):
    B, H, S, dq = q1.shape
    tq = ATT_TQ
    chunk = np.arange(tq) // CHUNK
    diag = jnp.asarray(np.where(chunk[None, :] <= chunk[:, None], 0.0, NEG_BIG), F32)
    qspec = pl.BlockSpec((1, 1, tq, dq), lambda b, h, i, p: (b, h, i, 0))
    kspec = pl.BlockSpec((1, 1, S, dq), lambda b, h, i, p: (b, h, 0, 0))
    col = lambda: pltpu.VMEM((tq, 1), F32)
    acc = lambda: pltpu.VMEM((tq, 2 * A_V_DIM), F32)
    grid_spec = pltpu.PrefetchScalarGridSpec(
        num_scalar_prefetch=1,
        grid=(B, H, S // tq),
        in_specs=[pl.BlockSpec((4, HEAD_DIM), lambda b, h, i, p: (0, 0)),
                  pl.BlockSpec((1, A_V_DIM), lambda b, h, i, p: (0, 0)),
                  pl.BlockSpec((tq, tq), lambda b, h, i, p: (0, 0)),
                  qspec, qspec, kspec, kspec,
                  pl.BlockSpec((1, 1, S, 2 * A_V_DIM), lambda b, h, i, p: (b, h, 0, 0))],
        out_specs=pl.BlockSpec((1, tq, A_V_DIM), lambda b, h, i, p: (b, i, h)),
        scratch_shapes=[col(), acc(), col(), acc()],
    )
    return pl.pallas_call(
        functools.partial(_diff_attn_kernel, lambda_init=lambda_init),
        grid_spec=grid_spec,
        out_shape=jax.ShapeDtypeStruct((B, S, H * A_V_DIM), BF16),
        compiler_params=_cparams(3),
    )(plain, a_lambda, sub_g.reshape(1, A_V_DIM), diag, q1, q2, k1, k2, v1)


def _dsa_kernel(plain_ref, iq_ref, iw_ref, ik_ref, q_ref, k_ref, lat_ref, wuv_ref, tri_ref, o_ref,
                score_ref, score_t_ref, thr_ref, m_ref, acc_ref, *, topk):
    tq, tk = DSA_TQ, DSA_TK
    nh = B_HEADS
    i = pl.program_id(1)
    n_tiles = ((i + 1) * tq + tk - 1) // tk

    hg = DSA_HEAD_GROUP
    iw = iw_ref[0]

    def score_tile(j, masked):
        ks = pl.multiple_of(j * tk, tk)
        ik = ik_ref[0, pl.ds(ks, tk), :]
        score = None
        for g in range(IDX_HEADS // hg):
            iq = iq_ref[0, g * hg:(g + 1) * hg].reshape(hg * tq, HEAD_DIM)
            rel = jnp.maximum(_dot_nt(iq, ik), 0.0).reshape(hg, tq, tk)
            for h in range(hg):
                term = rel[h] * iw[:, g * hg + h:g * hg + h + 1]
                score = term if score is None else score + term
        if masked:
            score = jnp.where(_chunk_mask(i * tq, ks, tq, tk), score, -jnp.inf)
        score_ref[:, pl.ds(ks, tk)] = score
        score_t_ref[pl.ds(ks, tk), :] = score.T

    def score_body(j, c):
        score_tile(j, False)
        return c

    lax.fori_loop(0, n_tiles - 1, score_body, 0)
    score_tile(n_tiles - 1, True)

    def image_to_float(t):
        return pltpu.bitcast(jnp.where(t < 0, t ^ 0x7FFFFFFF, t), F32)

    def count(pred, cand):
        parts = []
        for r0 in range(0, tq, COUNT_ROWS):
            rows = pl.ds(r0, COUNT_ROWS)
            cand_r = cand[r0:r0 + COUNT_ROWS]

            def body(j, acc, rows=rows, cand_r=cand_r):
                ks = pl.multiple_of(j * tk, tk)
                hit = jnp.where(pred(score_ref[rows, pl.ds(ks, tk)], cand_r), 1.0, 0.0)
                for c in range(tk // LANES):
                    acc = acc + hit[:, c * LANES:(c + 1) * LANES]
                return acc
            parts.append(lax.fori_loop(0, n_tiles, body, jnp.zeros((COUNT_ROWS, LANES), F32)))
        acc = jnp.concatenate(parts, axis=0)
        return jnp.sum(acc, axis=-1, keepdims=True)

    ge = lambda s, c: s >= c
    gt = lambda s, c: s > c

    def all_of(flags):
        return (jnp.min(flags) > 0.0).astype(I32)

    def count_t(pred, cand):
        def body(j, acc):
            ks = pl.multiple_of(j * WALK_KEYS, WALK_KEYS)
            hit = jnp.where(pred(score_t_ref[pl.ds(ks, WALK_KEYS), :], cand), 1.0, 0.0)
            return acc + jnp.sum(hit.reshape(-1, COUNT_CHAINS, 8, tq), axis=0)
        acc = lax.fori_loop(0, (i + 1) * (tq // WALK_KEYS), body,
                            jnp.zeros((COUNT_CHAINS, 8, tq), F32))
        return jnp.sum(jnp.sum(acc, axis=0), axis=0, keepdims=True)

    zero = jnp.zeros((1, tq), F32)
    n_ge0 = count_t(ge, zero)
    tie0 = jnp.where(jnp.logical_and(count_t(gt, zero) < topk, n_ge0 > topk), 1.0, 0.0)
    t0 = jnp.where(n_ge0 >= topk, 0, INT_MIN)
    settled0 = jnp.where(n_ge0 == topk, 1.0, tie0)

    def bit_body(state):
        step, t, settled, _ = state
        for _ in range(WALK_UNROLL):
            bit = jnp.where(step < 32, jnp.left_shift(jnp.int32(1), jnp.maximum(31 - step, 0)), 0)
            cand = t + bit
            n_ge = count_t(ge, image_to_float(cand))
            t = jnp.where(settled > 0.0, t, jnp.where(n_ge >= topk, cand, t))
            settled = jnp.where(n_ge == topk, 1.0, settled)
            step = step + 1
        return step, t, settled, all_of(settled)

    def bit_cond(state):
        step, _, _, all_settled = state
        return jnp.logical_and(step < 32, all_settled == 0)

    _, t, _, all_settled = lax.while_loop(
        bit_cond, bit_body, (jnp.int32(1), t0, settled0, all_of(settled0)))
    thr_row = jnp.where(t == INT_MIN, jnp.finfo(F32).min, image_to_float(t))
    thr_ref[...] = jnp.broadcast_to(thr_row, (LANES, tq)).T[:, 0:1]

    def min_where(pred, cand):
        def body(j, acc):
            ks = pl.multiple_of(j * tk, tk)
            s = score_ref[:, pl.ds(ks, tk)]
            s = jnp.where(pred(s, cand), s, jnp.inf)
            for c in range(tk // LANES):
                acc = jnp.minimum(acc, s[:, c * LANES:(c + 1) * LANES])
            return acc
        acc = lax.fori_loop(0, n_tiles, body, jnp.full((tq, LANES), jnp.inf, F32))
        return jnp.min(acc, axis=-1, keepdims=True)

    @pl.when(all_settled == 0)
    def _():
        thr0 = thr_ref[...]
        low = min_where(ge, thr0)
        nxt = min_where(gt, low)
        thr_ref[...] = jnp.where(count(ge, nxt) >= topk, nxt,
                                 jnp.where(low < jnp.inf, low, thr0))

    @pl.when(jnp.logical_or(all_settled == 0, jnp.max(tie0) > 0.0))
    def _():
        thr = thr_ref[...]
        quota = topk - count(gt, thr)

        def tie_body(j, carry):
            sl = pl.ds(pl.multiple_of(j * tk, tk), tk)
            score = score_ref[:, sl]
            eq = score == thr
            prefix = carry + jnp.dot(jnp.where(eq, 1.0, 0.0).astype(BF16), tri_ref[...],
                                     preferred_element_type=F32)
            score_ref[:, sl] = jnp.where(eq & (prefix > quota), -jnp.inf, score)
            return prefix[:, tk - 1:tk]

        lax.fori_loop(0, n_tiles, tie_body, jnp.zeros((tq, 1), F32))

    thr = thr_ref[...]

    def attend(plain):
        _softmax_init(plain, m_ref, acc_ref)

        def attn_body(j, c):
            ks = pl.multiple_of(j * tk, tk)
            k = k_ref[0, pl.ds(ks, tk), :]
            v1 = lat_ref[0, pl.ds(ks, tk), :]
            sel = score_ref[:, pl.ds(ks, tk)] >= thr
            for g in range(nh // hg):
                rows = pl.ds(g * hg * tq, hg * tq)
                q = q_ref[0, g * hg:(g + 1) * hg].reshape(hg * tq, LANES)
                s = _dot_nt(q, k).reshape(hg, tq, tk)
                s = jnp.where(sel[None], s, NEG_BIG).reshape(hg * tq, tk)
                _softmax_step(plain, s, v1, m_ref.at[rows, :], acc_ref.at[rows, :])
            return c

        lax.fori_loop(0, n_tiles, attn_body, 0)

    pl.when(plain_ref[0] == 1)(lambda: attend(True))
    pl.when(plain_ref[0] != 1)(lambda: attend(False))

    for h in range(nh):
        o = _softmax_result(acc_ref.at[pl.ds(h * tq, tq), :], B_LATENT).astype(BF16)
        oh = jnp.dot(o, wuv_ref[h], preferred_element_type=F32)
        o_ref[0, :, h * HEAD_DIM:(h + 1) * HEAD_DIM] = oh.astype(BF16)


def _dsa_attention(plain, iq, iw, ik, q, k, lat1, w_uv):
    B, nh, S, dq = q.shape
    dh = HEAD_DIM
    tq = DSA_TQ
    topk = min(IDX_TOPK_MAX, S // 4)
    col = np.arange(DSA_TK)
    tri = jnp.asarray(col[:, None] <= col[None, :], BF16)
    hspec = lambda w: pl.BlockSpec((1, nh, tq, w), lambda b, i, p: (b, 0, i, 0))
    kspec = lambda w: pl.BlockSpec((1, S, w), lambda b, i, p: (b, 0, 0))
    grid_spec = pltpu.PrefetchScalarGridSpec(
        num_scalar_prefetch=1,
        grid=(B, S // tq),
        in_specs=[hspec(dh),
                  pl.BlockSpec((1, tq, IDX_HEADS), lambda b, i, p: (b, i, 0)),
                  kspec(dh), hspec(dq), kspec(dq), kspec(2 * B_LATENT),
                  pl.BlockSpec((nh, B_LATENT, dh), lambda b, i, p: (0, 0, 0)),
                  pl.BlockSpec((DSA_TK, DSA_TK), lambda b, i, p: (0, 0))],
        out_specs=pl.BlockSpec((1, tq, nh * dh), lambda b, i, p: (b, i, 0)),
        scratch_shapes=[pltpu.VMEM((tq, S), F32),
                        pltpu.VMEM((S, tq), F32),
                        pltpu.VMEM((tq, 1), F32),
                        pltpu.VMEM((nh * tq, 1), F32),
                        pltpu.VMEM((nh * tq, 2 * B_LATENT), F32)],
    )
    return pl.pallas_call(
        functools.partial(_dsa_kernel, topk=topk),
        grid_spec=grid_spec,
        out_shape=jax.ShapeDtypeStruct((B, S, nh * dh), BF16),
        compiler_params=_cparams(2),
    )(plain, iq, iw, ik, q, k, lat1, w_uv.astype(BF16), tri)


def _out_kernel(a_ref, b_ref, x_ref, g1_ref, sh_ref, sc_ref, ng_ref, woa_ref, wob_ref, rwh_ref,
                rwl_ref, rb_ref, tri_ref, x1_ref, h2_ref, idx_ref, gate_ref, rank_ref, cnt_ref,
                carry_ref):
    first = jnp.logical_and(pl.program_id(0) == 0, pl.program_id(1) == 0)

    @pl.when(first)
    def _():
        carry_ref[...] = jnp.zeros(carry_ref.shape, F32)

    mix = (jnp.dot(a_ref[0], woa_ref[...], preferred_element_type=F32)
           + jnp.dot(b_ref[0], wob_ref[...], preferred_element_type=F32))
    x1 = x_ref[0] + g1_ref[0] * mix
    x1_ref[0] = x1
    ms = jnp.mean(x1 * x1, axis=-1, keepdims=True)
    h2 = x1 * lax.rsqrt(ms + RMS_EPS) * ng_ref[...]
    h2 = h2 * (1.0 + sc_ref[0]) + sh_ref[0]
    h2_ref[0] = h2

    h_top = pltpu.bitcast(pltpu.bitcast(h2, I32) & -65536, F32)
    h_hi = h_top.astype(BF16)
    h_lo = (h2 - h_top).astype(BF16)
    logits = (jnp.dot(h_hi, rwh_ref[...], preferred_element_type=F32)
              + jnp.dot(h_hi, rwl_ref[...], preferred_element_type=F32)
              + jnp.dot(h_lo, rwh_ref[...], preferred_element_type=F32)
              + jnp.dot(h_lo, rwl_ref[...], preferred_element_type=F32)) + rb_ref[...]
    ts = logits.shape[0]
    lane_i = lax.broadcasted_iota(I32, (ts, LANES), 1)
    lane = lane_i.astype(F32)
    neg_inf = jnp.float32(-jnp.inf)
    l = jnp.where(lane_i < N_EXPERTS, logits, neg_inf)
    vals, idxs = [], []
    for _ in range(TOP_K):
        m = jnp.max(l, axis=-1, keepdims=True)
        idx = jnp.min(jnp.where(l == m, lane, float(LANES)), axis=-1, keepdims=True)
        vals.append(m)
        idxs.append(idx)
        l = jnp.where(lane == idx, neg_inf, l)
    es = [jnp.exp(v - vals[0]) for v in vals]
    denom = es[0] + es[1] + es[2] + es[3]

    onehot = jnp.zeros((ts, LANES), F32)
    for idx in idxs:
        onehot = onehot + jnp.where(lane == idx, 1.0, 0.0)
    prefix = jnp.dot(tri_ref[...], onehot.astype(BF16), preferred_element_type=F32) + carry_ref[...]
    idx_out = jnp.zeros((ts, LANES), I32)
    gate_out = jnp.zeros((ts, LANES), F32)
    rank_out = jnp.zeros((ts, LANES), I32)
    for k in range(TOP_K):
        rank = jnp.sum(jnp.where(lane == idxs[k], prefix, 0.0), axis=-1, keepdims=True)
        idx_out = jnp.where(lane_i == k, idxs[k].astype(I32), idx_out)
        gate_out = jnp.where(lane_i == k, es[k] / denom, gate_out)
        rank_out = jnp.where(lane_i == k, rank.astype(I32), rank_out)
    idx_ref[0] = idx_out
    gate_ref[0] = gate_out
    rank_ref[0] = rank_out
    carry = carry_ref[...] + jnp.sum(onehot, axis=0, keepdims=True)
    carry_ref[...] = carry
    cnt_ref[...] = carry


def _out_and_route(a_out, b_out, x, g1, sh2, sc2, norm_g, w_out, router_w, router_b):
    B, S, D = x.shape
    ts = OUT_ROWS
    aw = a_out.shape[-1]
    w_bf = w_out.astype(BF16)
    rw = jnp.concatenate([router_w, jnp.zeros((D, LANES - N_EXPERTS), F32)], axis=1)
    rw_top = lax.bitcast_convert_type(lax.bitcast_convert_type(rw, I32) & -65536, F32)
    rw_hi = rw_top.astype(BF16)
    rw_lo = (rw - rw_top).astype(BF16)
    rb =jnp.concatenate([router_b, jnp.zeros((LANES - N_EXPERTS,), F32)]).reshape(1, LANES)
    r = np.arange(ts)
    tri = jnp.asarray(r[:, None] > r[None, :], BF16)
    row = lambda b, i: (b, 0, 0)
    full = lambda b, i: (0, 0)
    tok = lambda w: pl.BlockSpec((1, ts, w), lambda b, i: (b, i, 0))
    return pl.pallas_call(
        _out_kernel,
        grid=(B, S // ts),
        in_specs=[tok(aw), tok(D - aw), tok(D),
                  pl.BlockSpec((1, 1, D), row), pl.BlockSpec((1, 1, D), row),
                  pl.BlockSpec((1, 1, D), row), pl.BlockSpec((1, D), full),
                  pl.BlockSpec((aw, D), full), pl.BlockSpec((D - aw, D), full),
                  pl.BlockSpec((D, LANES), full), pl.BlockSpec((D, LANES), full),
                  pl.BlockSpec((1, LANES), full), pl.BlockSpec((ts, ts), full)],
        out_specs=[tok(D), tok(D), tok(LANES), tok(LANES), tok(LANES),
                   pl.BlockSpec((1, LANES), full)],
        out_shape=[jax.ShapeDtypeStruct((B, S, D), F32), jax.ShapeDtypeStruct((B, S, D), F32),
                   jax.ShapeDtypeStruct((B, S, LANES), I32),
                   jax.ShapeDtypeStruct((B, S, LANES), F32),
                   jax.ShapeDtypeStruct((B, S, LANES), I32),
                   jax.ShapeDtypeStruct((1, LANES), F32)],
        scratch_shapes=[pltpu.VMEM((1, LANES), F32)],
        compiler_params=_cparams(2),
    )(a_out, b_out, x, g1, sh2, sc2, norm_g.reshape(1, D), w_bf[:aw], w_bf[aw:], rw_hi, rw_lo, rb,
      tri)


def _row_copy_wait(src_ref, dst_ref, sem, rows):
    pltpu.make_async_copy(src_ref.at[pl.ds(0, rows), :], dst_ref.at[pl.ds(0, rows), :], sem).wait()


def _dispatch_kernel(fill_ref, dest_ref, h_ref, xbuf_ref, zero_ref, sem, zsem):
    rows = h_ref.shape[0]
    blk = zero_ref.shape[0]
    n_pad, n_used = fill_ref[2 * N_EXPERTS], fill_ref[2 * N_EXPERTS + 1]
    n_blocks = xbuf_ref.shape[0] // blk

    @pl.when(pl.program_id(0) == 0)
    def _():
        zero_ref[...] = jnp.zeros(zero_ref.shape, F32)
        zero_row = zero_ref.at[pl.ds(0, 1), :]

        def per_expert(e, c):
            start = fill_ref[e]

            def per_row(r, c2):
                pltpu.make_async_copy(zero_row, xbuf_ref.at[pl.ds(start + r, 1), :], zsem).start()
                return c2
            return lax.fori_loop(0, fill_ref[N_EXPERTS + e], per_row, c)

        lax.fori_loop(0, N_EXPERTS, per_expert, 0)

        def per_block(b, c):
            pltpu.make_async_copy(zero_ref, xbuf_ref.at[pl.ds(b * blk, blk), :], zsem).start()
            return c

        lax.fori_loop(n_used, n_blocks, per_block, 0)

        def wait_block(b, c):
            pltpu.make_async_copy(zero_ref, xbuf_ref.at[pl.ds(0, blk), :], zsem).wait()
            return c

        def wait_row(r, c):
            pltpu.make_async_copy(zero_row, xbuf_ref.at[pl.ds(0, 1), :], zsem).wait()
            return c

        lax.fori_loop(n_used - n_pad // blk, n_blocks, wait_block, 0)
        lax.fori_loop(0, lax.rem(n_pad, blk), wait_row, 0)

    def body(r, c):
        for k in range(TOP_K):
            d = dest_ref[0, 0, r * TOP_K + k]
            pltpu.make_async_copy(h_ref.at[pl.ds(r, 1), :], xbuf_ref.at[pl.ds(d, 1), :], sem).start()
        return c

    lax.fori_loop(0, rows, body, 0)
    for _ in range(TOP_K):
        _row_copy_wait(h_ref, xbuf_ref, sem, rows)


def _dispatch(h2, dest, fill, n_rows):
    T, D = h2.shape
    ts = min(DISPATCH_ROWS, T)
    grid_spec = pltpu.PrefetchScalarGridSpec(
        num_scalar_prefetch=1,
        grid=(T // ts,),
        in_specs=[pl.BlockSpec((1, 1, ts * TOP_K), lambda i, f: (i, 0, 0),
                               memory_space=pltpu.SMEM),
                  pl.BlockSpec((ts, D), lambda i, f: (i, 0))],
        out_specs=pl.BlockSpec(memory_space=pl.ANY),
        scratch_shapes=[pltpu.VMEM((MOE_ROWS, D), F32), pltpu.SemaphoreType.DMA(()),
                        pltpu.SemaphoreType.DMA(())],
    )
    return pl.pallas_call(
        _dispatch_kernel,
        grid_spec=grid_spec,
        out_shape=jax.ShapeDtypeStruct((n_rows, D), F32),
        compiler_params=_cparams(1),
    )(fill, dest.reshape(T // ts, 1, ts * TOP_K), h2)


def _moe_kernel(be_ref, nused_ref, x_ref, wg_ref, bg_ref, wu_ref, bu_ref, wd_ref, bd_ref, y_ref,
                wg_s, wu_s, wd_s):
    i = pl.program_id(0)
    used = i < nused_ref[0]
    new_expert = jnp.logical_or(i == 0, be_ref[i] != be_ref[jnp.maximum(i - 1, 0)])

    @pl.when(jnp.logical_and(used, new_expert))
    def _():
        wg_s[...] = wg_ref[0].astype(BF16)
        wu_s[...] = wu_ref[0].astype(BF16)
        wd_s[...] = wd_ref[0].astype(BF16)

    @pl.when(used)
    def _():
        xb = x_ref[...].astype(BF16)
        g = jnp.dot(xb, wg_s[...], preferred_element_type=F32) + bg_ref[0]
        u = jnp.dot(xb, wu_s[...], preferred_element_type=F32) + bu_ref[0]
        g = jnp.minimum(g, SWIGLU_LIMIT)
        u = jnp.clip(u, -SWIGLU_LIMIT, SWIGLU_LIMIT)
        a = g * (1.0 / (1.0 + jnp.exp(-SWIGLU_ALPHA * g))) * (u + 1.0)
        y_ref[...] = jnp.dot(a.astype(BF16), wd_s[...], preferred_element_type=F32) + bd_ref[0]

    @pl.when(jnp.logical_not(used))
    def _():
        y_ref[...] = jnp.zeros(y_ref.shape, F32)


def _moe_experts(xbuf, block_expert, n_used, w_gate, b_gate, w_up, b_up, w_down, b_down):
    R, D = xbuf.shape
    E, _, F = w_gate.shape
    G = MOE_ROWS
    nb = R // G
    last = lambda i, nu: jnp.maximum(jnp.minimum(i, nu[0] - 1), 0)
    blk = lambda i, be, nu: (last(i, nu), 0)
    wsel = lambda i, be, nu: (be[last(i, nu)], 0, 0)
    grid_spec = pltpu.PrefetchScalarGridSpec(
        num_scalar_prefetch=2,
        grid=(nb,),
        in_specs=[pl.BlockSpec((G, D), blk),
                  pl.BlockSpec((1, D, F), wsel), pl.BlockSpec((1, 1, F), wsel),
                  pl.BlockSpec((1, D, F), wsel), pl.BlockSpec((1, 1, F), wsel),
                  pl.BlockSpec((1, F, D), wsel), pl.BlockSpec((1, 1, D), wsel)],
        out_specs=pl.BlockSpec((G, D), lambda i, be, nu: (i, 0)),
        scratch_shapes=[pltpu.VMEM((D, F), BF16), pltpu.VMEM((D, F), BF16),
                        pltpu.VMEM((F, D), BF16)],
    )
    return pl.pallas_call(
        _moe_kernel,
        grid_spec=grid_spec,
        out_shape=jax.ShapeDtypeStruct((R, D), F32),
        compiler_params=_cparams(1),
    )(block_expert, n_used, xbuf, w_gate, b_gate.reshape(E, 1, F),
      w_up, b_up.reshape(E, 1, F), w_down, b_down.reshape(E, 1, D))


def _combine_kernel(dest_ref, gate_ref, x1_ref, g2_ref, ybuf_ref, o_ref, buf_ref, sem):
    rows = x1_ref.shape[0]

    def body(r, c):
        for k in range(TOP_K):
            d = dest_ref[0, 0, r * TOP_K + k]
            pltpu.make_async_copy(ybuf_ref.at[pl.ds(d, 1), :], buf_ref.at[k, pl.ds(r, 1), :],
                                  sem).start()
        return c

    lax.fori_loop(0, rows, body, 0)
    for k in range(TOP_K):
        _row_copy_wait(ybuf_ref, buf_ref.at[k], sem, rows)
    gate = gate_ref[...]
    y = buf_ref[0] * gate[:, 0:1]
    for k in range(1, TOP_K):
        y = y + buf_ref[k] * gate[:, k:k + 1]
    o_ref[...] = x1_ref[...] + g2_ref[0] * y


def _combine(ybuf, dest, gates, x1, g2, seq):
    T, D = x1.shape
    ts = COMBINE_ROWS
    per_seq = seq // ts
    return pl.pallas_call(
        _combine_kernel,
        grid=(T // ts,),
        in_specs=[pl.BlockSpec((1, 1, ts * TOP_K), lambda i: (i, 0, 0), memory_space=pltpu.SMEM),
                  pl.BlockSpec((ts, LANES), lambda i: (i, 0)),
                  pl.BlockSpec((ts, D), lambda i: (i, 0)),
                  pl.BlockSpec((1, 1, D), lambda i: (i // per_seq, 0, 0)),
                  pl.BlockSpec(memory_space=pl.ANY)],
        out_specs=pl.BlockSpec((ts, D), lambda i: (i, 0)),
        out_shape=jax.ShapeDtypeStruct((T, D), F32),
        scratch_shapes=[pltpu.VMEM((TOP_K, ts, D), F32), pltpu.SemaphoreType.DMA(())],
        compiler_params=_cparams(1),
    )(dest.reshape(T // ts, 1, ts * TOP_K), gates, x1, g2, ybuf)


def _layer(x, mod, lambda_init, norm1_g, norm2_g, w_in, w_out, a_q_norm_g, a_k_norm_g, a_lambda,
           a_sub_g, b_q_norm_g, b_k_norm_g, b_kv_norm_g, b_w_uv, router_w, router_b, w_gate,
           b_gate, w_up, b_up, w_down, b_down):
    B, S, D = x.shape
    T = B * S
    sh1, sc1, g1, sh2, sc2, g2 = [m.reshape(B, 1, D) for m in jnp.split(mod, 6, axis=-1)]

    (aq1, aq2, ak1, ak2, bq, bk, ik, iq, blat1, av1, iw), plain_a, plain_b = _project(
        x, sh1, sc1, norm1_g, w_in, a_q_norm_g, a_k_norm_g, b_q_norm_g, b_k_norm_g, b_kv_norm_g)
    a_out = _diff_attention(plain_a, aq1, aq2, ak1, ak2, av1, a_lambda, a_sub_g, lambda_init)
    b_out = _dsa_attention(plain_b, iq, iw, ik, bq, bk, blat1, b_w_uv)

    x1, h2, top_idx, gates, rank, counts = _out_and_route(
        a_out, b_out, x, g1, sh2, sc2, norm2_g, w_out, router_w, router_b)

    G = MOE_ROWS
    counts = counts[0, :N_EXPERTS].astype(I32)
    padded = ((counts + G - 1) // G) * G
    pcum = jnp.cumsum(padded)
    poff = pcum - padded
    nb = (T * TOP_K) // G + N_EXPERTS
    starts = jnp.arange(nb, dtype=I32) * G
    block_expert = jnp.minimum(
        jnp.sum((pcum[None, :] <= starts[:, None]).astype(I32), axis=1), N_EXPERTS - 1)
    n_used = (pcum[-1:] // G).astype(I32)
    top_idx = top_idx.reshape(T, LANES)[:, :TOP_K]
    dest = (poff[top_idx] + rank.reshape(T, LANES)[:, :TOP_K]).astype(I32).reshape(T * TOP_K)

    pad = padded - counts
    fill = jnp.concatenate([poff + counts, pad, jnp.sum(pad, keepdims=True), n_used]).astype(I32)
    xbuf = _dispatch(h2.reshape(T, D), dest, fill, nb * G)
    ybuf = _moe_experts(xbuf, block_expert.astype(I32), n_used, w_gate, b_gate, w_up, b_up,
                        w_down, b_down)
    out = _combine(ybuf, dest, gates.reshape(T, LANES), x1.reshape(T, D), g2, S)
    return out.reshape(B, S, D)


def kernel(x, c, norm1_g, norm2_g, w_ada, b_ada, w_in, w_out, a_q_norm_g, a_k_norm_g, a_lambda,
           a_sub_g, b_q_norm_g, b_k_norm_g, b_kv_norm_g, b_w_uv, router_w, router_b, w_gate,
           b_gate, w_up, b_up, w_down, b_down):
    depth = w_in.shape[0]
    for l in range(depth):
        lambda_init = 0.8 - 0.6 * math.exp(-0.3 * l)
        mod = _ada(c, w_ada[l], b_ada[l])
        x = _layer(x, mod, lambda_init, norm1_g[l], norm2_g[l], w_in[l], w_out[l], a_q_norm_g[l],
                   a_k_norm_g[l], a_lambda[l], a_sub_g[l], b_q_norm_g[l], b_k_norm_g[l],
                   b_kv_norm_g[l], b_w_uv[l], router_w[l], router_b[l], w_gate[l], b_gate[l],
                   w_up[l], b_up[l], w_down[l], b_down[l])
    return x
```

```python
import functools
import math

import numpy as np
import jax
import jax.numpy as jnp
from jax import lax
from jax.experimental import pallas as pl
from jax.experimental.pallas import tpu as pltpu

F32 = jnp.float32
BF16 = jnp.bfloat16
I32 = jnp.int32

CHUNK = 64
HEAD_DIM = 64
ROT_DIM = HEAD_DIM // 4
ROPE_THETA = 500000.0
RMS_EPS = 1e-6
A_HEADS = 4
A_V_DIM = 2 * HEAD_DIM
B_HEADS = 8
B_LATENT = 128
IDX_HEADS = 8
IDX_TOPK_MAX = 256
N_EXPERTS = 32
TOP_K = 4
SWIGLU_LIMIT = 7.0
SWIGLU_ALPHA = 1.702

LANES = 128
INT_MIN = -(2 ** 31)
NEG_BIG = -1e30
VMEM_LIMIT = 56 * 1024 * 1024

NORM_SLACK = 1.01
PLAIN_EXP_MAX_BOUND = 30.0

PROJ_ROWS = 512
ATT_TQ = 512
DSA_TQ = 512
WALK_KEYS = 256
DSA_TK = 512
DSA_HEAD_GROUP = 4
COUNT_ROWS = 128
COUNT_CHAINS = 4
WALK_UNROLL = 4
OUT_ROWS = 512
MOE_ROWS = 512
DISPATCH_ROWS = 1024
COMBINE_ROWS = 1024


def _cparams(n_axes):
    return pltpu.CompilerParams(
        dimension_semantics=("arbitrary",) * n_axes, vmem_limit_bytes=VMEM_LIMIT)


def _dot_nt(a, b):
    return lax.dot_general(a, b, (((1,), (1,)), ((), ())), preferred_element_type=F32)


def _ada_kernel(c_ref, w_ref, b_ref, o_ref):
    c = c_ref[...]
    sc = c / (1.0 + jnp.exp(-c))
    o_ref[...] = jnp.dot(sc, w_ref[...], preferred_element_type=F32,
                         precision=lax.Precision.HIGHEST) + b_ref[...]


def _ada(c, w, b):
    B, D = c.shape
    N = w.shape[1]
    return pl.pallas_call(
        _ada_kernel,
        grid=(N // D,),
        in_specs=[pl.BlockSpec((B, D), lambda j: (0, 0)),
                  pl.BlockSpec((D, D), lambda j: (0, j)),
                  pl.BlockSpec((1, D), lambda j: (0, j))],
        out_specs=pl.BlockSpec((B, D), lambda j: (0, j)),
        out_shape=jax.ShapeDtypeStruct((B, N), F32),
        compiler_params=_cparams(1),
    )(c, w, b.reshape(1, N))


C_QK = 0
C_BQ = 1024
C_BKIK = 1536
C_IQ = 1664
C_LAT = 2176
C_AV = 2304
C_IW = 2816
C_END = 2944
N_GAIN = C_IQ


def _group_sumsq(p, bd):
    return jnp.dot((p * p).astype(BF16), bd, preferred_element_type=F32)


def _rope(y, c, s1, s2):
    w = y.shape[1]
    return y * c + pltpu.roll(y, w - ROT_DIM // 2, 1) * s1 + pltpu.roll(y, ROT_DIM // 2, 1) * s2


def _proj_kernel(x_ref, sh_ref, sc_ref, g_ref, w_ref, gain_ref, kb_ref, latg_ref, rc_ref, rs1_ref,
                 rs2_ref, bd_ref,
                 aq1_ref, aq2_ref, ak1_ref, ak2_ref, bq_ref, bk_ref, ik_ref, iq_ref, lat_ref,
                 av_ref, iw_ref):
    x = x_ref[0]
    ms = jnp.mean(x * x, axis=-1, keepdims=True)
    h = x * lax.rsqrt(ms + RMS_EPS) * g_ref[...]
    h = (h * (1.0 + sc_ref[0]) + sh_ref[0]).astype(BF16)

    rc, rs1, rs2 = rc_ref[...], rs1_ref[...], rs2_ref[...]
    bd = bd_ref[...]
    ts = x.shape[0]
    lane = lax.broadcasted_iota(I32, (ts, LANES), 1)
    ones = jnp.ones((ts, LANES), F32)

    def proj(c0, width):
        return jnp.dot(h, w_ref[:, c0:c0 + width], preferred_element_type=F32)

    def normed(p, c0):
        width = p.shape[1]
        ss = _group_sumsq(p, bd[:width, :width])
        return p * lax.rsqrt(ss * (1.0 / HEAD_DIM) + RMS_EPS) * gain_ref[:, c0:c0 + width]

    def store_slots(ref, y, extra, first_head=0):
        for pair in range(y.shape[1] // LANES):
            z = y[:, pair * LANES:(pair + 1) * LANES]
            e = extra[:, pair * LANES:(pair + 1) * LANES]
            even = jnp.where(lane < HEAD_DIM, z,
                             jnp.where(lane == HEAD_DIM, pltpu.roll(e, HEAD_DIM, 1), 0.0))
            odd = jnp.where(lane < HEAD_DIM, pltpu.roll(z, HEAD_DIM, 1),
                            jnp.where(lane == HEAD_DIM, e, 0.0))
            ref[0, first_head + 2 * pair] = even.astype(BF16)
            ref[0, first_head + 2 * pair + 1] = odd.astype(BF16)

    def query(c0):
        y = _rope(normed(proj(c0, 256), c0), rc, rs1, rs2)
        norm = jnp.sqrt(jnp.dot((y * y).astype(BF16), bd, preferred_element_type=F32))
        return y, -norm * kb_ref[:, c0:c0 + 256]

    def key(c0):
        return _rope(normed(proj(c0, 256), c0), rc, rs1, rs2), jnp.ones((ts, 256), F32)

    store_slots(aq1_ref, *query(0))
    store_slots(aq2_ref, *query(256))
    store_slots(ak1_ref, *key(512))
    store_slots(ak2_ref, *key(768))
    for half in range(2):
        store_slots(bq_ref, *query(C_BQ + 256 * half), first_head=4 * half)
        y = _rope(proj(C_IQ + 256 * half, 256), rc, rs1, rs2)
        for j in range(4):
            iq_ref[0, 4 * half + j] = y[:, j * HEAD_DIM:(j + 1) * HEAD_DIM].astype(BF16)

    p = proj(C_BKIK, LANES)
    y = _rope(jnp.where(lane < HEAD_DIM, normed(p, C_BKIK), p),
              rc[:, :LANES], rs1[:, :LANES], rs2[:, :LANES])
    bk_ref[0] = jnp.where(lane < HEAD_DIM, y, jnp.where(lane == HEAD_DIM, 1.0, 0.0)).astype(BF16)
    ik_ref[0] = y[:, HEAD_DIM:].astype(BF16)

    p = proj(C_LAT, LANES)
    ms = jnp.mean(p * p, axis=-1, keepdims=True)
    lat = p * lax.rsqrt(ms + RMS_EPS) * latg_ref[...]
    lat_ref[0] = jnp.concatenate([lat, ones], axis=1).astype(BF16)

    for j in range(A_HEADS):
        av_ref[0, j] = jnp.concatenate([proj(C_AV + j * A_V_DIM, A_V_DIM), ones],
                                       axis=1).astype(BF16)

    p = proj(C_IW, LANES)
    iw_ref[0] = p[:, :IDX_HEADS] * (IDX_HEADS ** -0.5 * HEAD_DIM ** -0.5)


def _rope_tables(S, width):
    half = ROT_DIM // 2
    pos = jnp.arange(S, dtype=F32)
    inv = ROPE_THETA ** (-jnp.arange(0, ROT_DIM, 2, dtype=F32) / ROT_DIM)
    ang = pos[:, None] * inv[None, :]
    cos, sin = jnp.cos(ang), jnp.sin(ang)
    zeros = jnp.zeros((S, HEAD_DIM - ROT_DIM), F32)
    c = jnp.concatenate([cos, cos, zeros + 1.0], axis=1)
    s1 = jnp.concatenate([-sin, jnp.zeros((S, half), F32), zeros], axis=1)
    s2 = jnp.concatenate([jnp.zeros((S, half), F32), sin, zeros], axis=1)
    reps = width // HEAD_DIM
    return tuple(jnp.tile(t, (1, reps)) for t in (c, s1, s2))


def _project(x, sh1, sc1, norm_g, w_in, a_q_g, a_k_g, b_q_g, b_k_g, b_kv_g):
    B, S, D = x.shape
    ts = PROJ_ROWS
    sizes = (256, 256, 256, 256, 512, 512, 64, 128, 512, 64, 8)
    offs = np.concatenate([[0], np.cumsum(sizes)])
    seg = lambda i: w_in[:, offs[i]:offs[i + 1]]
    w_p = jnp.concatenate(
        [seg(0), seg(1), seg(2), seg(3), seg(5), seg(6), seg(9), seg(8), seg(7), seg(4), seg(10),
         jnp.zeros((D, C_END - C_IW - IDX_HEADS), F32)], axis=1).astype(BF16)
    scale = HEAD_DIM ** -0.5
    gain = jnp.concatenate(
        [jnp.tile(a_q_g * scale, 2 * A_HEADS), jnp.tile(a_k_g, 2 * A_HEADS),
         jnp.tile(b_q_g * scale, B_HEADS), b_k_g, jnp.ones((HEAD_DIM,), F32)]).reshape(1, N_GAIN)
    kb_a = 8.0 * NORM_SLACK * jnp.max(jnp.abs(a_k_g))
    kb_b = 8.0 * NORM_SLACK * jnp.max(jnp.abs(b_k_g))
    zeros = lambda n: jnp.zeros((n,), F32)
    kb = jnp.concatenate([zeros(512) + kb_a, zeros(512), zeros(512) + kb_b,
                          zeros(N_GAIN - C_BKIK)]).reshape(1, N_GAIN)
    bound_a = jnp.max(jnp.abs(a_q_g)) * kb_a
    bound_b = jnp.max(jnp.abs(b_q_g)) * kb_b
    rc, rs1, rs2 = _rope_tables(S, 256)
    gid = np.arange(256) // HEAD_DIM
    bd = jnp.asarray(gid[:, None] == gid[None, :], BF16)

    row = lambda b, i: (b, 0, 0)
    full = lambda b, i: (0, 0)
    heads = lambda n, w: pl.BlockSpec((1, n, ts, w), lambda b, i: (b, 0, i, 0))
    flat = lambda w: pl.BlockSpec((1, ts, w), lambda b, i: (b, i, 0))
    hshape = lambda n, w: jax.ShapeDtypeStruct((B, n, S, w), BF16)
    outs = pl.pallas_call(
        _proj_kernel,
        grid=(B, S // ts),
        in_specs=[pl.BlockSpec((1, ts, D), lambda b, i: (b, i, 0)),
                  pl.BlockSpec((1, 1, D), row), pl.BlockSpec((1, 1, D), row),
                  pl.BlockSpec((1, D), full),
                  pl.BlockSpec((D, C_END), full),
                  pl.BlockSpec((1, N_GAIN), full),
                  pl.BlockSpec((1, N_GAIN), full),
                  pl.BlockSpec((1, LANES), full),
                  pl.BlockSpec((ts, 256), lambda b, i: (i, 0)),
                  pl.BlockSpec((ts, 256), lambda b, i: (i, 0)),
                  pl.BlockSpec((ts, 256), lambda b, i: (i, 0)),
                  pl.BlockSpec((256, 256), full)],
        out_specs=[heads(A_HEADS, LANES)] * 4
        + [heads(B_HEADS, LANES), flat(LANES), flat(HEAD_DIM), heads(IDX_HEADS, HEAD_DIM),
           flat(2 * B_LATENT), heads(A_HEADS, 2 * A_V_DIM), flat(IDX_HEADS)],
        out_shape=[hshape(A_HEADS, LANES)] * 4
        + [hshape(B_HEADS, LANES), jax.ShapeDtypeStruct((B, S, LANES), BF16),
           jax.ShapeDtypeStruct((B, S, HEAD_DIM), BF16), hshape(IDX_HEADS, HEAD_DIM),
           jax.ShapeDtypeStruct((B, S, 2 * B_LATENT), BF16), hshape(A_HEADS, 2 * A_V_DIM),
           jax.ShapeDtypeStruct((B, S, IDX_HEADS), F32)],
        compiler_params=_cparams(2),
    )(x, sh1, sc1, norm_g.reshape(1, D), w_p, gain, kb, b_kv_g.reshape(1, B_LATENT), rc, rs1, rs2,
      bd)
    plain_a = (bound_a <= PLAIN_EXP_MAX_BOUND).astype(I32).reshape(1)
    plain_b = (bound_b <= PLAIN_EXP_MAX_BOUND).astype(I32).reshape(1)
    return outs, plain_a, plain_b


def _softmax_init(plain, m_ref, acc_ref):
    acc_ref[...] = jnp.zeros(acc_ref.shape, F32)
    if not plain:
        m_ref[...] = jnp.full(m_ref.shape, NEG_BIG, F32)


def _softmax_step(plain, s, v1, m_ref, acc_ref):
    if plain:
        acc_ref[...] += jnp.dot(jnp.exp(s).astype(BF16), v1, preferred_element_type=F32)
        return
    m_old = m_ref[...]
    m_new = jnp.maximum(m_old, jnp.max(s, axis=-1, keepdims=True))
    p = jnp.exp(s - m_new).astype(BF16)
    acc_ref[...] = (jnp.exp(m_old - m_new) * acc_ref[...]
                    + jnp.dot(p, v1, preferred_element_type=F32))
    m_ref[...] = m_new


def _softmax_result(acc_ref, width):
    acc = acc_ref[...]
    return acc[:, :width] / acc[:, width:width + 1]


def _chunk_mask(q0, k0, tq, tk):
    qc = (q0 + lax.broadcasted_iota(I32, (tq, tk), 0)) // CHUNK
    kc = (k0 + lax.broadcasted_iota(I32, (tq, tk), 1)) // CHUNK
    return kc <= qc


def _diff_attn_kernel(plain_ref, lam_ref, subg_ref, diag_ref, q1_ref, q2_ref, k1_ref, k2_ref, v_ref,
                      o_ref, m1_ref, acc1_ref, m2_ref, acc2_ref, *, lambda_init):
    tq = tk = ATT_TQ
    i = pl.program_id(2)
    n_tiles = i + 1

    def attend(plain):
        q1 = q1_ref[0, 0]
        q2 = q2_ref[0, 0]
        _softmax_init(plain, m1_ref, acc1_ref)
        _softmax_init(plain, m2_ref, acc2_ref)

        def body(j, c):
            ks = pl.multiple_of(j * tk, tk)
            v1 = v_ref[0, 0, pl.ds(ks, tk), :]
            s1 = _dot_nt(q1, k1_ref[0, 0, pl.ds(ks, tk), :])
            s2 = _dot_nt(q2, k2_ref[0, 0, pl.ds(ks, tk), :])
            _softmax_step(plain, s1, v1, m1_ref, acc1_ref)
            _softmax_step(plain, s2, v1, m2_ref, acc2_ref)
            return c

        lax.fori_loop(0, n_tiles - 1, body, 0)

        ks = pl.multiple_of((n_tiles - 1) * tk, tk)
        half = tq // 2
        for r0, nk in ((0, half), (half, tk)):
            rows = pl.ds(r0, half)
            bias = diag_ref[r0:r0 + half, :nk]
            v1 = v_ref[0, 0, pl.ds(ks, nk), :]
            s1 = _dot_nt(q1[r0:r0 + half], k1_ref[0, 0, pl.ds(ks, nk), :]) + bias
            s2 = _dot_nt(q2[r0:r0 + half], k2_ref[0, 0, pl.ds(ks, nk), :]) + bias
            _softmax_step(plain, s1, v1, m1_ref.at[rows, :], acc1_ref.at[rows, :])
            _softmax_step(plain, s2, v1, m2_ref.at[rows, :], acc2_ref.at[rows, :])

    pl.when(plain_ref[0] == 1)(lambda: attend(True))
    pl.when(plain_ref[0] != 1)(lambda: attend(False))

    lv = lam_ref[...]
    lam = (jnp.exp(jnp.sum(lv[0:1] * lv[1:2], axis=-1, keepdims=True))
           - jnp.exp(jnp.sum(lv[2:3] * lv[3:4], axis=-1, keepdims=True)) + lambda_init)
    o = _softmax_result(acc1_ref, A_V_DIM) - lam * _softmax_result(acc2_ref, A_V_DIM)
    ms = jnp.mean(o * o, axis=-1, keepdims=True)
    o = o * lax.rsqrt(ms + RMS_EPS) * subg_ref[...] * (1.0 - lambda_init)
    o_ref[0] = o.astype(BF16)


def _diff_attention(plain, q1, q2, k1, k2, v1, a_lambda, sub_g, lambda_init):
    B, H, S, dq = q1.shape
    tq = ATT_TQ
    chunk = np.arange(tq) // CHUNK
    diag = jnp.asarray(np.where(chunk[None, :] <= chunk[:, None], 0.0, NEG_BIG), F32)
    qspec = pl.BlockSpec((1, 1, tq, dq), lambda b, h, i, p: (b, h, i, 0))
    kspec = pl.BlockSpec((1, 1, S, dq), lambda b, h, i, p: (b, h, 0, 0))
    col = lambda: pltpu.VMEM((tq, 1), F32)
    acc = lambda: pltpu.VMEM((tq, 2 * A_V_DIM), F32)
    grid_spec = pltpu.PrefetchScalarGridSpec(
        num_scalar_prefetch=1,
        grid=(B, H, S // tq),
        in_specs=[pl.BlockSpec((4, HEAD_DIM), lambda b, h, i, p: (0, 0)),
                  pl.BlockSpec((1, A_V_DIM), lambda b, h, i, p: (0, 0)),
                  pl.BlockSpec((tq, tq), lambda b, h, i, p: (0, 0)),
                  qspec, qspec, kspec, kspec,
                  pl.BlockSpec((1, 1, S, 2 * A_V_DIM), lambda b, h, i, p: (b, h, 0, 0))],
        out_specs=pl.BlockSpec((1, tq, A_V_DIM), lambda b, h, i, p: (b, i, h)),
        scratch_shapes=[col(), acc(), col(), acc()],
    )
    return pl.pallas_call(
        functools.partial(_diff_attn_kernel, lambda_init=lambda_init),
        grid_spec=grid_spec,
        out_shape=jax.ShapeDtypeStruct((B, S, H * A_V_DIM), BF16),
        compiler_params=_cparams(3),
    )(plain, a_lambda, sub_g.reshape(1, A_V_DIM), diag, q1, q2, k1, k2, v1)


def _dsa_kernel(plain_ref, iq_ref, iw_ref, ik_ref, q_ref, k_ref, lat_ref, wuv_ref, tri_ref, o_ref,
                score_ref, score_t_ref, thr_ref, m_ref, acc_ref, *, topk):
    tq, tk = DSA_TQ, DSA_TK
    nh = B_HEADS
    i = pl.program_id(1)
    n_tiles = ((i + 1) * tq + tk - 1) // tk

    hg = DSA_HEAD_GROUP
    iw = iw_ref[0]

    def score_tile(j, masked):
        ks = pl.multiple_of(j * tk, tk)
        ik = ik_ref[0, pl.ds(ks, tk), :]
        score = None
        for g in range(IDX_HEADS // hg):
            iq = iq_ref[0, g * hg:(g + 1) * hg].reshape(hg * tq, HEAD_DIM)
            rel = jnp.maximum(_dot_nt(iq, ik), 0.0).reshape(hg, tq, tk)
            for h in range(hg):
                term = rel[h] * iw[:, g * hg + h:g * hg + h + 1]
                score = term if score is None else score + term
        if masked:
            score = jnp.where(_chunk_mask(i * tq, ks, tq, tk), score, -jnp.inf)
        score_ref[:, pl.ds(ks, tk)] = score
        score_t_ref[pl.ds(ks, tk), :] = score.T

    def score_body(j, c):
        score_tile(j, False)
        return c

    lax.fori_loop(0, n_tiles - 1, score_body, 0)
    score_tile(n_tiles - 1, True)

    def image_to_float(t):
        return pltpu.bitcast(jnp.where(t < 0, t ^ 0x7FFFFFFF, t), F32)

    def count(pred, cand):
        parts = []
        for r0 in range(0, tq, COUNT_ROWS):
            rows = pl.ds(r0, COUNT_ROWS)
            cand_r = cand[r0:r0 + COUNT_ROWS]

            def body(j, acc, rows=rows, cand_r=cand_r):
                ks = pl.multiple_of(j * tk, tk)
                hit = jnp.where(pred(score_ref[rows, pl.ds(ks, tk)], cand_r), 1.0, 0.0)
                for c in range(tk // LANES):
                    acc = acc + hit[:, c * LANES:(c + 1) * LANES]
                return acc
            parts.append(lax.fori_loop(0, n_tiles, body, jnp.zeros((COUNT_ROWS, LANES), F32)))
        acc = jnp.concatenate(parts, axis=0)
        return jnp.sum(acc, axis=-1, keepdims=True)

    ge = lambda s, c: s >= c
    gt = lambda s, c: s > c

    def all_of(flags):
        return (jnp.min(flags) > 0.0).astype(I32)

    def count_t(pred, cand):
        def body(j, acc):
            ks = pl.multiple_of(j * WALK_KEYS, WALK_KEYS)
            hit = jnp.where(pred(score_t_ref[pl.ds(ks, WALK_KEYS), :], cand), 1.0, 0.0)
            return acc + jnp.sum(hit.reshape(-1, COUNT_CHAINS, 8, tq), axis=0)
        acc = lax.fori_loop(0, (i + 1) * (tq // WALK_KEYS), body,
                            jnp.zeros((COUNT_CHAINS, 8, tq), F32))
        return jnp.sum(jnp.sum(acc, axis=0), axis=0, keepdims=True)

    zero = jnp.zeros((1, tq), F32)
    n_ge0 = count_t(ge, zero)
    tie0 = jnp.where(jnp.logical_and(count_t(gt, zero) < topk, n_ge0 > topk), 1.0, 0.0)
    t0 = jnp.where(n_ge0 >= topk, 0, INT_MIN)
    settled0 = jnp.where(n_ge0 == topk, 1.0, tie0)

    def bit_body(state):
        step, t, settled, _ = state
        for _ in range(WALK_UNROLL):
            bit = jnp.where(step < 32, jnp.left_shift(jnp.int32(1), jnp.maximum(31 - step, 0)), 0)
            cand = t + bit
            n_ge = count_t(ge, image_to_float(cand))
            t = jnp.where(settled > 0.0, t, jnp.where(n_ge >= topk, cand, t))
            settled = jnp.where(n_ge == topk, 1.0, settled)
            step = step + 1
        return step, t, settled, all_of(settled)

    def bit_cond(state):
        step, _, _, all_settled = state
        return jnp.logical_and(step < 32, all_settled == 0)

    _, t, _, all_settled = lax.while_loop(
        bit_cond, bit_body, (jnp.int32(1), t0, settled0, all_of(settled0)))
    thr_row = jnp.where(t == INT_MIN, jnp.finfo(F32).min, image_to_float(t))
    thr_ref[...] = jnp.broadcast_to(thr_row, (LANES, tq)).T[:, 0:1]

    def min_where(pred, cand):
        def body(j, acc):
            ks = pl.multiple_of(j * tk, tk)
            s = score_ref[:, pl.ds(ks, tk)]
            s = jnp.where(pred(s, cand), s, jnp.inf)
            for c in range(tk // LANES):
                acc = jnp.minimum(acc, s[:, c * LANES:(c + 1) * LANES])
            return acc
        acc = lax.fori_loop(0, n_tiles, body, jnp.full((tq, LANES), jnp.inf, F32))
        return jnp.min(acc, axis=-1, keepdims=True)

    @pl.when(all_settled == 0)
    def _():
        thr0 = thr_ref[...]
        low = min_where(ge, thr0)
        nxt = min_where(gt, low)
        thr_ref[...] = jnp.where(count(ge, nxt) >= topk, nxt,
                                 jnp.where(low < jnp.inf, low, thr0))

    @pl.when(jnp.logical_or(all_settled == 0, jnp.max(tie0) > 0.0))
    def _():
        thr = thr_ref[...]
        quota = topk - count(gt, thr)

        def tie_body(j, carry):
            sl = pl.ds(pl.multiple_of(j * tk, tk), tk)
            score = score_ref[:, sl]
            eq = score == thr
            prefix = carry + jnp.dot(jnp.where(eq, 1.0, 0.0).astype(BF16), tri_ref[...],
                                     preferred_element_type=F32)
            score_ref[:, sl] = jnp.where(eq & (prefix > quota), -jnp.inf, score)
            return prefix[:, tk - 1:tk]

        lax.fori_loop(0, n_tiles, tie_body, jnp.zeros((tq, 1), F32))

    thr = thr_ref[...]

    def attend(plain):
        _softmax_init(plain, m_ref, acc_ref)

        def attn_body(j, c):
            ks = pl.multiple_of(j * tk, tk)
            k = k_ref[0, pl.ds(ks, tk), :]
            v1 = lat_ref[0, pl.ds(ks, tk), :]
            sel = score_ref[:, pl.ds(ks, tk)] >= thr
            for g in range(nh // hg):
                rows = pl.ds(g * hg * tq, hg * tq)
                q = q_ref[0, g * hg:(g + 1) * hg].reshape(hg * tq, LANES)
                s = _dot_nt(q, k).reshape(hg, tq, tk)
                s = jnp.where(sel[None], s, NEG_BIG).reshape(hg * tq, tk)
                _softmax_step(plain, s, v1, m_ref.at[rows, :], acc_ref.at[rows, :])
            return c

        lax.fori_loop(0, n_tiles, attn_body, 0)

    pl.when(plain_ref[0] == 1)(lambda: attend(True))
    pl.when(plain_ref[0] != 1)(lambda: attend(False))

    for h in range(nh):
        o = _softmax_result(acc_ref.at[pl.ds(h * tq, tq), :], B_LATENT).astype(BF16)
        oh = jnp.dot(o, wuv_ref[h], preferred_element_type=F32)
        o_ref[0, :, h * HEAD_DIM:(h + 1) * HEAD_DIM] = oh.astype(BF16)


def _dsa_attention(plain, iq, iw, ik, q, k, lat1, w_uv):
    B, nh, S, dq = q.shape
    dh = HEAD_DIM
    tq = DSA_TQ
    topk = min(IDX_TOPK_MAX, S // 4)
    col = np.arange(DSA_TK)
    tri = jnp.asarray(col[:, None] <= col[None, :], BF16)
    hspec = lambda w: pl.BlockSpec((1, nh, tq, w), lambda b, i, p: (b, 0, i, 0))
    kspec = lambda w: pl.BlockSpec((1, S, w), lambda b, i, p: (b, 0, 0))
    grid_spec = pltpu.PrefetchScalarGridSpec(
        num_scalar_prefetch=1,
        grid=(B, S // tq),
        in_specs=[hspec(dh),
                  pl.BlockSpec((1, tq, IDX_HEADS), lambda b, i, p: (b, i, 0)),
                  kspec(dh), hspec(dq), kspec(dq), kspec(2 * B_LATENT),
                  pl.BlockSpec((nh, B_LATENT, dh), lambda b, i, p: (0, 0, 0)),
                  pl.BlockSpec((DSA_TK, DSA_TK), lambda b, i, p: (0, 0))],
        out_specs=pl.BlockSpec((1, tq, nh * dh), lambda b, i, p: (b, i, 0)),
        scratch_shapes=[pltpu.VMEM((tq, S), F32),
                        pltpu.VMEM((S, tq), F32),
                        pltpu.VMEM((tq, 1), F32),
                        pltpu.VMEM((nh * tq, 1), F32),
                        pltpu.VMEM((nh * tq, 2 * B_LATENT), F32)],
    )
    return pl.pallas_call(
        functools.partial(_dsa_kernel, topk=topk),
        grid_spec=grid_spec,
        out_shape=jax.ShapeDtypeStruct((B, S, nh * dh), BF16),
        compiler_params=_cparams(2),
    )(plain, iq, iw, ik, q, k, lat1, w_uv.astype(BF16), tri)


def _out_kernel(a_ref, b_ref, x_ref, g1_ref, sh_ref, sc_ref, ng_ref, woa_ref, wob_ref, rwh_ref,
                rwl_ref, rb_ref, tri_ref, x1_ref, h2_ref, idx_ref, gate_ref, rank_ref, cnt_ref,
                carry_ref):
    first = jnp.logical_and(pl.program_id(0) == 0, pl.program_id(1) == 0)

    @pl.when(first)
    def _():
        carry_ref[...] = jnp.zeros(carry_ref.shape, F32)

    mix = (jnp.dot(a_ref[0], woa_ref[...], preferred_element_type=F32)
           + jnp.dot(b_ref[0], wob_ref[...], preferred_element_type=F32))
    x1 = x_ref[0] + g1_ref[0] * mix
    x1_ref[0] = x1
    ms = jnp.mean(x1 * x1, axis=-1, keepdims=True)
    h2 = x1 * lax.rsqrt(ms + RMS_EPS) * ng_ref[...]
    h2 = h2 * (1.0 + sc_ref[0]) + sh_ref[0]
    h2_ref[0] = h2

    h_top = pltpu.bitcast(pltpu.bitcast(h2, I32) & -65536, F32)
    h_hi = h_top.astype(BF16)
    h_lo = (h2 - h_top).astype(BF16)
    logits = (jnp.dot(h_hi, rwh_ref[...], preferred_element_type=F32)
              + jnp.dot(h_hi, rwl_ref[...], preferred_element_type=F32)
              + jnp.dot(h_lo, rwh_ref[...], preferred_element_type=F32)
              + jnp.dot(h_lo, rwl_ref[...], preferred_element_type=F32)) + rb_ref[...]
    ts = logits.shape[0]
    lane_i = lax.broadcasted_iota(I32, (ts, LANES), 1)
    lane = lane_i.astype(F32)
    neg_inf = jnp.float32(-jnp.inf)
    l = jnp.where(lane_i < N_EXPERTS, logits, neg_inf)
    vals, idxs = [], []
    for _ in range(TOP_K):
        m = jnp.max(l, axis=-1, keepdims=True)
        idx = jnp.min(jnp.where(l == m, lane, float(LANES)), axis=-1, keepdims=True)
        vals.append(m)
        idxs.append(idx)
        l = jnp.where(lane == idx, neg_inf, l)
    es = [jnp.exp(v - vals[0]) for v in vals]
    denom = es[0] + es[1] + es[2] + es[3]

    onehot = jnp.zeros((ts, LANES), F32)
    for idx in idxs:
        onehot = onehot + jnp.where(lane == idx, 1.0, 0.0)
    prefix = jnp.dot(tri_ref[...], onehot.astype(BF16), preferred_element_type=F32) + carry_ref[...]
    idx_out = jnp.zeros((ts, LANES), I32)
    gate_out = jnp.zeros((ts, LANES), F32)
    rank_out = jnp.zeros((ts, LANES), I32)
    for k in range(TOP_K):
        rank = jnp.sum(jnp.where(lane == idxs[k], prefix, 0.0), axis=-1, keepdims=True)
        idx_out = jnp.where(lane_i == k, idxs[k].astype(I32), idx_out)
        gate_out = jnp.where(lane_i == k, es[k] / denom, gate_out)
        rank_out = jnp.where(lane_i == k, rank.astype(I32), rank_out)
    idx_ref[0] = idx_out
    gate_ref[0] = gate_out
    rank_ref[0] = rank_out
    carry = carry_ref[...] + jnp.sum(onehot, axis=0, keepdims=True)
    carry_ref[...] = carry
    cnt_ref[...] = carry


def _out_and_route(a_out, b_out, x, g1, sh2, sc2, norm_g, w_out, router_w, router_b):
    B, S, D = x.shape
    ts = OUT_ROWS
    aw = a_out.shape[-1]
    w_bf = w_out.astype(BF16)
    rw = jnp.concatenate([router_w, jnp.zeros((D, LANES - N_EXPERTS), F32)], axis=1)
    rw_top = lax.bitcast_convert_type(lax.bitcast_convert_type(rw, I32) & -65536, F32)
    rw_hi = rw_top.astype(BF16)
    rw_lo = (rw - rw_top).astype(BF16)
    rb =jnp.concatenate([router_b, jnp.zeros((LANES - N_EXPERTS,), F32)]).reshape(1, LANES)
    r = np.arange(ts)
    tri = jnp.asarray(r[:, None] > r[None, :], BF16)
    row = lambda b, i: (b, 0, 0)
    full = lambda b, i: (0, 0)
    tok = lambda w: pl.BlockSpec((1, ts, w), lambda b, i: (b, i, 0))
    return pl.pallas_call(
        _out_kernel,
        grid=(B, S // ts),
        in_specs=[tok(aw), tok(D - aw), tok(D),
                  pl.BlockSpec((1, 1, D), row), pl.BlockSpec((1, 1, D), row),
                  pl.BlockSpec((1, 1, D), row), pl.BlockSpec((1, D), full),
                  pl.BlockSpec((aw, D), full), pl.BlockSpec((D - aw, D), full),
                  pl.BlockSpec((D, LANES), full), pl.BlockSpec((D, LANES), full),
                  pl.BlockSpec((1, LANES), full), pl.BlockSpec((ts, ts), full)],
        out_specs=[tok(D), tok(D), tok(LANES), tok(LANES), tok(LANES),
                   pl.BlockSpec((1, LANES), full)],
        out_shape=[jax.ShapeDtypeStruct((B, S, D), F32), jax.ShapeDtypeStruct((B, S, D), F32),
                   jax.ShapeDtypeStruct((B, S, LANES), I32),
                   jax.ShapeDtypeStruct((B, S, LANES), F32),
                   jax.ShapeDtypeStruct((B, S, LANES), I32),
                   jax.ShapeDtypeStruct((1, LANES), F32)],
        scratch_shapes=[pltpu.VMEM((1, LANES), F32)],
        compiler_params=_cparams(2),
    )(a_out, b_out, x, g1, sh2, sc2, norm_g.reshape(1, D), w_bf[:aw], w_bf[aw:], rw_hi, rw_lo, rb,
      tri)


def _row_copy_wait(src_ref, dst_ref, sem, rows):
    pltpu.make_async_copy(src_ref.at[pl.ds(0, rows), :], dst_ref.at[pl.ds(0, rows), :], sem).wait()


def _dispatch_kernel(fill_ref, dest_ref, h_ref, xbuf_ref, zero_ref, sem, zsem):
    rows = h_ref.shape[0]
    blk = zero_ref.shape[0]
    n_pad, n_used = fill_ref[2 * N_EXPERTS], fill_ref[2 * N_EXPERTS + 1]
    n_blocks = xbuf_ref.shape[0] // blk

    @pl.when(pl.program_id(0) == 0)
    def _():
        zero_ref[...] = jnp.zeros(zero_ref.shape, F32)
        zero_row = zero_ref.at[pl.ds(0, 1), :]

        def per_expert(e, c):
            start = fill_ref[e]

            def per_row(r, c2):
                pltpu.make_async_copy(zero_row, xbuf_ref.at[pl.ds(start + r, 1), :], zsem).start()
                return c2
            return lax.fori_loop(0, fill_ref[N_EXPERTS + e], per_row, c)

        lax.fori_loop(0, N_EXPERTS, per_expert, 0)

        def per_block(b, c):
            pltpu.make_async_copy(zero_ref, xbuf_ref.at[pl.ds(b * blk, blk), :], zsem).start()
            return c

        lax.fori_loop(n_used, n_blocks, per_block, 0)

        def wait_block(b, c):
            pltpu.make_async_copy(zero_ref, xbuf_ref.at[pl.ds(0, blk), :], zsem).wait()
            return c

        def wait_row(r, c):
            pltpu.make_async_copy(zero_row, xbuf_ref.at[pl.ds(0, 1), :], zsem).wait()
            return c

        lax.fori_loop(n_used - n_pad // blk, n_blocks, wait_block, 0)
        lax.fori_loop(0, lax.rem(n_pad, blk), wait_row, 0)

    def body(r, c):
        for k in range(TOP_K):
            d = dest_ref[0, 0, r * TOP_K + k]
            pltpu.make_async_copy(h_ref.at[pl.ds(r, 1), :], xbuf_ref.at[pl.ds(d, 1), :], sem).start()
        return c

    lax.fori_loop(0, rows, body, 0)
    for _ in range(TOP_K):
        _row_copy_wait(h_ref, xbuf_ref, sem, rows)


def _dispatch(h2, dest, fill, n_rows):
    T, D = h2.shape
    ts = min(DISPATCH_ROWS, T)
    grid_spec = pltpu.PrefetchScalarGridSpec(
        num_scalar_prefetch=1,
        grid=(T // ts,),
        in_specs=[pl.BlockSpec((1, 1, ts * TOP_K), lambda i, f: (i, 0, 0),
                               memory_space=pltpu.SMEM),
                  pl.BlockSpec((ts, D), lambda i, f: (i, 0))],
        out_specs=pl.BlockSpec(memory_space=pl.ANY),
        scratch_shapes=[pltpu.VMEM((MOE_ROWS, D), F32), pltpu.SemaphoreType.DMA(()),
                        pltpu.SemaphoreType.DMA(())],
    )
    return pl.pallas_call(
        _dispatch_kernel,
        grid_spec=grid_spec,
        out_shape=jax.ShapeDtypeStruct((n_rows, D), F32),
        compiler_params=_cparams(1),
    )(fill, dest.reshape(T // ts, 1, ts * TOP_K), h2)


def _moe_kernel(be_ref, nused_ref, x_ref, wg_ref, bg_ref, wu_ref, bu_ref, wd_ref, bd_ref, y_ref,
                wg_s, wu_s, wd_s):
    i = pl.program_id(0)
    used = i < nused_ref[0]
    new_expert = jnp.logical_or(i == 0, be_ref[i] != be_ref[jnp.maximum(i - 1, 0)])

    @pl.when(jnp.logical_and(used, new_expert))
    def _():
        wg_s[...] = wg_ref[0].astype(BF16)
        wu_s[...] = wu_ref[0].astype(BF16)
        wd_s[...] = wd_ref[0].astype(BF16)

    @pl.when(used)
    def _():
        xb = x_ref[...].astype(BF16)
        g = jnp.dot(xb, wg_s[...], preferred_element_type=F32) + bg_ref[0]
        u = jnp.dot(xb, wu_s[...], preferred_element_type=F32) + bu_ref[0]
        g = jnp.minimum(g, SWIGLU_LIMIT)
        u = jnp.clip(u, -SWIGLU_LIMIT, SWIGLU_LIMIT)
        a = g * (1.0 / (1.0 + jnp.exp(-SWIGLU_ALPHA * g))) * (u + 1.0)
        y_ref[...] = jnp.dot(a.astype(BF16), wd_s[...], preferred_element_type=F32) + bd_ref[0]

    @pl.when(jnp.logical_not(used))
    def _():
        y_ref[...] = jnp.zeros(y_ref.shape, F32)


def _moe_experts(xbuf, block_expert, n_used, w_gate, b_gate, w_up, b_up, w_down, b_down):
    R, D = xbuf.shape
    E, _, F = w_gate.shape
    G = MOE_ROWS
    nb = R // G
    last = lambda i, nu: jnp.maximum(jnp.minimum(i, nu[0] - 1), 0)
    blk = lambda i, be, nu: (last(i, nu), 0)
    wsel = lambda i, be, nu: (be[last(i, nu)], 0, 0)
    grid_spec = pltpu.PrefetchScalarGridSpec(
        num_scalar_prefetch=2,
        grid=(nb,),
        in_specs=[pl.BlockSpec((G, D), blk),
                  pl.BlockSpec((1, D, F), wsel), pl.BlockSpec((1, 1, F), wsel),
                  pl.BlockSpec((1, D, F), wsel), pl.BlockSpec((1, 1, F), wsel),
                  pl.BlockSpec((1, F, D), wsel), pl.BlockSpec((1, 1, D), wsel)],
        out_specs=pl.BlockSpec((G, D), lambda i, be, nu: (i, 0)),
        scratch_shapes=[pltpu.VMEM((D, F), BF16), pltpu.VMEM((D, F), BF16),
                        pltpu.VMEM((F, D), BF16)],
    )
    return pl.pallas_call(
        _moe_kernel,
        grid_spec=grid_spec,
        out_shape=jax.ShapeDtypeStruct((R, D), F32),
        compiler_params=_cparams(1),
    )(block_expert, n_used, xbuf, w_gate, b_gate.reshape(E, 1, F),
      w_up, b_up.reshape(E, 1, F), w_down, b_down.reshape(E, 1, D))


def _combine_kernel(dest_ref, gate_ref, x1_ref, g2_ref, ybuf_ref, o_ref, buf_ref, sem):
    rows = x1_ref.shape[0]

    def body(r, c):
        for k in range(TOP_K):
            d = dest_ref[0, 0, r * TOP_K + k]
            pltpu.make_async_copy(ybuf_ref.at[pl.ds(d, 1), :], buf_ref.at[k, pl.ds(r, 1), :],
                                  sem).start()
        return c

    lax.fori_loop(0, rows, body, 0)
    for k in range(TOP_K):
        _row_copy_wait(ybuf_ref, buf_ref.at[k], sem, rows)
    gate = gate_ref[...]
    y = buf_ref[0] * gate[:, 0:1]
    for k in range(1, TOP_K):
        y = y + buf_ref[k] * gate[:, k:k + 1]
    o_ref[...] = x1_ref[...] + g2_ref[0] * y


def _combine(ybuf, dest, gates, x1, g2, seq):
    T, D = x1.shape
    ts = COMBINE_ROWS
    per_seq = seq // ts
    return pl.pallas_call(
        _combine_kernel,
        grid=(T // ts,),
        in_specs=[pl.BlockSpec((1, 1, ts * TOP_K), lambda i: (i, 0, 0), memory_space=pltpu.SMEM),
                  pl.BlockSpec((ts, LANES), lambda i: (i, 0)),
                  pl.BlockSpec((ts, D), lambda i: (i, 0)),
                  pl.BlockSpec((1, 1, D), lambda i: (i // per_seq, 0, 0)),
                  pl.BlockSpec(memory_space=pl.ANY)],
        out_specs=pl.BlockSpec((ts, D), lambda i: (i, 0)),
        out_shape=jax.ShapeDtypeStruct((T, D), F32),
        scratch_shapes=[pltpu.VMEM((TOP_K, ts, D), F32), pltpu.SemaphoreType.DMA(())],
        compiler_params=_cparams(1),
    )(dest.reshape(T // ts, 1, ts * TOP_K), gates, x1, g2, ybuf)


def _layer(x, mod, lambda_init, norm1_g, norm2_g, w_in, w_out, a_q_norm_g, a_k_norm_g, a_lambda,
           a_sub_g, b_q_norm_g, b_k_norm_g, b_kv_norm_g, b_w_uv, router_w, router_b, w_gate,
           b_gate, w_up, b_up, w_down, b_down):
    B, S, D = x.shape
    T = B * S
    sh1, sc1, g1, sh2, sc2, g2 = [m.reshape(B, 1, D) for m in jnp.split(mod, 6, axis=-1)]

    (aq1, aq2, ak1, ak2, bq, bk, ik, iq, blat1, av1, iw), plain_a, plain_b = _project(
        x, sh1, sc1, norm1_g, w_in, a_q_norm_g, a_k_norm_g, b_q_norm_g, b_k_norm_g, b_kv_norm_g)
    a_out = _diff_attention(plain_a, aq1, aq2, ak1, ak2, av1, a_lambda, a_sub_g, lambda_init)
    b_out = _dsa_attention(plain_b, iq, iw, ik, bq, bk, blat1, b_w_uv)

    x1, h2, top_idx, gates, rank, counts = _out_and_route(
        a_out, b_out, x, g1, sh2, sc2, norm2_g, w_out, router_w, router_b)

    G = MOE_ROWS
    counts = counts[0, :N_EXPERTS].astype(I32)
    padded = ((counts + G - 1) // G) * G
    pcum = jnp.cumsum(padded)
    poff = pcum - padded
    nb = (T * TOP_K) // G + N_EXPERTS
    starts = jnp.arange(nb, dtype=I32) * G
    block_expert = jnp.minimum(
        jnp.sum((pcum[None, :] <= starts[:, None]).astype(I32), axis=1), N_EXPERTS - 1)
    n_used = (pcum[-1:] // G).astype(I32)
    top_idx = top_idx.reshape(T, LANES)[:, :TOP_K]
    dest = (poff[top_idx] + rank.reshape(T, LANES)[:, :TOP_K]).astype(I32).reshape(T * TOP_K)

    pad = padded - counts
    fill = jnp.concatenate([poff + counts, pad, jnp.sum(pad, keepdims=True), n_used]).astype(I32)
    xbuf = _dispatch(h2.reshape(T, D), dest, fill, nb * G)
    ybuf = _moe_experts(xbuf, block_expert.astype(I32), n_used, w_gate, b_gate, w_up, b_up,
                        w_down, b_down)
    out = _combine(ybuf, dest, gates.reshape(T, LANES), x1.reshape(T, D), g2, S)
    return out.reshape(B, S, D)


def kernel(x, c, norm1_g, norm2_g, w_ada, b_ada, w_in, w_out, a_q_norm_g, a_k_norm_g, a_lambda,
           a_sub_g, b_q_norm_g, b_k_norm_g, b_kv_norm_g, b_w_uv, router_w, router_b, w_gate,
           b_gate, w_up, b_up, w_down, b_down):
    depth = w_in.shape[0]
    for l in range(depth):
        lambda_init = 0.8 - 0.6 * math.exp(-0.3 * l)
        mod = _ada(c, w_ada[l], b_ada[l])
        x = _layer(x, mod, lambda_init, norm1_g[l], norm2_g[l], w_in[l], w_out[l], a_q_norm_g[l],
                   a_k_norm_g[l], a_lambda[l], a_sub_g[l], b_q_norm_g[l], b_k_norm_g[l],
                   b_kv_norm_g[l], b_w_uv[l], router_w[l], router_b[l], w_gate[l], b_gate[l],
                   w_up[l], b_up[l], w_down[l], b_down[l])
    return x
```

```python
import functools
import math

import numpy as np
import jax
import jax.numpy as jnp
from jax import lax
from jax.experimental import pallas as pl
from jax.experimental.pallas import tpu as pltpu

F32 = jnp.float32
BF16 = jnp.bfloat16
I32 = jnp.int32

CHUNK = 64
HEAD_DIM = 64
ROT_DIM = HEAD_DIM // 4
ROPE_THETA = 500000.0
RMS_EPS = 1e-6
A_HEADS = 4
A_V_DIM = 2 * HEAD_DIM
B_HEADS = 8
B_LATENT = 128
IDX_HEADS = 8
IDX_TOPK_MAX = 256
N_EXPERTS = 32
TOP_K = 4
SWIGLU_LIMIT = 7.0
SWIGLU_ALPHA = 1.702

LANES = 128
INT_MIN = -(2 ** 31)
NEG_BIG = -1e30
VMEM_LIMIT = 56 * 1024 * 1024

NORM_SLACK = 1.01
PLAIN_EXP_MAX_BOUND = 30.0

PROJ_ROWS = 512
ATT_TQ = 512
DSA_TQ = 512
WALK_KEYS = 256
DSA_TK = 512
DSA_HEAD_GROUP = 4
COUNT_ROWS = 128
COUNT_CHAINS = 4
WALK_UNROLL = 4
OUT_ROWS = 512
MOE_ROWS = 1024
DISPATCH_ROWS = 1024
COMBINE_ROWS = 1024


def _cparams(n_axes):
    return pltpu.CompilerParams(
        dimension_semantics=("arbitrary",) * n_axes, vmem_limit_bytes=VMEM_LIMIT)


def _dot_nt(a, b):
    return lax.dot_general(a, b, (((1,), (1,)), ((), ())), preferred_element_type=F32)


def _ada_kernel(c_ref, w_ref, b_ref, o_ref):
    c = c_ref[...]
    sc = c / (1.0 + jnp.exp(-c))
    o_ref[...] = jnp.dot(sc, w_ref[...], preferred_element_type=F32,
                         precision=lax.Precision.HIGHEST) + b_ref[...]


def _ada(c, w, b):
    B, D = c.shape
    N = w.shape[1]
    return pl.pallas_call(
        _ada_kernel,
        grid=(N // D,),
        in_specs=[pl.BlockSpec((B, D), lambda j: (0, 0)),
                  pl.BlockSpec((D, D), lambda j: (0, j)),
                  pl.BlockSpec((1, D), lambda j: (0, j))],
        out_specs=pl.BlockSpec((B, D), lambda j: (0, j)),
        out_shape=jax.ShapeDtypeStruct((B, N), F32),
        compiler_params=_cparams(1),
    )(c, w, b.reshape(1, N))


C_QK = 0
C_BQ = 1024
C_BKIK = 1536
C_IQ = 1664
C_LAT = 2176
C_AV = 2304
C_IW = 2816
C_END = 2944
N_GAIN = C_IQ


def _group_sumsq(p, bd):
    return jnp.dot((p * p).astype(BF16), bd, preferred_element_type=F32)


def _rope(y, c, s1, s2):
    w = y.shape[1]
    return y * c + pltpu.roll(y, w - ROT_DIM // 2, 1) * s1 + pltpu.roll(y, ROT_DIM // 2, 1) * s2


def _proj_kernel(x_ref, sh_ref, sc_ref, g_ref, w_ref, gain_ref, kb_ref, latg_ref, rc_ref, rs1_ref,
                 rs2_ref, bd_ref,
                 aq1_ref, aq2_ref, ak1_ref, ak2_ref, bq_ref, bk_ref, ik_ref, iq_ref, lat_ref,
                 av_ref, iw_ref):
    x = x_ref[0]
    ms = jnp.mean(x * x, axis=-1, keepdims=True)
    h = x * lax.rsqrt(ms + RMS_EPS) * g_ref[...]
    h = (h * (1.0 + sc_ref[0]) + sh_ref[0]).astype(BF16)

    rc, rs1, rs2 = rc_ref[...], rs1_ref[...], rs2_ref[...]
    bd = bd_ref[...]
    ts = x.shape[0]
    lane = lax.broadcasted_iota(I32, (ts, LANES), 1)
    ones = jnp.ones((ts, LANES), F32)

    def proj(c0, width):
        return jnp.dot(h, w_ref[:, c0:c0 + width], preferred_element_type=F32)

    def normed(p, c0):
        width = p.shape[1]
        ss = _group_sumsq(p, bd[:width, :width])
        return p * lax.rsqrt(ss * (1.0 / HEAD_DIM) + RMS_EPS) * gain_ref[:, c0:c0 + width]

    def store_slots(ref, y, extra, first_head=0):
        for pair in range(y.shape[1] // LANES):
            z = y[:, pair * LANES:(pair + 1) * LANES]
            e = extra[:, pair * LANES:(pair + 1) * LANES]
            even = jnp.where(lane < HEAD_DIM, z,
                             jnp.where(lane == HEAD_DIM, pltpu.roll(e, HEAD_DIM, 1), 0.0))
            odd = jnp.where(lane < HEAD_DIM, pltpu.roll(z, HEAD_DIM, 1),
                            jnp.where(lane == HEAD_DIM, e, 0.0))
            ref[0, first_head + 2 * pair] = even.astype(BF16)
            ref[0, first_head + 2 * pair + 1] = odd.astype(BF16)

    def query(c0):
        y = _rope(normed(proj(c0, 256), c0), rc, rs1, rs2)
        norm = jnp.sqrt(jnp.dot((y * y).astype(BF16), bd, preferred_element_type=F32))
        return y, -norm * kb_ref[:, c0:c0 + 256]

    def key(c0):
        return _rope(normed(proj(c0, 256), c0), rc, rs1, rs2), jnp.ones((ts, 256), F32)

    store_slots(aq1_ref, *query(0))
    store_slots(aq2_ref, *query(256))
    store_slots(ak1_ref, *key(512))
    store_slots(ak2_ref, *key(768))
    for half in range(2):
        store_slots(bq_ref, *query(C_BQ + 256 * half), first_head=4 * half)
        y = _rope(proj(C_IQ + 256 * half, 256), rc, rs1, rs2)
        for j in range(4):
            iq_ref[0, 4 * half + j] = y[:, j * HEAD_DIM:(j + 1) * HEAD_DIM].astype(BF16)

    p = proj(C_BKIK, LANES)
    y = _rope(jnp.where(lane < HEAD_DIM, normed(p, C_BKIK), p),
              rc[:, :LANES], rs1[:, :LANES], rs2[:, :LANES])
    bk_ref[0] = jnp.where(lane < HEAD_DIM, y, jnp.where(lane == HEAD_DIM, 1.0, 0.0)).astype(BF16)
    ik_ref[0] = y[:, HEAD_DIM:].astype(BF16)

    p = proj(C_LAT, LANES)
    ms = jnp.mean(p * p, axis=-1, keepdims=True)
    lat = p * lax.rsqrt(ms + RMS_EPS) * latg_ref[...]
    lat_ref[0] = jnp.concatenate([lat, ones], axis=1).astype(BF16)

    for j in range(A_HEADS):
        av_ref[0, j] = jnp.concatenate([proj(C_AV + j * A_V_DIM, A_V_DIM), ones],
                                       axis=1).astype(BF16)

    p = proj(C_IW, LANES)
    iw_ref[0] = p[:, :IDX_HEADS] * (IDX_HEADS ** -0.5 * HEAD_DIM ** -0.5)


def _rope_tables(S, width):
    half = ROT_DIM // 2
    pos = jnp.arange(S, dtype=F32)
    inv = ROPE_THETA ** (-jnp.arange(0, ROT_DIM, 2, dtype=F32) / ROT_DIM)
    ang = pos[:, None] * inv[None, :]
    cos, sin = jnp.cos(ang), jnp.sin(ang)
    zeros = jnp.zeros((S, HEAD_DIM - ROT_DIM), F32)
    c = jnp.concatenate([cos, cos, zeros + 1.0], axis=1)
    s1 = jnp.concatenate([-sin, jnp.zeros((S, half), F32), zeros], axis=1)
    s2 = jnp.concatenate([jnp.zeros((S, half), F32), sin, zeros], axis=1)
    reps = width // HEAD_DIM
    return tuple(jnp.tile(t, (1, reps)) for t in (c, s1, s2))


def _project(x, sh1, sc1, norm_g, w_in, a_q_g, a_k_g, b_q_g, b_k_g, b_kv_g):
    B, S, D = x.shape
    ts = PROJ_ROWS
    sizes = (256, 256, 256, 256, 512, 512, 64, 128, 512, 64, 8)
    offs = np.concatenate([[0], np.cumsum(sizes)])
    seg = lambda i: w_in[:, offs[i]:offs[i + 1]]
    w_p = jnp.concatenate(
        [seg(0), seg(1), seg(2), seg(3), seg(5), seg(6), seg(9), seg(8), seg(7), seg(4), seg(10),
         jnp.zeros((D, C_END - C_IW - IDX_HEADS), F32)], axis=1).astype(BF16)
    scale = HEAD_DIM ** -0.5
    gain = jnp.concatenate(
        [jnp.tile(a_q_g * scale, 2 * A_HEADS), jnp.tile(a_k_g, 2 * A_HEADS),
         jnp.tile(b_q_g * scale, B_HEADS), b_k_g, jnp.ones((HEAD_DIM,), F32)]).reshape(1, N_GAIN)
    kb_a = 8.0 * NORM_SLACK * jnp.max(jnp.abs(a_k_g))
    kb_b = 8.0 * NORM_SLACK * jnp.max(jnp.abs(b_k_g))
    zeros = lambda n: jnp.zeros((n,), F32)
    kb = jnp.concatenate([zeros(512) + kb_a, zeros(512), zeros(512) + kb_b,
                          zeros(N_GAIN - C_BKIK)]).reshape(1, N_GAIN)
    bound_a = jnp.max(jnp.abs(a_q_g)) * kb_a
    bound_b = jnp.max(jnp.abs(b_q_g)) * kb_b
    rc, rs1, rs2 = _rope_tables(S, 256)
    gid = np.arange(256) // HEAD_DIM
    bd = jnp.asarray(gid[:, None] == gid[None, :], BF16)

    row = lambda b, i: (b, 0, 0)
    full = lambda b, i: (0, 0)
    heads = lambda n, w: pl.BlockSpec((1, n, ts, w), lambda b, i: (b, 0, i, 0))
    flat = lambda w: pl.BlockSpec((1, ts, w), lambda b, i: (b, i, 0))
    hshape = lambda n, w: jax.ShapeDtypeStruct((B, n, S, w), BF16)
    outs = pl.pallas_call(
        _proj_kernel,
        grid=(B, S // ts),
        in_specs=[pl.BlockSpec((1, ts, D), lambda b, i: (b, i, 0)),
                  pl.BlockSpec((1, 1, D), row), pl.BlockSpec((1, 1, D), row),
                  pl.BlockSpec((1, D), full),
                  pl.BlockSpec((D, C_END), full),
                  pl.BlockSpec((1, N_GAIN), full),
                  pl.BlockSpec((1, N_GAIN), full),
                  pl.BlockSpec((1, LANES), full),
                  pl.BlockSpec((ts, 256), lambda b, i: (i, 0)),
                  pl.BlockSpec((ts, 256), lambda b, i: (i, 0)),
                  pl.BlockSpec((ts, 256), lambda b, i: (i, 0)),
                  pl.BlockSpec((256, 256), full)],
        out_specs=[heads(A_HEADS, LANES)] * 4
        + [heads(B_HEADS, LANES), flat(LANES), flat(HEAD_DIM), heads(IDX_HEADS, HEAD_DIM),
           flat(2 * B_LATENT), heads(A_HEADS, 2 * A_V_DIM), flat(IDX_HEADS)],
        out_shape=[hshape(A_HEADS, LANES)] * 4
        + [hshape(B_HEADS, LANES), jax.ShapeDtypeStruct((B, S, LANES), BF16),
           jax.ShapeDtypeStruct((B, S, HEAD_DIM), BF16), hshape(IDX_HEADS, HEAD_DIM),
           jax.ShapeDtypeStruct((B, S, 2 * B_LATENT), BF16), hshape(A_HEADS, 2 * A_V_DIM),
           jax.ShapeDtypeStruct((B, S, IDX_HEADS), F32)],
        compiler_params=_cparams(2),
    )(x, sh1, sc1, norm_g.reshape(1, D), w_p, gain, kb, b_kv_g.reshape(1, B_LATENT), rc, rs1, rs2,
      bd)
    plain_a = (bound_a <= PLAIN_EXP_MAX_BOUND).astype(I32).reshape(1)
    plain_b = (bound_b <= PLAIN_EXP_MAX_BOUND).astype(I32).reshape(1)
    return outs, plain_a, plain_b


def _softmax_init(plain, m_ref, acc_ref):
    acc_ref[...] = jnp.zeros(acc_ref.shape, F32)
    if not plain:
        m_ref[...] = jnp.full(m_ref.shape, NEG_BIG, F32)


def _softmax_step(plain, s, v1, m_ref, acc_ref):
    if plain:
        acc_ref[...] += jnp.dot(jnp.exp(s).astype(BF16), v1, preferred_element_type=F32)
        return
    m_old = m_ref[...]
    m_new = jnp.maximum(m_old, jnp.max(s, axis=-1, keepdims=True))
    p = jnp.exp(s - m_new).astype(BF16)
    acc_ref[...] = (jnp.exp(m_old - m_new) * acc_ref[...]
                    + jnp.dot(p, v1, preferred_element_type=F32))
    m_ref[...] = m_new


def _softmax_result(acc_ref, width):
    acc = acc_ref[...]
    return acc[:, :width] / acc[:, width:width + 1]


def _chunk_mask(q0, k0, tq, tk):
    qc = (q0 + lax.broadcasted_iota(I32, (tq, tk), 0)) // CHUNK
    kc = (k0 + lax.broadcasted_iota(I32, (tq, tk), 1)) // CHUNK
    return kc <= qc


def _diff_attn_kernel(plain_ref, lam_ref, subg_ref, diag_ref, q1_ref, q2_ref, k1_ref, k2_ref, v_ref,
                      o_ref, m1_ref, acc1_ref, m2_ref, acc2_ref, *, lambda_init):
    tq = tk = ATT_TQ
    i = pl.program_id(2)
    n_tiles = i + 1

    def attend(plain):
        q1 = q1_ref[0, 0]
        q2 = q2_ref[0, 0]
        _softmax_init(plain, m1_ref, acc1_ref)
        _softmax_init(plain, m2_ref, acc2_ref)

        def body(j, c):
            ks = pl.multiple_of(j * tk, tk)
            v1 = v_ref[0, 0, pl.ds(ks, tk), :]
            s1 = _dot_nt(q1, k1_ref[0, 0, pl.ds(ks, tk), :])
            s2 = _dot_nt(q2, k2_ref[0, 0, pl.ds(ks, tk), :])
            _softmax_step(plain, s1, v1, m1_ref, acc1_ref)
            _softmax_step(plain, s2, v1, m2_ref, acc2_ref)
            return c

        lax.fori_loop(0, n_tiles - 1, body, 0)

        ks = pl.multiple_of((n_tiles - 1) * tk, tk)
        half = tq // 2
        for r0, nk in ((0, half), (half, tk)):
            rows = pl.ds(r0, half)
            bias = diag_ref[r0:r0 + half, :nk]
            v1 = v_ref[0, 0, pl.ds(ks, nk), :]
            s1 = _dot_nt(q1[r0:r0 + half], k1_ref[0, 0, pl.ds(ks, nk), :]) + bias
            s2 = _dot_nt(q2[r0:r0 + half], k2_ref[0, 0, pl.ds(ks, nk), :]) + bias
            _softmax_step(plain, s1, v1, m1_ref.at[rows, :], acc1_ref.at[rows, :])
            _softmax_step(plain, s2, v1, m2_ref.at[rows, :], acc2_ref.at[rows, :])

    pl.when(plain_ref[0] == 1)(lambda: attend(True))
    pl.when(plain_ref[0] != 1)(lambda: attend(False))

    lv = lam_ref[...]
    lam = (jnp.exp(jnp.sum(lv[0:1] * lv[1:2], axis=-1, keepdims=True))
           - jnp.exp(jnp.sum(lv[2:3] * lv[3:4], axis=-1, keepdims=True)) + lambda_init)
    o = _softmax_result(acc1_ref, A_V_DIM) - lam * _softmax_result(acc2_ref, A_V_DIM)
    ms = jnp.mean(o * o, axis=-1, keepdims=True)
    o = o * lax.rsqrt(ms + RMS_EPS) * subg_ref[...] * (1.0 - lambda_init)
    o_ref[0] = o.astype(BF16)


def _diff_attention(plain, q1, q2, k1, k2, v1, a_lambda, sub_g, lambda_init):
    B, H, S, dq = q1.shape
    tq = ATT_TQ
    chunk = np.arange(tq) // CHUNK
    diag = jnp.asarray(np.where(chunk[None, :] <= chunk[:, None], 0.0, NEG_BIG), F32)
    qspec = pl.BlockSpec((1, 1, tq, dq), lambda b, h, i, p: (b, h, i, 0))
    kspec = pl.BlockSpec((1, 1, S, dq), lambda b, h, i, p: (b, h, 0, 0))
    col = lambda: pltpu.VMEM((tq, 1), F32)
    acc = lambda: pltpu.VMEM((tq, 2 * A_V_DIM), F32)
    grid_spec = pltpu.PrefetchScalarGridSpec(
        num_scalar_prefetch=1,
        grid=(B, H, S // tq),
        in_specs=[pl.BlockSpec((4, HEAD_DIM), lambda b, h, i, p: (0, 0)),
                  pl.BlockSpec((1, A_V_DIM), lambda b, h, i, p: (0, 0)),
                  pl.BlockSpec((tq, tq), lambda b, h, i, p: (0, 0)),
                  qspec, qspec, kspec, kspec,
                  pl.BlockSpec((1, 1, S, 2 * A_V_DIM), lambda b, h, i, p: (b, h, 0, 0))],
        out_specs=pl.BlockSpec((1, tq, A_V_DIM), lambda b, h, i, p: (b, i, h)),
        scratch_shapes=[col(), acc(), col(), acc()],
    )
    return pl.pallas_call(
        functools.partial(_diff_attn_kernel, lambda_init=lambda_init),
        grid_spec=grid_spec,
        out_shape=jax.ShapeDtypeStruct((B, S, H * A_V_DIM), BF16),
        compiler_params=_cparams(3),
    )(plain, a_lambda, sub_g.reshape(1, A_V_DIM), diag, q1, q2, k1, k2, v1)


def _dsa_kernel(plain_ref, iq_ref, iw_ref, ik_ref, q_ref, k_ref, lat_ref, wuv_ref, tri_ref, o_ref,
                score_ref, score_t_ref, thr_ref, m_ref, acc_ref, *, topk):
    tq, tk = DSA_TQ, DSA_TK
    nh = B_HEADS
    i = pl.program_id(1)
    n_tiles = ((i + 1) * tq + tk - 1) // tk

    hg = DSA_HEAD_GROUP
    iw = iw_ref[0]

    def score_tile(j, masked):
        ks = pl.multiple_of(j * tk, tk)
        ik = ik_ref[0, pl.ds(ks, tk), :]
        score = None
        for g in range(IDX_HEADS // hg):
            iq = iq_ref[0, g * hg:(g + 1) * hg].reshape(hg * tq, HEAD_DIM)
            rel = jnp.maximum(_dot_nt(iq, ik), 0.0).reshape(hg, tq, tk)
            for h in range(hg):
                term = rel[h] * iw[:, g * hg + h:g * hg + h + 1]
                score = term if score is None else score + term
        if masked:
            score = jnp.where(_chunk_mask(i * tq, ks, tq, tk), score, -jnp.inf)
        score_ref[:, pl.ds(ks, tk)] = score
        score_t_ref[pl.ds(ks, tk), :] = score.T

    def score_body(j, c):
        score_tile(j, False)
        return c

    lax.fori_loop(0, n_tiles - 1, score_body, 0)
    score_tile(n_tiles - 1, True)

    def image_to_float(t):
        return pltpu.bitcast(jnp.where(t < 0, t ^ 0x7FFFFFFF, t), F32)

    def count(pred, cand):
        parts = []
        for r0 in range(0, tq, COUNT_ROWS):
            rows = pl.ds(r0, COUNT_ROWS)
            cand_r = cand[r0:r0 + COUNT_ROWS]

            def body(j, acc, rows=rows, cand_r=cand_r):
                ks = pl.multiple_of(j * tk, tk)
                hit = jnp.where(pred(score_ref[rows, pl.ds(ks, tk)], cand_r), 1.0, 0.0)
                for c in range(tk // LANES):
                    acc = acc + hit[:, c * LANES:(c + 1) * LANES]
                return acc
            parts.append(lax.fori_loop(0, n_tiles, body, jnp.zeros((COUNT_ROWS, LANES), F32)))
        acc = jnp.concatenate(parts, axis=0)
        return jnp.sum(acc, axis=-1, keepdims=True)

    ge = lambda s, c: s >= c
    gt = lambda s, c: s > c

    def all_of(flags):
        return (jnp.min(flags) > 0.0).astype(I32)

    def count_t(pred, cand):
        def body(j, acc):
            ks = pl.multiple_of(j * WALK_KEYS, WALK_KEYS)
            hit = jnp.where(pred(score_t_ref[pl.ds(ks, WALK_KEYS), :], cand), 1.0, 0.0)
            return acc + jnp.sum(hit.reshape(-1, COUNT_CHAINS, 8, tq), axis=0)
        acc = lax.fori_loop(0, (i + 1) * (tq // WALK_KEYS), body,
                            jnp.zeros((COUNT_CHAINS, 8, tq), F32))
        return jnp.sum(jnp.sum(acc, axis=0), axis=0, keepdims=True)

    zero = jnp.zeros((1, tq), F32)
    n_ge0 = count_t(ge, zero)
    tie0 = jnp.where(jnp.logical_and(count_t(gt, zero) < topk, n_ge0 > topk), 1.0, 0.0)
    t0 = jnp.where(n_ge0 >= topk, 0, INT_MIN)
    settled0 = jnp.where(n_ge0 == topk, 1.0, tie0)

    def bit_body(state):
        step, t, settled, _ = state
        for _ in range(WALK_UNROLL):
            bit = jnp.where(step < 32, jnp.left_shift(jnp.int32(1), jnp.maximum(31 - step, 0)), 0)
            cand = t + bit
            n_ge = count_t(ge, image_to_float(cand))
            t = jnp.where(settled > 0.0, t, jnp.where(n_ge >= topk, cand, t))
            settled = jnp.where(n_ge == topk, 1.0, settled)
            step = step + 1
        return step, t, settled, all_of(settled)

    def bit_cond(state):
        step, _, _, all_settled = state
        return jnp.logical_and(step < 32, all_settled == 0)

    _, t, _, all_settled = lax.while_loop(
        bit_cond, bit_body, (jnp.int32(1), t0, settled0, all_of(settled0)))
    thr_row = jnp.where(t == INT_MIN, jnp.finfo(F32).min, image_to_float(t))
    thr_ref[...] = jnp.broadcast_to(thr_row, (LANES, tq)).T[:, 0:1]

    def min_where(pred, cand):
        def body(j, acc):
            ks = pl.multiple_of(j * tk, tk)
            s = score_ref[:, pl.ds(ks, tk)]
            s = jnp.where(pred(s, cand), s, jnp.inf)
            for c in range(tk // LANES):
                acc = jnp.minimum(acc, s[:, c * LANES:(c + 1) * LANES])
            return acc
        acc = lax.fori_loop(0, n_tiles, body, jnp.full((tq, LANES), jnp.inf, F32))
        return jnp.min(acc, axis=-1, keepdims=True)

    @pl.when(all_settled == 0)
    def _():
        thr0 = thr_ref[...]
        low = min_where(ge, thr0)
        nxt = min_where(gt, low)
        thr_ref[...] = jnp.where(count(ge, nxt) >= topk, nxt,
                                 jnp.where(low < jnp.inf, low, thr0))

    @pl.when(jnp.logical_or(all_settled == 0, jnp.max(tie0) > 0.0))
    def _():
        thr = thr_ref[...]
        quota = topk - count(gt, thr)

        def tie_body(j, carry):
            sl = pl.ds(pl.multiple_of(j * tk, tk), tk)
            score = score_ref[:, sl]
            eq = score == thr
            prefix = carry + jnp.dot(jnp.where(eq, 1.0, 0.0).astype(BF16), tri_ref[...],
                                     preferred_element_type=F32)
            score_ref[:, sl] = jnp.where(eq & (prefix > quota), -jnp.inf, score)
            return prefix[:, tk - 1:tk]

        lax.fori_loop(0, n_tiles, tie_body, jnp.zeros((tq, 1), F32))

    thr = thr_ref[...]

    def attend(plain):
        _softmax_init(plain, m_ref, acc_ref)

        def attn_body(j, c):
            ks = pl.multiple_of(j * tk, tk)
            k = k_ref[0, pl.ds(ks, tk), :]
            v1 = lat_ref[0, pl.ds(ks, tk), :]
            sel = score_ref[:, pl.ds(ks, tk)] >= thr
            for g in range(nh // hg):
                rows = pl.ds(g * hg * tq, hg * tq)
                q = q_ref[0, g * hg:(g + 1) * hg].reshape(hg * tq, LANES)
                s = _dot_nt(q, k).reshape(hg, tq, tk)
                s = jnp.where(sel[None], s, NEG_BIG).reshape(hg * tq, tk)
                _softmax_step(plain, s, v1, m_ref.at[rows, :], acc_ref.at[rows, :])
            return c

        lax.fori_loop(0, n_tiles, attn_body, 0)

    pl.when(plain_ref[0] == 1)(lambda: attend(True))
    pl.when(plain_ref[0] != 1)(lambda: attend(False))

    for h in range(nh):
        o = _softmax_result(acc_ref.at[pl.ds(h * tq, tq), :], B_LATENT).astype(BF16)
        oh = jnp.dot(o, wuv_ref[h], preferred_element_type=F32)
        o_ref[0, :, h * HEAD_DIM:(h + 1) * HEAD_DIM] = oh.astype(BF16)


def _dsa_attention(plain, iq, iw, ik, q, k, lat1, w_uv):
    B, nh, S, dq = q.shape
    dh = HEAD_DIM
    tq = DSA_TQ
    topk = min(IDX_TOPK_MAX, S // 4)
    col = np.arange(DSA_TK)
    tri = jnp.asarray(col[:, None] <= col[None, :], BF16)
    hspec = lambda w: pl.BlockSpec((1, nh, tq, w), lambda b, i, p: (b, 0, i, 0))
    kspec = lambda w: pl.BlockSpec((1, S, w), lambda b, i, p: (b, 0, 0))
    grid_spec = pltpu.PrefetchScalarGridSpec(
        num_scalar_prefetch=1,
        grid=(B, S // tq),
        in_specs=[hspec(dh),
                  pl.BlockSpec((1, tq, IDX_HEADS), lambda b, i, p: (b, i, 0)),
                  kspec(dh), hspec(dq), kspec(dq), kspec(2 * B_LATENT),
                  pl.BlockSpec((nh, B_LATENT, dh), lambda b, i, p: (0, 0, 0)),
                  pl.BlockSpec((DSA_TK, DSA_TK), lambda b, i, p: (0, 0))],
        out_specs=pl.BlockSpec((1, tq, nh * dh), lambda b, i, p: (b, i, 0)),
        scratch_shapes=[pltpu.VMEM((tq, S), F32),
                        pltpu.VMEM((S, tq), F32),
                        pltpu.VMEM((tq, 1), F32),
                        pltpu.VMEM((nh * tq, 1), F32),
                        pltpu.VMEM((nh * tq, 2 * B_LATENT), F32)],
    )
    return pl.pallas_call(
        functools.partial(_dsa_kernel, topk=topk),
        grid_spec=grid_spec,
        out_shape=jax.ShapeDtypeStruct((B, S, nh * dh), BF16),
        compiler_params=_cparams(2),
    )(plain, iq, iw, ik, q, k, lat1, w_uv.astype(BF16), tri)


def _out_kernel(a_ref, b_ref, x_ref, g1_ref, sh_ref, sc_ref, ng_ref, woa_ref, wob_ref, rwh_ref,
                rwl_ref, rb_ref, tri_ref, x1_ref, h2_ref, idx_ref, gate_ref, rank_ref, cnt_ref,
                carry_ref):
    first = jnp.logical_and(pl.program_id(0) == 0, pl.program_id(1) == 0)

    @pl.when(first)
    def _():
        carry_ref[...] = jnp.zeros(carry_ref.shape, F32)

    mix = (jnp.dot(a_ref[0], woa_ref[...], preferred_element_type=F32)
           + jnp.dot(b_ref[0], wob_ref[...], preferred_element_type=F32))
    x1 = x_ref[0] + g1_ref[0] * mix
    x1_ref[0] = x1
    ms = jnp.mean(x1 * x1, axis=-1, keepdims=True)
    h2 = x1 * lax.rsqrt(ms + RMS_EPS) * ng_ref[...]
    h2 = h2 * (1.0 + sc_ref[0]) + sh_ref[0]
    h2_ref[0] = h2

    h_top = pltpu.bitcast(pltpu.bitcast(h2, I32) & -65536, F32)
    h_hi = h_top.astype(BF16)
    h_lo = (h2 - h_top).astype(BF16)
    logits = (jnp.dot(h_hi, rwh_ref[...], preferred_element_type=F32)
              + jnp.dot(h_hi, rwl_ref[...], preferred_element_type=F32)
              + jnp.dot(h_lo, rwh_ref[...], preferred_element_type=F32)
              + jnp.dot(h_lo, rwl_ref[...], preferred_element_type=F32)) + rb_ref[...]
    ts = logits.shape[0]
    lane_i = lax.broadcasted_iota(I32, (ts, LANES), 1)
    lane = lane_i.astype(F32)
    neg_inf = jnp.float32(-jnp.inf)
    l = jnp.where(lane_i < N_EXPERTS, logits, neg_inf)
    vals, idxs = [], []
    for _ in range(TOP_K):
        m = jnp.max(l, axis=-1, keepdims=True)
        idx = jnp.min(jnp.where(l == m, lane, float(LANES)), axis=-1, keepdims=True)
        vals.append(m)
        idxs.append(idx)
        l = jnp.where(lane == idx, neg_inf, l)
    es = [jnp.exp(v - vals[0]) for v in vals]
    denom = es[0] + es[1] + es[2] + es[3]

    onehot = jnp.zeros((ts, LANES), F32)
    for idx in idxs:
        onehot = onehot + jnp.where(lane == idx, 1.0, 0.0)
    prefix = jnp.dot(tri_ref[...], onehot.astype(BF16), preferred_element_type=F32) + carry_ref[...]
    idx_out = jnp.zeros((ts, LANES), I32)
    gate_out = jnp.zeros((ts, LANES), F32)
    rank_out = jnp.zeros((ts, LANES), I32)
    for k in range(TOP_K):
        rank = jnp.sum(jnp.where(lane == idxs[k], prefix, 0.0), axis=-1, keepdims=True)
        idx_out = jnp.where(lane_i == k, idxs[k].astype(I32), idx_out)
        gate_out = jnp.where(lane_i == k, es[k] / denom, gate_out)
        rank_out = jnp.where(lane_i == k, rank.astype(I32), rank_out)
    idx_ref[0] = idx_out
    gate_ref[0] = gate_out
    rank_ref[0] = rank_out
    carry = carry_ref[...] + jnp.sum(onehot, axis=0, keepdims=True)
    carry_ref[...] = carry
    cnt_ref[...] = carry


def _out_and_route(a_out, b_out, x, g1, sh2, sc2, norm_g, w_out, router_w, router_b):
    B, S, D = x.shape
    ts = OUT_ROWS
    aw = a_out.shape[-1]
    w_bf = w_out.astype(BF16)
    rw = jnp.concatenate([router_w, jnp.zeros((D, LANES - N_EXPERTS), F32)], axis=1)
    rw_top = lax.bitcast_convert_type(lax.bitcast_convert_type(rw, I32) & -65536, F32)
    rw_hi = rw_top.astype(BF16)
    rw_lo = (rw - rw_top).astype(BF16)
    rb =jnp.concatenate([router_b, jnp.zeros((LANES - N_EXPERTS,), F32)]).reshape(1, LANES)
    r = np.arange(ts)
    tri = jnp.asarray(r[:, None] > r[None, :], BF16)
    row = lambda b, i: (b, 0, 0)
    full = lambda b, i: (0, 0)
    tok = lambda w: pl.BlockSpec((1, ts, w), lambda b, i: (b, i, 0))
    return pl.pallas_call(
        _out_kernel,
        grid=(B, S // ts),
        in_specs=[tok(aw), tok(D - aw), tok(D),
                  pl.BlockSpec((1, 1, D), row), pl.BlockSpec((1, 1, D), row),
                  pl.BlockSpec((1, 1, D), row), pl.BlockSpec((1, D), full),
                  pl.BlockSpec((aw, D), full), pl.BlockSpec((D - aw, D), full),
                  pl.BlockSpec((D, LANES), full), pl.BlockSpec((D, LANES), full),
                  pl.BlockSpec((1, LANES), full), pl.BlockSpec((ts, ts), full)],
        out_specs=[tok(D), tok(D), tok(LANES), tok(LANES), tok(LANES),
                   pl.BlockSpec((1, LANES), full)],
        out_shape=[jax.ShapeDtypeStruct((B, S, D), F32), jax.ShapeDtypeStruct((B, S, D), F32),
                   jax.ShapeDtypeStruct((B, S, LANES), I32),
                   jax.ShapeDtypeStruct((B, S, LANES), F32),
                   jax.ShapeDtypeStruct((B, S, LANES), I32),
                   jax.ShapeDtypeStruct((1, LANES), F32)],
        scratch_shapes=[pltpu.VMEM((1, LANES), F32)],
        compiler_params=_cparams(2),
    )(a_out, b_out, x, g1, sh2, sc2, norm_g.reshape(1, D), w_bf[:aw], w_bf[aw:], rw_hi, rw_lo, rb,
      tri)


def _row_copy_wait(src_ref, dst_ref, sem, rows):
    pltpu.make_async_copy(src_ref.at[pl.ds(0, rows), :], dst_ref.at[pl.ds(0, rows), :], sem).wait()


def _dispatch_kernel(fill_ref, dest_ref, h_ref, xbuf_ref, zero_ref, sem, zsem):
    rows = h_ref.shape[0]
    blk = zero_ref.shape[0]
    n_pad, n_used = fill_ref[2 * N_EXPERTS], fill_ref[2 * N_EXPERTS + 1]
    n_blocks = xbuf_ref.shape[0] // blk

    @pl.when(pl.program_id(0) == 0)
    def _():
        zero_ref[...] = jnp.zeros(zero_ref.shape, F32)
        zero_row = zero_ref.at[pl.ds(0, 1), :]

        def per_expert(e, c):
            start = fill_ref[e]

            def per_row(r, c2):
                pltpu.make_async_copy(zero_row, xbuf_ref.at[pl.ds(start + r, 1), :], zsem).start()
                return c2
            return lax.fori_loop(0, fill_ref[N_EXPERTS + e], per_row, c)

        lax.fori_loop(0, N_EXPERTS, per_expert, 0)

        def per_block(b, c):
            pltpu.make_async_copy(zero_ref, xbuf_ref.at[pl.ds(b * blk, blk), :], zsem).start()
            return c

        lax.fori_loop(n_used, n_blocks, per_block, 0)

        def wait_block(b, c):
            pltpu.make_async_copy(zero_ref, xbuf_ref.at[pl.ds(0, blk), :], zsem).wait()
            return c

        def wait_row(r, c):
            pltpu.make_async_copy(zero_row, xbuf_ref.at[pl.ds(0, 1), :], zsem).wait()
            return c

        lax.fori_loop(n_used - n_pad // blk, n_blocks, wait_block, 0)
        lax.fori_loop(0, lax.rem(n_pad, blk), wait_row, 0)

    def body(r, c):
        for k in range(TOP_K):
            d = dest_ref[0, 0, r * TOP_K + k]
            pltpu.make_async_copy(h_ref.at[pl.ds(r, 1), :], xbuf_ref.at[pl.ds(d, 1), :], sem).start()
        return c

    lax.fori_loop(0, rows, body, 0)
    for _ in range(TOP_K):
        _row_copy_wait(h_ref, xbuf_ref, sem, rows)


def _dispatch(h2, dest, fill, n_rows):
    T, D = h2.shape
    ts = min(DISPATCH_ROWS, T)
    grid_spec = pltpu.PrefetchScalarGridSpec(
        num_scalar_prefetch=1,
        grid=(T // ts,),
        in_specs=[pl.BlockSpec((1, 1, ts * TOP_K), lambda i, f: (i, 0, 0),
                               memory_space=pltpu.SMEM),
                  pl.BlockSpec((ts, D), lambda i, f: (i, 0))],
        out_specs=pl.BlockSpec(memory_space=pl.ANY),
        scratch_shapes=[pltpu.VMEM((MOE_ROWS, D), F32), pltpu.SemaphoreType.DMA(()),
                        pltpu.SemaphoreType.DMA(())],
    )
    return pl.pallas_call(
        _dispatch_kernel,
        grid_spec=grid_spec,
        out_shape=jax.ShapeDtypeStruct((n_rows, D), F32),
        compiler_params=_cparams(1),
    )(fill, dest.reshape(T // ts, 1, ts * TOP_K), h2)


def _moe_kernel(be_ref, nused_ref, x_ref, wg_ref, bg_ref, wu_ref, bu_ref, wd_ref, bd_ref, y_ref,
                wg_s, wu_s, wd_s):
    i = pl.program_id(0)
    used = i < nused_ref[0]
    new_expert = jnp.logical_or(i == 0, be_ref[i] != be_ref[jnp.maximum(i - 1, 0)])

    @pl.when(jnp.logical_and(used, new_expert))
    def _():
        wg_s[...] = wg_ref[0].astype(BF16)
        wu_s[...] = wu_ref[0].astype(BF16)
        wd_s[...] = wd_ref[0].astype(BF16)

    @pl.when(used)
    def _():
        xb = x_ref[...].astype(BF16)
        g = jnp.dot(xb, wg_s[...], preferred_element_type=F32) + bg_ref[0]
        u = jnp.dot(xb, wu_s[...], preferred_element_type=F32) + bu_ref[0]
        g = jnp.minimum(g, SWIGLU_LIMIT)
        u = jnp.clip(u, -SWIGLU_LIMIT, SWIGLU_LIMIT)
        a = g * (1.0 / (1.0 + jnp.exp(-SWIGLU_ALPHA * g))) * (u + 1.0)
        y_ref[...] = jnp.dot(a.astype(BF16), wd_s[...], preferred_element_type=F32) + bd_ref[0]

    @pl.when(jnp.logical_not(used))
    def _():
        y_ref[...] = jnp.zeros(y_ref.shape, F32)


def _moe_experts(xbuf, block_expert, n_used, w_gate, b_gate, w_up, b_up, w_down, b_down):
    R, D = xbuf.shape
    E, _, F = w_gate.shape
    G = MOE_ROWS
    nb = R // G
    last = lambda i, nu: jnp.maximum(jnp.minimum(i, nu[0] - 1), 0)
    blk = lambda i, be, nu: (last(i, nu), 0)
    wsel = lambda i, be, nu: (be[last(i, nu)], 0, 0)
    grid_spec = pltpu.PrefetchScalarGridSpec(
        num_scalar_prefetch=2,
        grid=(nb,),
        in_specs=[pl.BlockSpec((G, D), blk),
                  pl.BlockSpec((1, D, F), wsel), pl.BlockSpec((1, 1, F), wsel),
                  pl.BlockSpec((1, D, F), wsel), pl.BlockSpec((1, 1, F), wsel),
                  pl.BlockSpec((1, F, D), wsel), pl.BlockSpec((1, 1, D), wsel)],
        out_specs=pl.BlockSpec((G, D), lambda i, be, nu: (i, 0)),
        scratch_shapes=[pltpu.VMEM((D, F), BF16), pltpu.VMEM((D, F), BF16),
                        pltpu.VMEM((F, D), BF16)],
    )
    return pl.pallas_call(
        _moe_kernel,
        grid_spec=grid_spec,
        out_shape=jax.ShapeDtypeStruct((R, D), F32),
        compiler_params=_cparams(1),
    )(block_expert, n_used, xbuf, w_gate, b_gate.reshape(E, 1, F),
      w_up, b_up.reshape(E, 1, F), w_down, b_down.reshape(E, 1, D))


def _combine_kernel(dest_ref, gate_ref, x1_ref, g2_ref, ybuf_ref, o_ref, buf_ref, sem):
    rows = x1_ref.shape[0]

    def body(r, c):
        for k in range(TOP_K):
            d = dest_ref[0, 0, r * TOP_K + k]
            pltpu.make_async_copy(ybuf_ref.at[pl.ds(d, 1), :], buf_ref.at[k, pl.ds(r, 1), :],
                                  sem).start()
        return c

    lax.fori_loop(0, rows, body, 0)
    for k in range(TOP_K):
        _row_copy_wait(ybuf_ref, buf_ref.at[k], sem, rows)
    gate = gate_ref[...]
    y = buf_ref[0] * gate[:, 0:1]
    for k in range(1, TOP_K):
        y = y + buf_ref[k] * gate[:, k:k + 1]
    o_ref[...] = x1_ref[...] + g2_ref[0] * y


def _combine(ybuf, dest, gates, x1, g2, seq):
    T, D = x1.shape
    ts = COMBINE_ROWS
    per_seq = seq // ts
    return pl.pallas_call(
        _combine_kernel,
        grid=(T // ts,),
        in_specs=[pl.BlockSpec((1, 1, ts * TOP_K), lambda i: (i, 0, 0), memory_space=pltpu.SMEM),
                  pl.BlockSpec((ts, LANES), lambda i: (i, 0)),
                  pl.BlockSpec((ts, D), lambda i: (i, 0)),
                  pl.BlockSpec((1, 1, D), lambda i: (i // per_seq, 0, 0)),
                  pl.BlockSpec(memory_space=pl.ANY)],
        out_specs=pl.BlockSpec((ts, D), lambda i: (i, 0)),
        out_shape=jax.ShapeDtypeStruct((T, D), F32),
        scratch_shapes=[pltpu.VMEM((TOP_K, ts, D), F32), pltpu.SemaphoreType.DMA(())],
        compiler_params=_cparams(1),
    )(dest.reshape(T // ts, 1, ts * TOP_K), gates, x1, g2, ybuf)


def _layer(x, mod, lambda_init, norm1_g, norm2_g, w_in, w_out, a_q_norm_g, a_k_norm_g, a_lambda,
           a_sub_g, b_q_norm_g, b_k_norm_g, b_kv_norm_g, b_w_uv, router_w, router_b, w_gate,
           b_gate, w_up, b_up, w_down, b_down):
    B, S, D = x.shape
    T = B * S
    sh1, sc1, g1, sh2, sc2, g2 = [m.reshape(B, 1, D) for m in jnp.split(mod, 6, axis=-1)]

    (aq1, aq2, ak1, ak2, bq, bk, ik, iq, blat1, av1, iw), plain_a, plain_b = _project(
        x, sh1, sc1, norm1_g, w_in, a_q_norm_g, a_k_norm_g, b_q_norm_g, b_k_norm_g, b_kv_norm_g)
    a_out = _diff_attention(plain_a, aq1, aq2, ak1, ak2, av1, a_lambda, a_sub_g, lambda_init)
    b_out = _dsa_attention(plain_b, iq, iw, ik, bq, bk, blat1, b_w_uv)

    x1, h2, top_idx, gates, rank, counts = _out_and_route(
        a_out, b_out, x, g1, sh2, sc2, norm2_g, w_out, router_w, router_b)

    G = MOE_ROWS
    counts = counts[0, :N_EXPERTS].astype(I32)
    padded = ((counts + G - 1) // G) * G
    pcum = jnp.cumsum(padded)
    poff = pcum - padded
    nb = (T * TOP_K) // G + N_EXPERTS
    starts = jnp.arange(nb, dtype=I32) * G
    block_expert = jnp.minimum(
        jnp.sum((pcum[None, :] <= starts[:, None]).astype(I32), axis=1), N_EXPERTS - 1)
    n_used = (pcum[-1:] // G).astype(I32)
    top_idx = top_idx.reshape(T, LANES)[:, :TOP_K]
    dest = (poff[top_idx] + rank.reshape(T, LANES)[:, :TOP_K]).astype(I32).reshape(T * TOP_K)

    pad = padded - counts
    fill = jnp.concatenate([poff + counts, pad, jnp.sum(pad, keepdims=True), n_used]).astype(I32)
    xbuf = _dispatch(h2.reshape(T, D), dest, fill, nb * G)
    ybuf = _moe_experts(xbuf, block_expert.astype(I32), n_used, w_gate, b_gate, w_up, b_up,
                        w_down, b_down)
    out = _combine(ybuf, dest, gates.reshape(T, LANES), x1.reshape(T, D), g2, S)
    return out.reshape(B, S, D)


def kernel(x, c, norm1_g, norm2_g, w_ada, b_ada, w_in, w_out, a_q_norm_g, a_k_norm_g, a_lambda,
           a_sub_g, b_q_norm_g, b_k_norm_g, b_kv_norm_g, b_w_uv, router_w, router_b, w_gate,
           b_gate, w_up, b_up, w_down, b_down):
    depth = w_in.shape[0]
    for l in range(depth):
        lambda_init = 0.8 - 0.6 * math.exp(-0.3 * l)
        mod = _ada(c, w_ada[l], b_ada[l])
        x = _layer(x, mod, lambda_init, norm1_g[l], norm2_g[l], w_in[l], w_out[l], a_q_norm_g[l],
                   a_k_norm_g[l], a_lambda[l], a_sub_g[l], b_q_norm_g[l], b_k_norm_g[l],
                   b_kv_norm_g[l], b_w_uv[l], router_w[l], router_b[l], w_gate[l], b_gate[l],
                   w_up[l], b_up[l], w_down[l], b_down[l])
    return x
```
